```python
import math, functools
import jax, jax.numpy as jnp
from jax import lax
import numpy as np

D_MODEL = 1024
BATCH = 32
SEQ = 256
DEPTH = 1
DEC_BATCH = 8
DEC_SEQ = 1024
PAST_LEN = 256

GRID_W = 64
D_MIX = D_MODEL
D_ATTN = D_MIX // 2
D_HYENA = D_MIX - D_ATTN
HEAD_DIM = 64
N_HEADS = D_ATTN // HEAD_DIM
WIN_ROWS = 8
WIN_COLS = 16
Q_COLS = 16
K_COLS = Q_COLS + WIN_COLS
Q_BLOCK = 128
HYENA_ORDER = 2
SHORT_CONV = 3
FILTER_EMB = 33
FILTER_BANDS = (FILTER_EMB - 1) // 2
FILTER_HIDDEN = 64
DECAY_TARGET = 1e-2
SHORT_DECAY_PCT = 0.3
LONG_DECAY_PCT = 1.5
MOD_SHIFT = 0.05
N_EXPERTS = 16
EC_FACTOR = 2
EXPERT_FF = 2816
N_MOD = 6
EPS = 1e-6
NEG_INF = -1e30

kernel_name = 'hybrid_natten_hyena_ec_diffusion_step'


def rmsnorm(x, g):
    xf = x.astype(jnp.float32)
    y = xf * lax.rsqrt(jnp.mean(xf * xf, axis=-1, keepdims=True) + EPS)
    return y.astype(x.dtype) * g


def ada_mod(cond, w_ada, b_ada):
    m = jax.nn.silu(cond) @ w_ada + b_ada
    return jnp.split(m, N_MOD, axis=-1)


def to_heads(a):
    b, t, _ = a.shape
    return a.reshape(b, t, N_HEADS, HEAD_DIM).transpose(0, 2, 1, 3)


def context_attention(q, k, v):
    b, h, t, dh = q.shape
    nb = t // Q_BLOCK
    qb = q.reshape(b, h, nb, Q_BLOCK, dh).transpose(2, 0, 1, 3, 4)

    def block(q_i):
        s = jnp.einsum('bhqd,bhtd->bhqt', q_i, k).astype(jnp.float32)
        p = jax.nn.softmax(s, axis=-1).astype(v.dtype)
        return jnp.einsum('bhqt,bhtd->bhqd', p, v)

    o = lax.map(block, qb)
    return o.transpose(1, 2, 0, 3, 4).reshape(b, h, t, dh)


def latent_neighbourhood_attention(q, k, v, k_ctx, v_ctx, rpb):
    b, h, l, dh = q.shape
    rows = l // GRID_W
    wr = min(WIN_ROWS, rows)
    ncb = GRID_W // Q_COLS
    n_win = wr * K_COLS
    row_start = jnp.clip(jnp.arange(rows) - wr // 2, 0, rows - wr)
    q_col = jnp.arange(GRID_W).reshape(ncb, Q_COLS)
    win_start = jnp.clip(q_col - WIN_COLS // 2, 0, GRID_W - WIN_COLS)
    blk_start = jnp.clip(jnp.arange(ncb) * Q_COLS - WIN_COLS // 2, 0, GRID_W - K_COLS)
    key_col = blk_start[:, None] + jnp.arange(K_COLS)[None, :]
    kc = key_col[:, None, :]
    col_ok = (kc >= win_start[..., None]) & (kc < win_start[..., None] + WIN_COLS)
    dcol = jnp.clip(kc - q_col[..., None], -(WIN_COLS - 1), WIN_COLS - 1) + (WIN_COLS - 1)
    qg = q.reshape(b, h, rows, ncb, Q_COLS, dh)
    kg = k.reshape(b, h, rows, GRID_W, dh)
    vg = v.reshape(b, h, rows, GRID_W, dh)

    def one_row(r):
        rs = row_start[r]
        q_r = lax.dynamic_index_in_dim(qg, r, axis=2, keepdims=False)
        k_r = jnp.take(lax.dynamic_slice_in_dim(kg, rs, wr, axis=2), key_col, axis=3)
        v_r = jnp.take(lax.dynamic_slice_in_dim(vg, rs, wr, axis=2), key_col, axis=3)
        drow = rs + jnp.arange(wr) - r + (WIN_ROWS - 1)
        bias = rpb[:, drow[None, None, :, None], dcol[:, :, None, :]]
        s_win = jnp.einsum('bhjqd,bhijkd->bhjqik', q_r, k_r).astype(jnp.float32) + bias.astype(jnp.float32)
        s_win = jnp.where(col_ok[:, :, None, :], s_win, NEG_INF).reshape(b, h, ncb, Q_COLS, n_win)
        s_ctx = jnp.einsum('bhjqd,bhtd->bhjqt', q_r, k_ctx).astype(jnp.float32)
        p = jax.nn.softmax(jnp.concatenate([s_win, s_ctx], axis=-1), axis=-1).astype(v.dtype)
        p_win = p[..., :n_win].reshape(b, h, ncb, Q_COLS, wr, K_COLS)
        return (jnp.einsum('bhjqik,bhijkd->bhjqd', p_win, v_r)
                + jnp.einsum('bhjqt,bhtd->bhjqd', p[..., n_win:], v_ctx))

    o = lax.map(one_row, jnp.arange(rows))
    return o.transpose(1, 2, 0, 3, 4, 5).reshape(b, h, l, dh)


def implicit_filter_fft(length, w1, b1, w2, b2, w3, freq):
    f32 = jnp.float32
    t = jnp.linspace(0.0, 1.0, length, dtype=f32)[:, None]
    w = (2.0 * math.pi / length) * jnp.arange(length, dtype=f32)[:, None]
    bands = jnp.linspace(1e-4, FILTER_BANDS - 1, FILTER_BANDS, dtype=f32)[None, :]
    z = jnp.concatenate([t, jnp.cos(bands * w), -jnp.sin(bands * w)], axis=-1)
    fr = freq.astype(f32)
    hid = jnp.sin(fr * (z @ w1.astype(f32) + b1.astype(f32)))
    hid = jnp.sin(fr * (hid @ w2.astype(f32) + b2.astype(f32)))
    filt = (hid @ w3.astype(f32)).reshape(length, 2, HYENA_ORDER, D_HYENA)
    deltas = jnp.abs(jnp.linspace(math.log(DECAY_TARGET) / LONG_DECAY_PCT,
                                  math.log(DECAY_TARGET) / SHORT_DECAY_PCT, D_HYENA, dtype=f32))
    window = jnp.exp(-t * deltas) + MOD_SHIFT
    filt = filt * window[:, None, None, :]
    fwd, bwd = filt[:, 0], filt[:, 1]
    two_sided = jnp.concatenate([fwd, jnp.zeros((1, HYENA_ORDER, D_HYENA), f32), bwd[1:][::-1]], axis=0)
    two_sided = two_sided / jnp.sum(jnp.abs(two_sided), axis=0, keepdims=True)
    return jnp.fft.rfft(two_sided, axis=0)


def short_conv(u, w, b):
    up = jnp.pad(u, ((0, 0), (1, 1), (0, 0)))
    return up[:, :-2] * w[0] + up[:, 1:-1] * w[1] + up[:, 2:] * w[2] + b


def long_conv(u, k_f, bias):
    length = u.shape[1]
    uf = u.astype(jnp.float32)
    y = jnp.fft.irfft(jnp.fft.rfft(uf, n=2 * length, axis=1) * k_f[None], n=2 * length, axis=1)[:, :length]
    return y + uf * bias.astype(jnp.float32)


def hyena_mixer(u, conv_w, conv_b, k_f, hy_bias):
    uc = short_conv(u, conv_w, conv_b)
    v, x1, x2 = jnp.split(uc, 3, axis=-1)
    z = x1.astype(jnp.float32) * long_conv(v, k_f[:, 0], hy_bias[0])
    y = x2.astype(jnp.float32) * long_conv(z, k_f[:, 1], hy_bias[1])
    return y.astype(u.dtype)


def expert_choice_ffn(x, w_router, w1, w3, w2):
    b, t, d = x.shape
    xf = x.reshape(b * t, d)
    n = b * t
    cap = EC_FACTOR * n // N_EXPERTS
    aff = jax.nn.softmax((xf @ w_router).astype(jnp.float32), axis=-1)
    top_aff, top_idx = lax.top_k(aff.T, cap)
    xe = xf[top_idx]
    hid = jax.nn.silu(jnp.einsum('ecd,edf->ecf', xe, w1)) * jnp.einsum('ecd,edf->ecf', xe, w3)
    ye = jnp.einsum('ecf,efd->ecd', hid, w2) * top_aff[..., None].astype(xf.dtype)
    out = jnp.zeros_like(xf).at[top_idx.reshape(-1)].add(ye.reshape(-1, d))
    return out.reshape(b, t, d)


def trunk_layer(x, cond, attend, k_f, norm1_g, norm2_g, w_ada, b_ada, w_in, w_out, out_g_attn, out_g_hyena,
                conv_w, conv_b, hy_bias, w_router, w1, w3, w2):
    b, t, _ = x.shape
    sh1, sc1, g1, sh2, sc2, g2 = ada_mod(cond, w_ada, b_ada)
    h = rmsnorm(x, norm1_g) * (1.0 + sc1) + sh1
    proj = h @ w_in
    q, k, v, hy = jnp.split(proj, [D_ATTN, 2 * D_ATTN, 3 * D_ATTN], axis=-1)
    k = to_heads(k)
    v = to_heads(v)
    a = attend(to_heads(q) * (HEAD_DIM ** -0.5), k, v)
    a = a.transpose(0, 2, 1, 3).reshape(b, t, D_ATTN)
    y_h = hyena_mixer(hy, conv_w, conv_b, k_f, hy_bias)
    mix = jnp.concatenate([rmsnorm(a, out_g_attn), rmsnorm(y_h, out_g_hyena)], axis=-1) @ w_out
    x = x + g1 * mix
    h2 = rmsnorm(x, norm2_g) * (1.0 + sc2) + sh2
    x = x + g2 * expert_choice_ffn(h2, w_router, w1, w3, w2)
    return x, k, v


def setup_inputs(seed: int = 0) -> dict:
    key = jax.random.key(seed)
    ks = jax.random.split(key, 32)
    d = D_MODEL
    c3 = 3 * D_HYENA

    def nrm(k, shape, scale):
        return scale * jax.random.normal(k, shape, jnp.float32)

    return {
        'x_prompt': nrm(ks[0], (BATCH, SEQ, d), 1.0),
        'x_sample': nrm(ks[1], (DEC_BATCH, DEC_SEQ, d), 1.0),
        'cache_k': nrm(ks[2], (DEC_BATCH, DEPTH, N_HEADS, PAST_LEN, HEAD_DIM), 1.0),
        'cache_v': nrm(ks[3], (DEC_BATCH, DEPTH, N_HEADS, PAST_LEN, HEAD_DIM), 1.0),
        'c': nrm(ks[4], (DEC_BATCH, d), 1.0),
        'c_ctx': nrm(ks[5], (d,), 1.0),
        'norm1_g': 1.0 + nrm(ks[6], (DEPTH, d), 0.01),
        'norm2_g': 1.0 + nrm(ks[7], (DEPTH, d), 0.01),
        'w_ada': nrm(ks[8], (DEPTH, d, N_MOD * d), 0.5 * d ** -0.5),
        'b_ada': nrm(ks[9], (DEPTH, N_MOD * d), 0.02),
        'w_in': nrm(ks[10], (DEPTH, d, 3 * D_ATTN + c3), d ** -0.5),
        'w_out': nrm(ks[11], (DEPTH, D_MIX, d), D_MIX ** -0.5),
        'out_g_attn': 1.0 + nrm(ks[12], (DEPTH, D_ATTN), 0.01),
        'out_g_hyena': 1.0 + nrm(ks[13], (DEPTH, D_HYENA), 0.01),
        'rpb': nrm(ks[14], (DEPTH, N_HEADS, 2 * WIN_ROWS - 1, 2 * WIN_COLS - 1), 0.5),
        'conv_w': nrm(ks[15], (DEPTH, SHORT_CONV, c3), SHORT_CONV ** -0.5),
        'conv_b': nrm(ks[16], (DEPTH, c3), 0.02),
        'filt_w1': nrm(ks[17], (DEPTH, FILTER_EMB, FILTER_HIDDEN), FILTER_EMB ** -0.5),
        'filt_b1': nrm(ks[18], (DEPTH, FILTER_HIDDEN), 0.5),
        'filt_w2': nrm(ks[19], (DEPTH, FILTER_HIDDEN, FILTER_HIDDEN), FILTER_HIDDEN ** -0.5),
        'filt_b2': nrm(ks[20], (DEPTH, FILTER_HIDDEN), 0.5),
        'filt_w3': nrm(ks[21], (DEPTH, FILTER_HIDDEN, 2 * HYENA_ORDER * D_HYENA), FILTER_HIDDEN ** -0.5),
        'filt_freq': 1.0 + nrm(ks[22], (DEPTH, FILTER_HIDDEN), 0.01),
        'hyena_bias': nrm(ks[23], (DEPTH, HYENA_ORDER, D_HYENA), 0.5),
        'w_router': nrm(ks[24], (DEPTH, d, N_EXPERTS), d ** -0.5),
        'w1': nrm(ks[25], (DEPTH, N_EXPERTS, d, EXPERT_FF), d ** -0.5),
        'w3': nrm(ks[26], (DEPTH, N_EXPERTS, d, EXPERT_FF), d ** -0.5),
        'w2': nrm(ks[27], (DEPTH, N_EXPERTS, EXPERT_FF, d), EXPERT_FF ** -0.5),
        'final_g': 1.0 + nrm(ks[28], (d,), 0.01),
    }


def reference(x_prompt, x_sample, cache_k, cache_v, c, c_ctx, norm1_g, norm2_g, w_ada, b_ada, w_in, w_out,
              out_g_attn, out_g_hyena, rpb, conv_w, conv_b, filt_w1, filt_b1, filt_w2, filt_b2, filt_w3,
              filt_freq, hyena_bias, w_router, w1, w3, w2, final_g):
    cond_ctx = c_ctx[None, None, :]
    cond_lat = c[:, None, :]
    xp, xs = x_prompt, x_sample
    keys, vals = [], []
    for l in range(DEPTH):
        shared = (norm1_g[l], norm2_g[l], w_ada[l], b_ada[l], w_in[l], w_out[l], out_g_attn[l], out_g_hyena[l],
                  conv_w[l], conv_b[l], hyena_bias[l], w_router[l], w1[l], w3[l], w2[l])
        filt = (filt_w1[l], filt_b1[l], filt_w2[l], filt_b2[l], filt_w3[l], filt_freq[l])
        kf_ctx = implicit_filter_fft(xp.shape[1], *filt)
        kf_lat = implicit_filter_fft(xs.shape[1], *filt)
        xp, k_ctx, v_ctx = trunk_layer(xp, cond_ctx, context_attention, kf_ctx, *shared)
        keys.append(k_ctx)
        vals.append(v_ctx)
        attend = functools.partial(latent_neighbourhood_attention, k_ctx=cache_k[:, l], v_ctx=cache_v[:, l],
                                   rpb=rpb[l])
        xs, _, _ = trunk_layer(xs, cond_lat, attend, kf_lat, *shared)
    y_prompt = rmsnorm(xp, final_g)
    y_sample = rmsnorm(xs, final_g)
    state_k = jnp.stack(keys, axis=1)
    state_v = jnp.stack(vals, axis=1)
    return (y_prompt, y_sample, state_k, state_v)
```

```python
import functools
import math

import numpy as np
import jax
import jax.numpy as jnp
from jax import lax
from jax.experimental import pallas as pl
from jax.experimental.pallas import tpu as pltpu

D_MODEL = 1024
GRID_W = 64
D_ATTN = 512
D_HYENA = 512
HEAD_DIM = 64
N_HEADS = 8
WIN_ROWS = 8
WIN_COLS = 16
FILTER_EMB = 33
FILTER_BANDS = 16
FILTER_HIDDEN = 64
DECAY_TARGET = 1e-2
SHORT_DECAY_PCT = 0.3
LONG_DECAY_PCT = 1.5
MOD_SHIFT = 0.05
N_EXPERTS = 16
EC_FACTOR = 2
EXPERT_FF = 2816
N_MOD = 6
EPS = 1e-6
NEG_INF = -1e30

LANES = 128
MOD_ROWS = 16
VMEM_LIMIT = 56 * 1024 * 1024
HI = lax.Precision.HIGHEST
BF16 = jnp.bfloat16
F32 = jnp.float32

_NT = (((1,), (1,)), ((), ()))


def _params(*sem):
    return pltpu.CompilerParams(dimension_semantics=sem, vmem_limit_bytes=VMEM_LIMIT)


def _rms(x, n):
    return x * lax.rsqrt(jnp.sum(x * x, axis=-1, keepdims=True) * (1.0 / n) + EPS)


def _silu(x):
    return x * (1.0 / (1.0 + jnp.exp(-x)))


def _ada_kernel(c_ref, w_ref, b_ref, o_ref):
    s = _silu(c_ref[...])
    o_ref[...] = jnp.dot(s, w_ref[...], precision=HI, preferred_element_type=F32) + b_ref[...]


def ada_mod(cond, w_ada, b_ada):
    n = w_ada.shape[1]
    tn = 1024
    return pl.pallas_call(
        _ada_kernel,
        grid=(n // tn,),
        in_specs=[pl.BlockSpec((MOD_ROWS, D_MODEL), lambda j: (0, 0)),
                  pl.BlockSpec((D_MODEL, tn), lambda j: (0, j)),
                  pl.BlockSpec((1, tn), lambda j: (0, j))],
        out_specs=pl.BlockSpec((MOD_ROWS, tn), lambda j: (0, j)),
        out_shape=jax.ShapeDtypeStruct((MOD_ROWS, n), F32),
        compiler_params=_params("arbitrary"),
        name="ada_mod",
    )(cond, w_ada, b_ada.reshape(1, n))


def _inproj_kernel(x_ref, mod_ref, g_ref, w_ref, qkv_ref, hy_ref, *state_refs, nb, tt):
    x = x_ref[...].reshape(nb * tt, D_MODEL)
    mod = mod_ref[0]
    sh1 = mod[:, 0:D_MODEL]
    sc1 = mod[:, D_MODEL:2 * D_MODEL]
    h = _rms(x, D_MODEL) * g_ref[...] * (1.0 + sc1) + sh1
    proj = jnp.dot(h.astype(BF16), w_ref[...], preferred_element_type=F32)
    q = proj[:, 0:D_ATTN] * (HEAD_DIM ** -0.5)
    k = proj[:, D_ATTN:2 * D_ATTN]
    v = proj[:, 2 * D_ATTN:3 * D_ATTN]
    qkv_ref[:, :, 0:D_ATTN] = q.reshape(nb, tt, D_ATTN)
    qkv_ref[:, :, D_ATTN:2 * D_ATTN] = k.reshape(nb, tt, D_ATTN)
    qkv_ref[:, :, 2 * D_ATTN:3 * D_ATTN] = v.reshape(nb, tt, D_ATTN)
    hy_ref[...] = proj[:, 3 * D_ATTN:].reshape(nb, tt, 3 * D_HYENA)
    if state_refs:
        sk_ref, sv_ref = state_refs
        for bi in range(nb):
            for hd in range(N_HEADS):
                sl = slice(hd * HEAD_DIM, (hd + 1) * HEAD_DIM)
                sk_ref[bi, hd] = k[bi * tt:(bi + 1) * tt, sl]
                sv_ref[bi, hd] = v[bi * tt:(bi + 1) * tt, sl]


def in_projection(x, mods3, norm_g, w_in_bf16, *, nb, tt, mod_row, with_state):
    b, t, d = x.shape
    steps_per_b = t // tt
    grid = (b // nb * steps_per_b,)
    xmap = lambda i: (i // steps_per_b, i % steps_per_b, 0)
    nproj = w_in_bf16.shape[1]
    out_shape = [jax.ShapeDtypeStruct((b, t, 3 * D_ATTN), F32),
                 jax.ShapeDtypeStruct((b, t, 3 * D_HYENA), F32)]
    out_specs = [pl.BlockSpec((nb, tt, 3 * D_ATTN), xmap),
                 pl.BlockSpec((nb, tt, 3 * D_HYENA), xmap)]
    if with_state:
        assert steps_per_b == 1
        smap = lambda i: (i, 0, 0, 0)
        for _ in range(2):
            out_shape.append(jax.ShapeDtypeStruct((b, N_HEADS, t, HEAD_DIM), F32))
            out_specs.append(pl.BlockSpec((nb, N_HEADS, tt, HEAD_DIM), smap))
    return pl.pallas_call(
        functools.partial(_inproj_kernel, nb=nb, tt=tt),
        grid=grid,
        in_specs=[pl.BlockSpec((nb, tt, d), xmap),
                  pl.BlockSpec((1, 1, N_MOD * d), lambda i: (mod_row(i), 0, 0)),
                  pl.BlockSpec((1, d), lambda i: (0, 0)),
                  pl.BlockSpec((d, nproj), lambda i: (0, 0))],
        out_specs=out_specs,
        out_shape=out_shape,
        compiler_params=_params("arbitrary"),
        name="in_projection",
    )(x, mods3, norm_g.reshape(1, d), w_in_bf16)


def _lane_mask(shape, half):
    lane = lax.broadcasted_iota(jnp.int32, shape, 1)
    return (lane < HEAD_DIM) if half == 0 else (lane >= HEAD_DIM)


def _ctx_attn_kernel(qkv_ref, o_ref):
    for j in range(N_HEADS // 2):
        lo = j * LANES
        qp = qkv_ref[0, :, lo:lo + LANES]
        kp = qkv_ref[0, :, D_ATTN + lo:D_ATTN + lo + LANES].astype(BF16)
        vp = qkv_ref[0, :, 2 * D_ATTN + lo:2 * D_ATTN + lo + LANES]
        out = None
        for half in range(2):
            m = _lane_mask(qp.shape, half)
            qm = jnp.where(m, qp, 0.0).astype(BF16)
            vm = jnp.where(m, vp, 0.0).astype(BF16)
            s = lax.dot_general(qm, kp, _NT, preferred_element_type=F32)
            p = jnp.exp(s - jnp.max(s, axis=-1, keepdims=True))
            l = jnp.sum(p, axis=-1, keepdims=True)
            o = jnp.dot(p.astype(BF16), vm, preferred_element_type=F32) * (1.0 / l)
            out = o if out is None else out + o
        o_ref[0, :, lo:lo + LANES] = out


def context_attention(qkv):
    b, t, _ = qkv.shape
    return pl.pallas_call(
        _ctx_attn_kernel,
        grid=(b,),
        in_specs=[pl.BlockSpec((1, t, 3 * D_ATTN), lambda i: (i, 0, 0))],
        out_specs=pl.BlockSpec((1, t, D_ATTN), lambda i: (i, 0, 0)),
        out_shape=jax.ShapeDtypeStruct((b, t, D_ATTN), F32),
        compiler_params=_params("arbitrary"),
        name="context_attention",
    )(qkv)


N_TPAIR = 2 * WIN_ROWS - 2
RPB_PAD_L = GRID_W - WIN_COLS
RPB_PAD_R = LANES - RPB_PAD_L - (2 * WIN_COLS - 1)


def _row_start(r, rows, wr):
    return min(max(r - wr // 2, 0), rows - wr)


def _build_bias(rpb_ref, tp_ref, m_ref, rows, wr):
    shape = (GRID_W, LANES)
    c = lax.broadcasted_iota(jnp.int32, shape, 0)
    lane = lax.broadcasted_iota(jnp.int32, shape, 1)
    kc = lane & (GRID_W - 1)
    ws = jnp.clip(c - WIN_COLS // 2, 0, GRID_W - WIN_COLS)
    col_ok = (kc >= ws) & (kc < ws + WIN_COLS)
    first = lane < GRID_W
    for hh in range(2):
        for dr in range(N_TPAIR):
            xa = jnp.broadcast_to(rpb_ref[hh, dr:dr + 1, :], shape)
            xb = jnp.broadcast_to(rpb_ref[hh, dr + 1:dr + 2, :], shape)
            ta = pltpu.roll(xa, GRID_W + 1, 1, stride=1, stride_axis=0)
            tb = pltpu.roll(xb, 1, 1, stride=1, stride_axis=0)
            tp_ref[hh, dr] = jnp.where(col_ok, jnp.where(first, ta, tb), NEG_INF)
        for r in range(rows):
            rs = _row_start(r, rows, wr)
            for jj in range(rows // 2):
                in_a = rs <= 2 * jj < rs + wr
                in_b = rs <= 2 * jj + 1 < rs + wr
                dr = 2 * jj - r + (WIN_ROWS - 1)
                if in_a and in_b:
                    blk = tp_ref[hh, dr]
                elif in_a:
                    blk = jnp.where(first, tp_ref[hh, dr], NEG_INF)
                elif in_b:
                    blk = jnp.where(first, NEG_INF, tp_ref[hh, dr])
                else:
                    blk = jnp.full(shape, NEG_INF, F32)
                m_ref[hh, r * GRID_W:(r + 1) * GRID_W, jj * LANES:(jj + 1) * LANES] = blk


def _key_range(qb, rows_per_blk, rows, wr):
    r0 = qb * rows_per_blk
    lo = min(_row_start(r, rows, wr) for r in range(r0, r0 + rows_per_blk))
    hi = max(_row_start(r, rows, wr) for r in range(r0, r0 + rows_per_blk)) + wr
    grp = 256 // GRID_W
    return (lo // grp) * 256, -(-hi // grp) * 256


def _lat_attn_kernel(q_ref, k_ref, v_ref, ck_ref, cv_ref, rpb_ref, o_ref, tp_ref, m_ref, *, rows, wr):
    @pl.when(pl.program_id(1) == 0)
    def _():
        _build_bias(rpb_ref, tp_ref, m_ref, rows, wr)

    l_tok = rows * GRID_W
    qblk = 256
    rows_per_blk = qblk // GRID_W
    sel_r = lax.broadcasted_iota(jnp.int32, (HEAD_DIM, LANES), 0)
    sel_c = lax.broadcasted_iota(jnp.int32, (HEAD_DIM, LANES), 1)
    for qb in range(l_tok // qblk):
        c0, c1 = _key_range(qb, rows_per_blk, rows, wr)
        qp = q_ref[0, qb * qblk:(qb + 1) * qblk, :]
        kp = k_ref[0, c0:c1, :].astype(BF16)
        vp = v_ref[0, c0:c1, :]
        out = None
        for hh in range(2):
            place = (sel_c == sel_r + hh * HEAD_DIM).astype(BF16)
            ck = jnp.dot(ck_ref[0, 0, hh].astype(BF16), place, preferred_element_type=F32).astype(BF16)
            cv = jnp.dot(cv_ref[0, 0, hh].astype(BF16), place, preferred_element_type=F32).astype(BF16)
            m = _lane_mask(qp.shape, hh)
            qm = jnp.where(m, qp, 0.0).astype(BF16)
            vm = jnp.where(_lane_mask(vp.shape, hh), vp, 0.0).astype(BF16)
            s_win = lax.dot_general(qm, kp, _NT, preferred_element_type=F32)
            s_win = s_win + m_ref[hh, qb * qblk:(qb + 1) * qblk, c0:c1]
            s_ctx = lax.dot_general(qm, ck, _NT, preferred_element_type=F32)
            mx = jnp.maximum(jnp.max(s_win, axis=-1, keepdims=True), jnp.max(s_ctx, axis=-1, keepdims=True))
            p_win = jnp.exp(s_win - mx)
            p_ctx = jnp.exp(s_ctx - mx)
            l = jnp.sum(p_win, axis=-1, keepdims=True) + jnp.sum(p_ctx, axis=-1, keepdims=True)
            o = (jnp.dot(p_win.astype(BF16), vm, preferred_element_type=F32)
                 + jnp.dot(p_ctx.astype(BF16), cv, preferred_element_type=F32)) * (1.0 / l)
            out = o if out is None else out + o
        o_ref[0, qb * qblk:(qb + 1) * qblk, :] = out


def latent_attention(qkv, cache_k, cache_v, rpb_ext):
    b, l_tok, _ = qkv.shape
    rows = l_tok // GRID_W
    wr = min(WIN_ROWS, rows)
    assert rows % 4 == 0 and wr == WIN_ROWS
    npair = N_HEADS // 2
    past = cache_k.shape[3]
    blk = lambda off: pl.BlockSpec((1, l_tok, LANES), lambda j, i: (i, 0, off + j))
    cspec = pl.BlockSpec((1, 1, 2, past, HEAD_DIM), lambda j, i: (i, 0, j, 0, 0))
    return pl.pallas_call(
        functools.partial(_lat_attn_kernel, rows=rows, wr=wr),
        grid=(npair, b),
        in_specs=[blk(0), blk(npair), blk(2 * npair), cspec, cspec,
                  pl.BlockSpec((2, 2 * WIN_ROWS - 1, LANES), lambda j, i: (j, 0, 0))],
        out_specs=pl.BlockSpec((1, l_tok, LANES), lambda j, i: (i, 0, j)),
        out_shape=jax.ShapeDtypeStruct((b, l_tok, D_ATTN), F32),
        scratch_shapes=[pltpu.VMEM((2, N_TPAIR, GRID_W, LANES), F32),
                        pltpu.VMEM((2, l_tok, l_tok), F32)],
        compiler_params=_params("arbitrary", "arbitrary"),
        name="latent_attention",
    )(qkv, qkv, qkv, cache_k, cache_v, rpb_ext)


def _dft_tables(length):
    n2 = 2 * length
    k = np.arange(length, dtype=np.float64)[:, None]
    t = np.arange(length, dtype=np.float64)[None, :]
    ang = 2.0 * np.pi * ((k * t) % n2) / n2
    fa = np.cos(ang)
    fb = -np.sin(ang)
    fb[0, :] = np.cos(np.pi * t[0])
    fwd = np.concatenate([fa, fb], axis=0)
    ga = 2.0 * np.cos(ang).T
    ga[:, 0] = 1.0
    gb = -2.0 * np.sin(ang).T
    gb[:, 0] = np.cos(np.pi * t[0])
    inv = np.concatenate([ga, gb], axis=1)
    return fwd.astype(np.float32), inv.astype(np.float32)


def _filter_consts(length):
    t = np.linspace(0.0, 1.0, length, dtype=np.float32)[:, None]
    w = (np.float32(2.0 * math.pi / length) * np.arange(length, dtype=np.float32))[:, None]
    bands = np.linspace(1e-4, FILTER_BANDS - 1, FILTER_BANDS, dtype=np.float32)[None, :]
    z = np.concatenate([t, np.cos(bands * w), -np.sin(bands * w)], axis=-1).astype(np.float32)
    zp = np.zeros((length, LANES), np.float32)
    zp[:, :FILTER_EMB] = z
    deltas = np.abs(np.linspace(math.log(DECAY_TARGET) / LONG_DECAY_PCT,
                                math.log(DECAY_TARGET) / SHORT_DECAY_PCT, D_HYENA, dtype=np.float32))
    window = (np.exp(-t * deltas) + np.float32(MOD_SHIFT)).astype(np.float32)
    return zp, window


def _split3(a, b):
    a_hi = a.astype(BF16)
    a_lo = (a - a_hi.astype(F32)).astype(BF16)
    b_hi = b.astype(BF16)
    b_lo = (b - b_hi.astype(F32)).astype(BF16)
    return (jnp.dot(a_hi, b_hi, preferred_element_type=F32)
            + jnp.dot(a_hi, b_lo, preferred_element_type=F32)
            + jnp.dot(a_lo, b_hi, preferred_element_type=F32))


def _filter_kernel(z_ref, win_ref, w1_ref, b1_ref, w2_ref, b2_ref, w3_ref, fr_ref, fwd_ref,
                   ka_ref, kb_ref, s_ref, d_ref, *, length):
    j = pl.program_id(0)

    @pl.when(j == 0)
    def _():
        fr = fr_ref[...]
        hid = jnp.sin(fr * (jnp.dot(z_ref[...], w1_ref[...], precision=HI, preferred_element_type=F32)
                            + b1_ref[...]))
        hid = jnp.sin(fr * (jnp.dot(hid, w2_ref[...], precision=HI, preferred_element_type=F32)
                            + b2_ref[...]))
        filt = jnp.dot(hid, w3_ref[...], precision=HI, preferred_element_type=F32)
        nc = 2 * D_HYENA
        win = win_ref[...]
        win2 = jnp.concatenate([win, win], axis=1)
        fw = filt[:, 0:nc] * win2
        row = lax.broadcasted_iota(jnp.int32, (length, nc), 0)
        bw = jnp.where(row == 0, 0.0, filt[:, nc:2 * nc] * win2)
        inv = 1.0 / jnp.sum(jnp.abs(fw) + jnp.abs(bw), axis=0, keepdims=True)
        s_ref[...] = (fw + bw) * inv
        d_ref[...] = (fw - bw) * inv

    tm = ka_ref.shape[0]
    nsteps = length // tm
    fa = fwd_ref[0]
    fb = fwd_ref[1]
    ka_ref[...] = _split3(fa, s_ref[...])
    kb = _split3(fb, d_ref[...])
    nyq = _split3(fb[0:8, :], s_ref[...])[0:1, :]
    row = lax.broadcasted_iota(jnp.int32, kb.shape, 0)
    kb_ref[...] = jnp.where((row == 0) & (j == 0), nyq, kb)
    del nsteps


def hyena_filter(length, w1p, b1, w2, b2, w3, freq, fwd_tab):
    zp, window = _filter_consts(length)
    nc = 2 * D_HYENA
    tm = 256
    fwd3 = fwd_tab.reshape(2, length, length)
    full = lambda a: pl.BlockSpec(a.shape, lambda j: (0,) * a.ndim)
    args = [jnp.asarray(zp), jnp.asarray(window), w1p, b1.reshape(1, -1), w2, b2.reshape(1, -1), w3,
            freq.reshape(1, -1)]
    return pl.pallas_call(
        functools.partial(_filter_kernel, length=length),
        grid=(length // tm,),
        in_specs=[full(a) for a in args] + [pl.BlockSpec((2, tm, length), lambda j: (0, j, 0))],
        out_specs=[pl.BlockSpec((tm, nc), lambda j: (j, 0))] * 2,
        out_shape=[jax.ShapeDtypeStruct((length, nc), F32)] * 2,
        scratch_shapes=[pltpu.VMEM((length, nc), F32), pltpu.VMEM((length, nc), F32)],
        compiler_params=_params("arbitrary"),
        name="hyena_filter",
    )(*args, fwd3)


def _long_conv(u, f_ref, g_ref, ka, kb, bias, length):
    x = jnp.dot(f_ref[...], u.astype(BF16), preferred_element_type=F32)
    xa = x[0:length]
    xb = x[length:2 * length]
    row = lax.broadcasted_iota(jnp.int32, xa.shape, 0)
    dc = row == 0
    ya = xa * ka - jnp.where(dc, 0.0, xb * kb)
    yb = jnp.where(dc, xb * kb, xa * kb + xb * ka)
    y = (jnp.dot(g_ref[:, 0:length], ya.astype(BF16), preferred_element_type=F32)
         + jnp.dot(g_ref[:, length:2 * length], yb.astype(BF16), preferred_element_type=F32))
    return y * (1.0 / (2 * length)) + u * bias


def _hyena_kernel(v_ref, x1_ref, x2_ref, cw_ref, cb_ref, f_ref, g_ref, ka_ref, kb_ref, hb_ref, o_ref,
                  *, length, tc):
    row = lax.broadcasted_iota(jnp.int32, (length, tc), 0)

    def short(u_ref, part):
        u = u_ref[0]
        up = jnp.where(row == 0, 0.0, pltpu.roll(u, 1, 0))
        dn = jnp.where(row == length - 1, 0.0, pltpu.roll(u, length - 1, 0))
        w = cw_ref[part]
        return up * w[0:1] + u * w[1:2] + dn * w[2:3] + cb_ref[part]

    v = short(v_ref, 0)
    x1 = short(x1_ref, 1)
    x2 = short(x2_ref, 2)
    z = x1 * _long_conv(v, f_ref, g_ref, ka_ref[:, 0:tc], kb_ref[:, 0:tc], hb_ref[0], length)
    y = x2 * _long_conv(z, f_ref, g_ref, ka_ref[:, tc:2 * tc], kb_ref[:, tc:2 * tc], hb_ref[1], length)
    o_ref[0] = y


def hyena_mixer(hy, conv_w, conv_b, fwd_bf16, inv_bf16, ka, kb, hy_bias):
    b, length, _ = hy.shape
    tc = 256
    nct = D_HYENA // tc
    regroup = lambda a: a.reshape(length, 2, nct, tc).transpose(2, 0, 1, 3).reshape(nct, length, 2 * tc)
    cw = conv_w.reshape(3, 3, nct, 1, tc).transpose(1, 2, 0, 3, 4).reshape(3, nct, 3, tc)
    cb = conv_b.reshape(3, nct, 1, tc)
    hb = hy_bias.reshape(2, nct, 1, tc).transpose(1, 0, 2, 3)
    part = lambda p: pl.BlockSpec((1, length, tc), lambda c, i: (i, 0, p * nct + c))
    return pl.pallas_call(
        functools.partial(_hyena_kernel, length=length, tc=tc),
        grid=(nct, b),
        in_specs=[part(0), part(1), part(2),
                  pl.BlockSpec((3, None, 3, tc), lambda c, i: (0, c, 0, 0)),
                  pl.BlockSpec((3, None, 1, tc), lambda c, i: (0, c, 0, 0)),
                  pl.BlockSpec((2 * length, length), lambda c, i: (0, 0)),
                  pl.BlockSpec((length, 2 * length), lambda c, i: (0, 0)),
                  pl.BlockSpec((None, length, 2 * tc), lambda c, i: (c, 0, 0)),
                  pl.BlockSpec((None, length, 2 * tc), lambda c, i: (c, 0, 0)),
                  pl.BlockSpec((None, 2, 1, tc), lambda c, i: (c, 0, 0, 0))],
        out_specs=pl.BlockSpec((1, length, tc), lambda c, i: (i, 0, c)),
        out_shape=jax.ShapeDtypeStruct((b, length, D_HYENA), F32),
        compiler_params=_params("arbitrary", "arbitrary"),
        name="hyena_mixer",
    )(hy, hy, hy, cw, cb, fwd_bf16, inv_bf16, regroup(ka), regroup(kb), hb)


def _outproj_kernel(a_ref, y_ref, x_ref, mod_ref, ga_ref, gh_ref, g2_ref, wo_ref, wr_ref,
                    x1_ref, h2_ref, aff_ref, *, nb, tt):
    rows = nb * tt
    a = a_ref[...].reshape(rows, D_ATTN)
    yh = y_ref[...].reshape(rows, D_HYENA)
    x = x_ref[...].reshape(rows, D_MODEL)
    mod = mod_ref[0]
    g1 = mod[:, 2 * D_MODEL:3 * D_MODEL]
    sh2 = mod[:, 3 * D_MODEL:4 * D_MODEL]
    sc2 = mod[:, 4 * D_MODEL:5 * D_MODEL]
    an = (_rms(a, D_ATTN) * ga_ref[...]).astype(BF16)
    yn = (_rms(yh, D_HYENA) * gh_ref[...]).astype(BF16)
    mix = (jnp.dot(an, wo_ref[0:D_ATTN, :], preferred_element_type=F32)
           + jnp.dot(yn, wo_ref[D_ATTN:, :], preferred_element_type=F32))
    x1 = x + g1 * mix
    h2 = _rms(x1, D_MODEL) * g2_ref[...] * (1.0 + sc2) + sh2
    x1_ref[...] = x1
    sub = D_MODEL // LANES
    for j in range(sub):
        h2_ref[pl.ds(j, rows, stride=sub), :] = h2[:, j * LANES:(j + 1) * LANES]
    logits = lax.dot_general(wr_ref[...], h2, _NT, precision=HI, preferred_element_type=F32)
    p = jnp.exp(logits - jnp.max(logits, axis=0, keepdims=True))
    aff_ref[...] = p / jnp.sum(p, axis=0, keepdims=True)


def out_projection(a, yh, x, mods3, out_g_attn, out_g_hyena, norm2_g, w_out_bf16, w_router_t,
                   *, nb, tt, mod_row):
    b, t, d = x.shape
    steps_per_b = t // tt
    n = b * t
    rows = nb * tt
    grid = (n // rows,)
    xmap = lambda i: (i // steps_per_b, i % steps_per_b, 0)
    vec = lambda w: pl.BlockSpec((1, w), lambda i: (0, 0))
    return pl.pallas_call(
        functools.partial(_outproj_kernel, nb=nb, tt=tt),
        grid=grid,
        in_specs=[pl.BlockSpec((nb, tt, D_ATTN), xmap),
                  pl.BlockSpec((nb, tt, D_HYENA), xmap),
                  pl.BlockSpec((nb, tt, d), xmap),
                  pl.BlockSpec((1, 1, N_MOD * d), lambda i: (mod_row(i), 0, 0)),
                  vec(D_ATTN), vec(D_HYENA), vec(d),
                  pl.BlockSpec((d, d), lambda i: (0, 0)),
                  pl.BlockSpec((N_EXPERTS, d), lambda i: (0, 0))],
        out_specs=[pl.BlockSpec((rows, d), lambda i: (i, 0)),
                   pl.BlockSpec((rows * d // LANES, LANES), lambda i: (i, 0)),
                   pl.BlockSpec((N_EXPERTS, rows), lambda i: (0, i))],
        out_shape=[jax.ShapeDtypeStruct((n, d), F32),
                   jax.ShapeDtypeStruct((n * d // LANES, LANES), F32),
                   jax.ShapeDtypeStruct((N_EXPERTS, n), F32)],
        compiler_params=_params("arbitrary"),
        name="out_projection",
    )(a, yh, x, mods3, out_g_attn.reshape(1, -1), out_g_hyena.reshape(1, -1), norm2_g.reshape(1, -1),
      w_out_bf16, w_router_t)


SEL_CHUNK = 256


def _cumsum_excl(mask_ref, out_ref, n):
    i = lax.broadcasted_iota(jnp.int32, (SEL_CHUNK, SEL_CHUNK), 0)
    j = lax.broadcasted_iota(jnp.int32, (SEL_CHUNK, SEL_CHUNK), 1)
    tri = (i < j).astype(BF16)
    carry = jnp.zeros((N_EXPERTS, 1), F32)
    for ch in range(n // SEL_CHUNK):
        sl = slice(ch * SEL_CHUNK, (ch + 1) * SEL_CHUNK)
        m = mask_ref[:, sl]
        out_ref[:, sl] = jnp.dot(m.astype(BF16), tri, preferred_element_type=F32) + carry
        carry = carry + jnp.sum(m, axis=1, keepdims=True)
    return carry


def _select_kernel(aff_ref, idx_ref, w_ref, pos_ref, mask_ref, rank_ref, *, n, cap):
    as_float = lambda bits: lax.bitcast_convert_type(bits, F32)

    def bit_step(i, thr):
        cand = thr | (jnp.int32(1) << (30 - i))
        cnt = jnp.sum((aff_ref[...] >= as_float(cand)).astype(jnp.int32), axis=1, keepdims=True)
        return jnp.where(cnt >= cap, cand, thr)

    thr = lax.fori_loop(0, 31, bit_step, jnp.zeros((N_EXPERTS, 1), jnp.int32))
    aff = aff_ref[...]
    gt = aff >= as_float(thr + 1)
    eq = (aff >= as_float(thr)) & jnp.logical_not(gt)
    need = cap - jnp.sum(gt.astype(jnp.int32), axis=1, keepdims=True)
    mask_ref[...] = eq.astype(F32)
    _cumsum_excl(mask_ref, rank_ref, n)
    sel = gt | (eq & (rank_ref[...] < need.astype(F32)))
    mask_ref[...] = sel.astype(F32)
    _cumsum_excl(mask_ref, rank_ref, n)
    pos = jnp.where(sel, rank_ref[...].astype(jnp.int32), -1)
    pos_ref[...] = pos

    sblk = 128
    lane_e = lax.broadcasted_iota(jnp.int32, (sblk, N_EXPERTS), 1)
    lane_t = lax.broadcasted_iota(jnp.int32, (sblk, LANES), 1)

    idx_ref[...] = jnp.zeros(idx_ref.shape, jnp.int32)
    w_ref[...] = jnp.zeros(w_ref.shape, F32)
    for e in range(N_EXPERTS):
        def slot_block(sb, carry, e=e):
            s0 = pl.multiple_of(sb * sblk, sblk)
            slot = lax.broadcasted_iota(jnp.int32, (sblk, LANES), 0) + s0

            def chunk(tcn, acc):
                acc_i, acc_w = acc
                t0 = pl.multiple_of(tcn * LANES, LANES)
                hit = pos_ref[e:e + 1, pl.ds(t0, LANES)] == slot
                acc_i = acc_i + jnp.where(hit, lane_t + t0, 0)
                acc_w = acc_w + jnp.where(hit, aff_ref[e:e + 1, pl.ds(t0, LANES)], 0.0)
                return acc_i, acc_w

            acc_i, acc_w = lax.fori_loop(
                0, n // LANES, chunk,
                (jnp.zeros((sblk, LANES), jnp.int32), jnp.zeros((sblk, LANES), F32)), unroll=4)
            col_i = jnp.sum(acc_i, axis=1, keepdims=True)
            col_w = jnp.sum(acc_w, axis=1, keepdims=True)
            rs = pl.ds(s0, sblk)
            idx_ref[rs, :] = jnp.where(lane_e == e, col_i, idx_ref[rs, :])
            w_ref[rs, :] = jnp.where(lane_e == e, col_w, w_ref[rs, :])
            return carry

        lax.fori_loop(0, cap // sblk, slot_block, 0)


def expert_select(aff_t):
    e, n = aff_t.shape
    cap = EC_FACTOR * n // N_EXPERTS
    return pl.pallas_call(
        functools.partial(_select_kernel, n=n, cap=cap),
        grid=(1,),
        in_specs=[pl.BlockSpec((e, n), lambda i: (0, 0))],
        out_specs=[pl.BlockSpec((cap, e), lambda i: (0, 0)),
                   pl.BlockSpec((cap, e), lambda i: (0, 0)),
                   pl.BlockSpec((e, n), lambda i: (0, 0))],
        out_shape=[jax.ShapeDtypeStruct((cap, e), jnp.int32),
                   jax.ShapeDtypeStruct((cap, e), F32),
                   jax.ShapeDtypeStruct((e, n), jnp.int32)],
        scratch_shapes=[pltpu.VMEM((e, n), F32), pltpu.VMEM((e, n), F32)],
        compiler_params=_params("arbitrary"),
        name="expert_select",
    )(aff_t)


FF_TILE = 256
FFN_ROWS = 512
Y_PAD = 64


TOK_SUB = D_MODEL // LANES


def _row_copy(src_hbm, tok, xe_ref, slot, sem):
    src = src_hbm.at[pl.ds(pl.multiple_of(tok * TOK_SUB, TOK_SUB), TOK_SUB), :]
    dst = xe_ref.at[pl.ds(pl.multiple_of(slot * TOK_SUB, TOK_SUB), TOK_SUB), :]
    return pltpu.make_async_copy(src, dst, sem)


def _ffn_kernel(idx_ref, hp_hbm, hs_hbm, wt_ref, w1_ref, w3_ref, w2_ref, y_ref, xe_ref, xb_ref, sem,
                *, cap):
    e = pl.program_id(0)
    f = pl.program_id(1)
    rows = 2 * cap

    @pl.when(f == 0)
    def _():
        def start(s, c):
            _row_copy(hp_hbm, idx_ref[0, e, s], xe_ref, s, sem).start()
            _row_copy(hs_hbm, idx_ref[1, e, s], xe_ref, cap + s, sem).start()
            return c

        lax.fori_loop(0, cap, start, 0)

        def wait(s, c):
            _row_copy(hp_hbm, 0, xe_ref, s, sem).wait()
            _row_copy(hs_hbm, 0, xe_ref, cap + s, sem).wait()
            return c

        lax.fori_loop(0, cap, wait, 0)
        for rc in range(rows // FFN_ROWS):
            for j in range(TOK_SUB):
                part = xe_ref[pl.ds(rc * FFN_ROWS * TOK_SUB + j, FFN_ROWS, stride=TOK_SUB), :]
                xb_ref[rc * FFN_ROWS:(rc + 1) * FFN_ROWS, j * LANES:(j + 1) * LANES] = part.astype(BF16)
        y_ref[0] = jnp.zeros(y_ref.shape[1:], F32)

    w1b = w1_ref[0].astype(BF16)
    w3b = w3_ref[0].astype(BF16)
    w2b = w2_ref[0].astype(BF16)
    for rc in range(rows // FFN_ROWS):
        rs = slice(rc * FFN_ROWS, (rc + 1) * FFN_ROWS)
        xb = xb_ref[rs, :]
        h1 = jnp.dot(xb, w1b, preferred_element_type=F32)
        h3 = jnp.dot(xb, w3b, preferred_element_type=F32)
        hid = (_silu(h1) * h3).astype(BF16)
        y_ref[0, rs, :] += jnp.dot(hid, w2b, preferred_element_type=F32)

    @pl.when(f == pl.num_programs(1) - 1)
    def _():
        lane_e = lax.broadcasted_iota(jnp.int32, (rows, N_EXPERTS), 1)
        scale = jnp.sum(jnp.where(lane_e == e, wt_ref[...], 0.0), axis=1, keepdims=True)
        y_ref[0, 0:rows, :] = y_ref[0, 0:rows, :] * scale


def expert_ffn(idx, h2p, h2s, wts, w1, w3, w2):
    _, ne, cap = idx.shape
    d = w1.shape[1]
    ff = w1.shape[2]
    rows = 2 * cap
    grid_spec = pltpu.PrefetchScalarGridSpec(
        num_scalar_prefetch=1,
        grid=(ne, ff // FF_TILE),
        in_specs=[pl.BlockSpec(memory_space=pl.ANY),
                  pl.BlockSpec(memory_space=pl.ANY),
                  pl.BlockSpec((rows, ne), lambda e, f, idx: (0, 0)),
                  pl.BlockSpec((1, d, FF_TILE), lambda e, f, idx: (e, 0, f)),
                  pl.BlockSpec((1, d, FF_TILE), lambda e, f, idx: (e, 0, f)),
                  pl.BlockSpec((1, FF_TILE, d), lambda e, f, idx: (e, f, 0))],
        out_specs=pl.BlockSpec((1, rows + Y_PAD, d), lambda e, f, idx: (e, 0, 0)),
        scratch_shapes=[pltpu.VMEM((rows * TOK_SUB, LANES), F32), pltpu.VMEM((rows, d), BF16),
                        pltpu.SemaphoreType.DMA(())],
    )
    return pl.pallas_call(
        functools.partial(_ffn_kernel, cap=cap),
        grid_spec=grid_spec,
        out_shape=jax.ShapeDtypeStruct((ne, rows + Y_PAD, d), F32),
        compiler_params=_params("arbitrary", "arbitrary"),
        name="expert_ffn",
    )(idx, h2p, h2s, wts, w1, w3, w2)


COMB_TILE = 512


def _combine_kernel(idx_ref, off_ref, x1_ref, mod_ref, fg_ref, y_hbm, o_ref, acc_ref, stage_ref, tile_ref,
                    sem, *, row0):
    i = pl.program_id(0)
    acc_ref[...] = jnp.zeros(acc_ref.shape, F32)
    base = i * COMB_TILE
    sub = D_MODEL // LANES

    def per_expert(e, c):
        a = off_ref[e, i]
        cnt = off_ref[e, i + 1] - a
        a8 = lax.shift_left(lax.shift_right_logical(a, 3), 3)
        lead = a - a8
        nchunk = lax.shift_right_logical(cnt + lead + (Y_PAD - 1), Y_PAD.bit_length() - 1)

        def chunk_copy(ci):
            src0 = pl.multiple_of(row0 + a8 + ci * Y_PAD, 8)
            return pltpu.make_async_copy(y_hbm.at[e, pl.ds(src0, Y_PAD), :],
                                         stage_ref.at[pl.ds(pl.multiple_of(ci * Y_PAD, Y_PAD), Y_PAD), :], sem)

        def start(ci, cc):
            chunk_copy(ci).start()
            return cc

        def wait(ci, cc):
            chunk_copy(ci).wait()
            return cc

        lax.fori_loop(0, nchunk, start, 0)
        lax.fori_loop(0, nchunk, wait, 0)

        def retile(ci, cc):
            r0 = pl.multiple_of(ci * Y_PAD, Y_PAD)
            for j in range(sub):
                tile_ref[pl.ds(r0 * sub + j, Y_PAD, stride=sub), :] = (
                    stage_ref[pl.ds(r0, Y_PAD), j * LANES:(j + 1) * LANES])
            return cc

        lax.fori_loop(0, nchunk, retile, 0)

        def add_row(s, cc):
            t = idx_ref[e, a + s] - base
            dst = pl.ds(pl.multiple_of(t * sub, sub), sub)
            acc_ref[dst, :] += tile_ref[pl.ds(pl.multiple_of((lead + s) * sub, sub), sub), :]
            return cc

        lax.fori_loop(0, cnt, add_row, 0)
        return c

    lax.fori_loop(0, N_EXPERTS, per_expert, 0)
    g2 = mod_ref[0][:, 5 * D_MODEL:6 * D_MODEL]
    acc = jnp.concatenate([acc_ref[pl.ds(j, COMB_TILE, stride=sub), :] for j in range(sub)], axis=1)
    x = x1_ref[...] + g2 * acc
    o_ref[...] = _rms(x, D_MODEL) * fg_ref[...]


def combine(idx, offs, x1, mods3, final_g, y_all, *, row0, mod_row):
    n, d = x1.shape
    grid_spec = pltpu.PrefetchScalarGridSpec(
        num_scalar_prefetch=2,
        grid=(n // COMB_TILE,),
        in_specs=[pl.BlockSpec((COMB_TILE, d), lambda i, idx, off: (i, 0)),
                  pl.BlockSpec((1, 1, N_MOD * d), lambda i, idx, off: (mod_row(i), 0, 0)),
                  pl.BlockSpec((1, d), lambda i, idx, off: (0, 0)),
                  pl.BlockSpec(memory_space=pl.ANY)],
        out_specs=pl.BlockSpec((COMB_TILE, d), lambda i, idx, off: (i, 0)),
        scratch_shapes=[pltpu.VMEM((COMB_TILE * d // LANES, LANES), F32),
                        pltpu.VMEM((COMB_TILE + Y_PAD, d), F32),
                        pltpu.VMEM(((COMB_TILE + Y_PAD) * d // LANES, LANES), F32),
                        pltpu.SemaphoreType.DMA(())],
    )
    return pl.pallas_call(
        functools.partial(_combine_kernel, row0=row0),
        grid_spec=grid_spec,
        out_shape=jax.ShapeDtypeStruct((n, d), F32),
        compiler_params=_params("arbitrary"),
        name="combine",
    )(idx, offs, x1, mods3, final_g.reshape(1, d), y_all)


def _tile_offsets(pos, cap):
    sel = pos >= 0
    counts = jnp.sum(sel.reshape(N_EXPERTS, -1, COMB_TILE), axis=2).astype(jnp.int32)
    return jnp.concatenate([jnp.zeros((N_EXPERTS, 1), jnp.int32), jnp.cumsum(counts, axis=1)], axis=1)


def kernel(x_prompt, x_sample, cache_k, cache_v, c, c_ctx, norm1_g, norm2_g, w_ada, b_ada, w_in, w_out,
           out_g_attn, out_g_hyena, rpb, conv_w, conv_b, filt_w1, filt_b1, filt_w2, filt_b2, filt_w3,
           filt_freq, hyena_bias, w_router, w1, w3, w2, final_g):
    depth = norm1_g.shape[0]
    assert depth == 1
    l = 0
    bp, tp, d = x_prompt.shape
    bs, ts, _ = x_sample.shape

    cond = jnp.zeros((MOD_ROWS, d), F32).at[0].set(c_ctx).at[1:1 + bs].set(c)
    mods3 = ada_mod(cond, w_ada[l], b_ada[l]).reshape(MOD_ROWS, 1, N_MOD * d)

    w_in_b = w_in[l].astype(BF16)
    w_out_b = w_out[l].astype(BF16)
    w_router_t = w_router[l].T
    rpb_ext = jnp.pad(rpb[l], ((0, 0), (0, 0), (RPB_PAD_L, RPB_PAD_R)), mode="edge")
    w1p = jnp.pad(filt_w1[l], ((0, LANES - FILTER_EMB), (0, 0)))

    ctx_row = lambda i: 0
    tt_s = 512
    lat_row = lambda i: 1 + i // (ts // tt_s)

    qkv_p, hy_p, state_k, state_v = in_projection(x_prompt, mods3, norm1_g[l], w_in_b, nb=2, tt=tp,
                                                  mod_row=ctx_row, with_state=True)
    qkv_s, hy_s = in_projection(x_sample, mods3, norm1_g[l], w_in_b, nb=1, tt=tt_s,
                                mod_row=lat_row, with_state=False)

    a_p = context_attention(qkv_p)
    a_s = latent_attention(qkv_s, cache_k[:, l:l + 1], cache_v[:, l:l + 1], rpb_ext)

    def hyena(hy, length):
        fwd_tab, inv_tab = _dft_tables(length)
        fwd_tab = jnp.asarray(fwd_tab)
        ka, kb = hyena_filter(length, w1p, filt_b1[l], filt_w2[l], filt_b2[l], filt_w3[l], filt_freq[l],
                              fwd_tab)
        return hyena_mixer(hy, conv_w[l], conv_b[l], fwd_tab.astype(BF16), jnp.asarray(inv_tab).astype(BF16),
                           ka, kb, hyena_bias[l])

    yh_p = hyena(hy_p, tp)
    yh_s = hyena(hy_s, ts)

    x1_p, h2_p, aff_p = out_projection(a_p, yh_p, x_prompt, mods3, out_g_attn[l], out_g_hyena[l],
                                       norm2_g[l], w_out_b, w_router_t, nb=2, tt=tp, mod_row=ctx_row)
    x1_s, h2_s, aff_s = out_projection(a_s, yh_s, x_sample, mods3, out_g_attn[l], out_g_hyena[l],
                                       norm2_g[l], w_out_b, w_router_t, nb=1, tt=tt_s, mod_row=lat_row)

    idx_p, wt_p, pos_p = expert_select(aff_p)
    idx_s, wt_s, pos_s = expert_select(aff_s)
    cap = idx_p.shape[0]
    idx = jnp.stack([idx_p.T, idx_s.T])
    wts = jnp.concatenate([wt_p, wt_s], axis=0)
    y_all = expert_ffn(idx, h2_p, h2_s, wts, w1[l], w3[l], w2[l])

    comb_row_s = lambda i: 1 + i // (ts // COMB_TILE)
    y_p = combine(idx[0], _tile_offsets(pos_p, cap), x1_p, mods3, final_g, y_all, row0=0, mod_row=ctx_row)
    y_s = combine(idx[1], _tile_offsets(pos_s, cap), x1_s, mods3, final_g, y_all, row0=cap,
                  mod_row=comb_row_s)

    return (y_p.reshape(bp, tp, d), y_s.reshape(bs, ts, d),
            state_k.reshape(bp, depth, N_HEADS, tp, HEAD_DIM), state_v.reshape(bp, depth, N_HEADS, tp, HEAD_DIM))
```

```python
import functools
import math

import numpy as np
import jax
import jax.numpy as jnp
from jax import lax
from jax.experimental import pallas as pl
from jax.experimental.pallas import tpu as pltpu

D_MODEL = 1024
GRID_W = 64
D_ATTN = 512
D_HYENA = 512
HEAD_DIM = 64
N_HEADS = 8
WIN_ROWS = 8
WIN_COLS = 16
FILTER_EMB = 33
FILTER_BANDS = 16
FILTER_HIDDEN = 64
DECAY_TARGET = 1e-2
SHORT_DECAY_PCT = 0.3
LONG_DECAY_PCT = 1.5
MOD_SHIFT = 0.05
N_EXPERTS = 16
EC_FACTOR = 2
EXPERT_FF = 2816
N_MOD = 6
EPS = 1e-6
NEG_INF = -1e30

LANES = 128
MOD_ROWS = 16
VMEM_LIMIT = 56 * 1024 * 1024
HI = lax.Precision.HIGHEST
BF16 = jnp.bfloat16
F32 = jnp.float32

_NT = (((1,), (1,)), ((), ()))


def _params(*sem):
    return pltpu.CompilerParams(dimension_semantics=sem, vmem_limit_bytes=VMEM_LIMIT)


def _rms(x, n):
    return x * lax.rsqrt(jnp.sum(x * x, axis=-1, keepdims=True) * (1.0 / n) + EPS)


def _silu(x):
    return x * (1.0 / (1.0 + jnp.exp(-x)))


def _ada_kernel(c_ref, w_ref, b_ref, o_ref):
    s = _silu(c_ref[...])
    o_ref[...] = jnp.dot(s, w_ref[...], precision=HI, preferred_element_type=F32) + b_ref[...]


def ada_mod(cond, w_ada, b_ada):
    n = w_ada.shape[1]
    tn = 1024
    return pl.pallas_call(
        _ada_kernel,
        grid=(n // tn,),
        in_specs=[pl.BlockSpec((MOD_ROWS, D_MODEL), lambda j: (0, 0)),
                  pl.BlockSpec((D_MODEL, tn), lambda j: (0, j)),
                  pl.BlockSpec((1, tn), lambda j: (0, j))],
        out_specs=pl.BlockSpec((MOD_ROWS, tn), lambda j: (0, j)),
        out_shape=jax.ShapeDtypeStruct((MOD_ROWS, n), F32),
        compiler_params=_params("arbitrary"),
        name="ada_mod",
    )(cond, w_ada, b_ada.reshape(1, n))


def _inproj_kernel(x_ref, mod_ref, g_ref, w_ref, qkv_ref, hy_ref, *state_refs, nb, tt):
    x = x_ref[...].reshape(nb * tt, D_MODEL)
    mod = mod_ref[0]
    sh1 = mod[:, 0:D_MODEL]
    sc1 = mod[:, D_MODEL:2 * D_MODEL]
    h = _rms(x, D_MODEL) * g_ref[...] * (1.0 + sc1) + sh1
    proj = jnp.dot(h.astype(BF16), w_ref[...], preferred_element_type=F32)
    q = proj[:, 0:D_ATTN] * (HEAD_DIM ** -0.5)
    k = proj[:, D_ATTN:2 * D_ATTN]
    v = proj[:, 2 * D_ATTN:3 * D_ATTN]
    qkv_ref[:, :, 0:D_ATTN] = q.reshape(nb, tt, D_ATTN)
    qkv_ref[:, :, D_ATTN:2 * D_ATTN] = k.reshape(nb, tt, D_ATTN)
    qkv_ref[:, :, 2 * D_ATTN:3 * D_ATTN] = v.reshape(nb, tt, D_ATTN)
    hy_ref[...] = proj[:, 3 * D_ATTN:].reshape(nb, tt, 3 * D_HYENA)
    if state_refs:
        sk_ref, sv_ref = state_refs
        for bi in range(nb):
            for hd in range(N_HEADS):
                sl = slice(hd * HEAD_DIM, (hd + 1) * HEAD_DIM)
                sk_ref[bi, hd] = k[bi * tt:(bi + 1) * tt, sl]
                sv_ref[bi, hd] = v[bi * tt:(bi + 1) * tt, sl]


def in_projection(x, mods3, norm_g, w_in_bf16, *, nb, tt, mod_row, with_state):
    b, t, d = x.shape
    steps_per_b = t // tt
    grid = (b // nb * steps_per_b,)
    xmap = lambda i: (i // steps_per_b, i % steps_per_b, 0)
    nproj = w_in_bf16.shape[1]
    out_shape = [jax.ShapeDtypeStruct((b, t, 3 * D_ATTN), F32),
                 jax.ShapeDtypeStruct((b, t, 3 * D_HYENA), F32)]
    out_specs = [pl.BlockSpec((nb, tt, 3 * D_ATTN), xmap),
                 pl.BlockSpec((nb, tt, 3 * D_HYENA), xmap)]
    if with_state:
        assert steps_per_b == 1
        smap = lambda i: (i, 0, 0, 0)
        for _ in range(2):
            out_shape.append(jax.ShapeDtypeStruct((b, N_HEADS, t, HEAD_DIM), F32))
            out_specs.append(pl.BlockSpec((nb, N_HEADS, tt, HEAD_DIM), smap))
    return pl.pallas_call(
        functools.partial(_inproj_kernel, nb=nb, tt=tt),
        grid=grid,
        in_specs=[pl.BlockSpec((nb, tt, d), xmap),
                  pl.BlockSpec((1, 1, N_MOD * d), lambda i: (mod_row(i), 0, 0)),
                  pl.BlockSpec((1, d), lambda i: (0, 0)),
                  pl.BlockSpec((d, nproj), lambda i: (0, 0))],
        out_specs=out_specs,
        out_shape=out_shape,
        compiler_params=_params("arbitrary"),
        name="in_projection",
    )(x, mods3, norm_g.reshape(1, d), w_in_bf16)


def _lane_mask(shape, half):
    lane = lax.broadcasted_iota(jnp.int32, shape, 1)
    return (lane < HEAD_DIM) if half == 0 else (lane >= HEAD_DIM)


def _ctx_attn_kernel(qkv_ref, o_ref):
    for j in range(N_HEADS // 2):
        lo = j * LANES
        qp = qkv_ref[0, :, lo:lo + LANES]
        kp = qkv_ref[0, :, D_ATTN + lo:D_ATTN + lo + LANES].astype(BF16)
        vp = qkv_ref[0, :, 2 * D_ATTN + lo:2 * D_ATTN + lo + LANES]
        out = None
        for half in range(2):
            m = _lane_mask(qp.shape, half)
            qm = jnp.where(m, qp, 0.0).astype(BF16)
            vm = jnp.where(m, vp, 0.0).astype(BF16)
            s = lax.dot_general(qm, kp, _NT, preferred_element_type=F32)
            p = jnp.exp(s - jnp.max(s, axis=-1, keepdims=True))
            l = jnp.sum(p, axis=-1, keepdims=True)
            o = jnp.dot(p.astype(BF16), vm, preferred_element_type=F32) * (1.0 / l)
            out = o if out is None else out + o
        o_ref[0, :, lo:lo + LANES] = out


def context_attention(qkv):
    b, t, _ = qkv.shape
    return pl.pallas_call(
        _ctx_attn_kernel,
        grid=(b,),
        in_specs=[pl.BlockSpec((1, t, 3 * D_ATTN), lambda i: (i, 0, 0))],
        out_specs=pl.BlockSpec((1, t, D_ATTN), lambda i: (i, 0, 0)),
        out_shape=jax.ShapeDtypeStruct((b, t, D_ATTN), F32),
        compiler_params=_params("arbitrary"),
        name="context_attention",
    )(qkv)


N_TPAIR = 2 * WIN_ROWS - 2
RPB_PAD_L = GRID_W - WIN_COLS
RPB_PAD_R = LANES - RPB_PAD_L - (2 * WIN_COLS - 1)


def _row_start(r, rows, wr):
    return min(max(r - wr // 2, 0), rows - wr)


def _build_bias(rpb_ref, tp_ref, m_ref, rows, wr):
    shape = (GRID_W, LANES)
    c = lax.broadcasted_iota(jnp.int32, shape, 0)
    lane = lax.broadcasted_iota(jnp.int32, shape, 1)
    kc = lane & (GRID_W - 1)
    ws = jnp.clip(c - WIN_COLS // 2, 0, GRID_W - WIN_COLS)
    col_ok = (kc >= ws) & (kc < ws + WIN_COLS)
    first = lane < GRID_W
    for hh in range(2):
        for dr in range(N_TPAIR):
            xa = jnp.broadcast_to(rpb_ref[hh, dr:dr + 1, :], shape)
            xb = jnp.broadcast_to(rpb_ref[hh, dr + 1:dr + 2, :], shape)
            ta = pltpu.roll(xa, GRID_W + 1, 1, stride=1, stride_axis=0)
            tb = pltpu.roll(xb, 1, 1, stride=1, stride_axis=0)
            tp_ref[hh, dr] = jnp.where(col_ok, jnp.where(first, ta, tb), NEG_INF)
        for r in range(rows):
            rs = _row_start(r, rows, wr)
            for jj in range(rows // 2):
                in_a = rs <= 2 * jj < rs + wr
                in_b = rs <= 2 * jj + 1 < rs + wr
                dr = 2 * jj - r + (WIN_ROWS - 1)
                if in_a and in_b:
                    blk = tp_ref[hh, dr]
                elif in_a:
                    blk = jnp.where(first, tp_ref[hh, dr], NEG_INF)
                elif in_b:
                    blk = jnp.where(first, NEG_INF, tp_ref[hh, dr])
                else:
                    blk = jnp.full(shape, NEG_INF, F32)
                m_ref[hh, r * GRID_W:(r + 1) * GRID_W, jj * LANES:(jj + 1) * LANES] = blk


def _key_range(qb, rows_per_blk, rows, wr):
    r0 = qb * rows_per_blk
    lo = min(_row_start(r, rows, wr) for r in range(r0, r0 + rows_per_blk))
    hi = max(_row_start(r, rows, wr) for r in range(r0, r0 + rows_per_blk)) + wr
    grp = 256 // GRID_W
    return (lo // grp) * 256, -(-hi // grp) * 256


def _lat_attn_kernel(q_ref, k_ref, v_ref, ck_ref, cv_ref, rpb_ref, o_ref, tp_ref, m_ref, *, rows, wr):
    @pl.when(pl.program_id(1) == 0)
    def _():
        _build_bias(rpb_ref, tp_ref, m_ref, rows, wr)

    l_tok = rows * GRID_W
    qblk = 256
    rows_per_blk = qblk // GRID_W
    sel_r = lax.broadcasted_iota(jnp.int32, (HEAD_DIM, LANES), 0)
    sel_c = lax.broadcasted_iota(jnp.int32, (HEAD_DIM, LANES), 1)
    for qb in range(l_tok // qblk):
        c0, c1 = _key_range(qb, rows_per_blk, rows, wr)
        qp = q_ref[0, qb * qblk:(qb + 1) * qblk, :]
        kp = k_ref[0, c0:c1, :].astype(BF16)
        vp = v_ref[0, c0:c1, :]
        out = None
        for hh in range(2):
            place = (sel_c == sel_r + hh * HEAD_DIM).astype(BF16)
            ck = jnp.dot(ck_ref[0, 0, hh].astype(BF16), place, preferred_element_type=F32).astype(BF16)
            cv = jnp.dot(cv_ref[0, 0, hh].astype(BF16), place, preferred_element_type=F32).astype(BF16)
            m = _lane_mask(qp.shape, hh)
            qm = jnp.where(m, qp, 0.0).astype(BF16)
            vm = jnp.where(_lane_mask(vp.shape, hh), vp, 0.0).astype(BF16)
            s_win = lax.dot_general(qm, kp, _NT, preferred_element_type=F32)
            s_win = s_win + m_ref[hh, qb * qblk:(qb + 1) * qblk, c0:c1]
            s_ctx = lax.dot_general(qm, ck, _NT, preferred_element_type=F32)
            mx = jnp.maximum(jnp.max(s_win, axis=-1, keepdims=True), jnp.max(s_ctx, axis=-1, keepdims=True))
            p_win = jnp.exp(s_win - mx)
            p_ctx = jnp.exp(s_ctx - mx)
            l = jnp.sum(p_win, axis=-1, keepdims=True) + jnp.sum(p_ctx, axis=-1, keepdims=True)
            o = (jnp.dot(p_win.astype(BF16), vm, preferred_element_type=F32)
                 + jnp.dot(p_ctx.astype(BF16), cv, preferred_element_type=F32)) * (1.0 / l)
            out = o if out is None else out + o
        o_ref[0, qb * qblk:(qb + 1) * qblk, :] = out


def latent_attention(qkv, cache_k, cache_v, rpb_ext):
    b, l_tok, _ = qkv.shape
    rows = l_tok // GRID_W
    wr = min(WIN_ROWS, rows)
    assert rows % 4 == 0 and wr == WIN_ROWS
    npair = N_HEADS // 2
    past = cache_k.shape[3]
    blk = lambda off: pl.BlockSpec((1, l_tok, LANES), lambda j, i: (i, 0, off + j))
    cspec = pl.BlockSpec((1, 1, 2, past, HEAD_DIM), lambda j, i: (i, 0, j, 0, 0))
    return pl.pallas_call(
        functools.partial(_lat_attn_kernel, rows=rows, wr=wr),
        grid=(npair, b),
        in_specs=[blk(0), blk(npair), blk(2 * npair), cspec, cspec,
                  pl.BlockSpec((2, 2 * WIN_ROWS - 1, LANES), lambda j, i: (j, 0, 0))],
        out_specs=pl.BlockSpec((1, l_tok, LANES), lambda j, i: (i, 0, j)),
        out_shape=jax.ShapeDtypeStruct((b, l_tok, D_ATTN), F32),
        scratch_shapes=[pltpu.VMEM((2, N_TPAIR, GRID_W, LANES), F32),
                        pltpu.VMEM((2, l_tok, l_tok), F32)],
        compiler_params=_params("arbitrary", "arbitrary"),
        name="latent_attention",
    )(qkv, qkv, qkv, cache_k, cache_v, rpb_ext)


def _dft_tables(length):
    n2 = 2 * length
    k = np.arange(length, dtype=np.float64)[:, None]
    t = np.arange(length, dtype=np.float64)[None, :]
    ang = 2.0 * np.pi * ((k * t) % n2) / n2
    fa = np.cos(ang)
    fb = -np.sin(ang)
    fb[0, :] = np.cos(np.pi * t[0])
    fwd = np.concatenate([fa, fb], axis=0)
    ga = 2.0 * np.cos(ang).T
    ga[:, 0] = 1.0
    gb = -2.0 * np.sin(ang).T
    gb[:, 0] = np.cos(np.pi * t[0])
    inv = np.concatenate([ga, gb], axis=1)
    return fwd.astype(np.float32), inv.astype(np.float32)


def _filter_consts(length):
    t = np.linspace(0.0, 1.0, length, dtype=np.float32)[:, None]
    w = (np.float32(2.0 * math.pi / length) * np.arange(length, dtype=np.float32))[:, None]
    bands = np.linspace(1e-4, FILTER_BANDS - 1, FILTER_BANDS, dtype=np.float32)[None, :]
    z = np.concatenate([t, np.cos(bands * w), -np.sin(bands * w)], axis=-1).astype(np.float32)
    zp = np.zeros((length, LANES), np.float32)
    zp[:, :FILTER_EMB] = z
    deltas = np.abs(np.linspace(math.log(DECAY_TARGET) / LONG_DECAY_PCT,
                                math.log(DECAY_TARGET) / SHORT_DECAY_PCT, D_HYENA, dtype=np.float32))
    window = (np.exp(-t * deltas) + np.float32(MOD_SHIFT)).astype(np.float32)
    return zp, window


def _split3(a, b):
    a_hi = a.astype(BF16)
    a_lo = (a - a_hi.astype(F32)).astype(BF16)
    b_hi = b.astype(BF16)
    b_lo = (b - b_hi.astype(F32)).astype(BF16)
    return (jnp.dot(a_hi, b_hi, preferred_element_type=F32)
            + jnp.dot(a_hi, b_lo, preferred_element_type=F32)
            + jnp.dot(a_lo, b_hi, preferred_element_type=F32))


def _filter_kernel(z_ref, win_ref, w1_ref, b1_ref, w2_ref, b2_ref, w3_ref, fr_ref, fwd_ref,
                   ka_ref, kb_ref, s_ref, d_ref, *, length):
    j = pl.program_id(0)

    @pl.when(j == 0)
    def _():
        fr = fr_ref[...]
        hid = jnp.sin(fr * (jnp.dot(z_ref[...], w1_ref[...], precision=HI, preferred_element_type=F32)
                            + b1_ref[...]))
        hid = jnp.sin(fr * (jnp.dot(hid, w2_ref[...], precision=HI, preferred_element_type=F32)
                            + b2_ref[...]))
        filt = jnp.dot(hid, w3_ref[...], precision=HI, preferred_element_type=F32)
        nc = 2 * D_HYENA
        win = win_ref[...]
        win2 = jnp.concatenate([win, win], axis=1)
        fw = filt[:, 0:nc] * win2
        row = lax.broadcasted_iota(jnp.int32, (length, nc), 0)
        bw = jnp.where(row == 0, 0.0, filt[:, nc:2 * nc] * win2)
        inv = 1.0 / jnp.sum(jnp.abs(fw) + jnp.abs(bw), axis=0, keepdims=True)
        s_ref[...] = (fw + bw) * inv
        d_ref[...] = (fw - bw) * inv

    tm = ka_ref.shape[0]
    nsteps = length // tm
    fa = fwd_ref[0]
    fb = fwd_ref[1]
    ka_ref[...] = _split3(fa, s_ref[...])
    kb = _split3(fb, d_ref[...])
    nyq = _split3(fb[0:8, :], s_ref[...])[0:1, :]
    row = lax.broadcasted_iota(jnp.int32, kb.shape, 0)
    kb_ref[...] = jnp.where((row == 0) & (j == 0), nyq, kb)
    del nsteps


def hyena_filter(length, w1p, b1, w2, b2, w3, freq, fwd_tab):
    zp, window = _filter_consts(length)
    nc = 2 * D_HYENA
    tm = 256
    fwd3 = fwd_tab.reshape(2, length, length)
    full = lambda a: pl.BlockSpec(a.shape, lambda j: (0,) * a.ndim)
    args = [jnp.asarray(zp), jnp.asarray(window), w1p, b1.reshape(1, -1), w2, b2.reshape(1, -1), w3,
            freq.reshape(1, -1)]
    return pl.pallas_call(
        functools.partial(_filter_kernel, length=length),
        grid=(length // tm,),
        in_specs=[full(a) for a in args] + [pl.BlockSpec((2, tm, length), lambda j: (0, j, 0))],
        out_specs=[pl.BlockSpec((tm, nc), lambda j: (j, 0))] * 2,
        out_shape=[jax.ShapeDtypeStruct((length, nc), F32)] * 2,
        scratch_shapes=[pltpu.VMEM((length, nc), F32), pltpu.VMEM((length, nc), F32)],
        compiler_params=_params("arbitrary"),
        name="hyena_filter",
    )(*args, fwd3)


def _long_conv(u, f_ref, g_ref, ka, kb, bias, length):
    x = jnp.dot(f_ref[...], u.astype(BF16), preferred_element_type=F32)
    xa = x[0:length]
    xb = x[length:2 * length]
    row = lax.broadcasted_iota(jnp.int32, xa.shape, 0)
    dc = row == 0
    ya = xa * ka - jnp.where(dc, 0.0, xb * kb)
    yb = jnp.where(dc, xb * kb, xa * kb + xb * ka)
    y = (jnp.dot(g_ref[:, 0:length], ya.astype(BF16), preferred_element_type=F32)
         + jnp.dot(g_ref[:, length:2 * length], yb.astype(BF16), preferred_element_type=F32))
    return y * (1.0 / (2 * length)) + u * bias


def _hyena_kernel(v_ref, x1_ref, x2_ref, cw_ref, cb_ref, f_ref, g_ref, ka_ref, kb_ref, hb_ref, o_ref,
                  *, length, tc):
    row = lax.broadcasted_iota(jnp.int32, (length, tc), 0)

    def short(u_ref, part):
        u = u_ref[0]
        up = jnp.where(row == 0, 0.0, pltpu.roll(u, 1, 0))
        dn = jnp.where(row == length - 1, 0.0, pltpu.roll(u, length - 1, 0))
        w = cw_ref[part]
        return up * w[0:1] + u * w[1:2] + dn * w[2:3] + cb_ref[part]

    v = short(v_ref, 0)
    x1 = short(x1_ref, 1)
    x2 = short(x2_ref, 2)
    z = x1 * _long_conv(v, f_ref, g_ref, ka_ref[:, 0:tc], kb_ref[:, 0:tc], hb_ref[0], length)
    y = x2 * _long_conv(z, f_ref, g_ref, ka_ref[:, tc:2 * tc], kb_ref[:, tc:2 * tc], hb_ref[1], length)
    o_ref[0] = y


def hyena_mixer(hy, conv_w, conv_b, fwd_bf16, inv_bf16, ka, kb, hy_bias):
    b, length, _ = hy.shape
    tc = 256
    nct = D_HYENA // tc
    regroup = lambda a: a.reshape(length, 2, nct, tc).transpose(2, 0, 1, 3).reshape(nct, length, 2 * tc)
    cw = conv_w.reshape(3, 3, nct, 1, tc).transpose(1, 2, 0, 3, 4).reshape(3, nct, 3, tc)
    cb = conv_b.reshape(3, nct, 1, tc)
    hb = hy_bias.reshape(2, nct, 1, tc).transpose(1, 0, 2, 3)
    part = lambda p: pl.BlockSpec((1, length, tc), lambda c, i: (i, 0, p * nct + c))
    return pl.pallas_call(
        functools.partial(_hyena_kernel, length=length, tc=tc),
        grid=(nct, b),
        in_specs=[part(0), part(1), part(2),
                  pl.BlockSpec((3, None, 3, tc), lambda c, i: (0, c, 0, 0)),
                  pl.BlockSpec((3, None, 1, tc), lambda c, i: (0, c, 0, 0)),
                  pl.BlockSpec((2 * length, length), lambda c, i: (0, 0)),
                  pl.BlockSpec((length, 2 * length), lambda c, i: (0, 0)),
                  pl.BlockSpec((None, length, 2 * tc), lambda c, i: (c, 0, 0)),
                  pl.BlockSpec((None, length, 2 * tc), lambda c, i: (c, 0, 0)),
                  pl.BlockSpec((None, 2, 1, tc), lambda c, i: (c, 0, 0, 0))],
        out_specs=pl.BlockSpec((1, length, tc), lambda c, i: (i, 0, c)),
        out_shape=jax.ShapeDtypeStruct((b, length, D_HYENA), F32),
        compiler_params=_params("arbitrary", "arbitrary"),
        name="hyena_mixer",
    )(hy, hy, hy, cw, cb, fwd_bf16, inv_bf16, regroup(ka), regroup(kb), hb)


def _outproj_kernel(a_ref, y_ref, x_ref, mod_ref, ga_ref, gh_ref, g2_ref, wo_ref, wr_ref,
                    x1_ref, h2_ref, aff_ref, *, nb, tt):
    rows = nb * tt
    a = a_ref[...].reshape(rows, D_ATTN)
    yh = y_ref[...].reshape(rows, D_HYENA)
    x = x_ref[...].reshape(rows, D_MODEL)
    mod = mod_ref[0]
    g1 = mod[:, 2 * D_MODEL:3 * D_MODEL]
    sh2 = mod[:, 3 * D_MODEL:4 * D_MODEL]
    sc2 = mod[:, 4 * D_MODEL:5 * D_MODEL]
    an = (_rms(a, D_ATTN) * ga_ref[...]).astype(BF16)
    yn = (_rms(yh, D_HYENA) * gh_ref[...]).astype(BF16)
    mix = (jnp.dot(an, wo_ref[0:D_ATTN, :], preferred_element_type=F32)
           + jnp.dot(yn, wo_ref[D_ATTN:, :], preferred_element_type=F32))
    x1 = x + g1 * mix
    h2 = _rms(x1, D_MODEL) * g2_ref[...] * (1.0 + sc2) + sh2
    x1_ref[...] = x1
    sub = D_MODEL // LANES
    for j in range(sub):
        h2_ref[pl.ds(j, rows, stride=sub), :] = h2[:, j * LANES:(j + 1) * LANES]
    logits = lax.dot_general(wr_ref[...], h2, _NT, precision=HI, preferred_element_type=F32)
    p = jnp.exp(logits - jnp.max(logits, axis=0, keepdims=True))
    aff_ref[...] = p / jnp.sum(p, axis=0, keepdims=True)


def out_projection(a, yh, x, mods3, out_g_attn, out_g_hyena, norm2_g, w_out_bf16, w_router_t,
                   *, nb, tt, mod_row):
    b, t, d = x.shape
    steps_per_b = t // tt
    n = b * t
    rows = nb * tt
    grid = (n // rows,)
    xmap = lambda i: (i // steps_per_b, i % steps_per_b, 0)
    vec = lambda w: pl.BlockSpec((1, w), lambda i: (0, 0))
    return pl.pallas_call(
        functools.partial(_outproj_kernel, nb=nb, tt=tt),
        grid=grid,
        in_specs=[pl.BlockSpec((nb, tt, D_ATTN), xmap),
                  pl.BlockSpec((nb, tt, D_HYENA), xmap),
                  pl.BlockSpec((nb, tt, d), xmap),
                  pl.BlockSpec((1, 1, N_MOD * d), lambda i: (mod_row(i), 0, 0)),
                  vec(D_ATTN), vec(D_HYENA), vec(d),
                  pl.BlockSpec((d, d), lambda i: (0, 0)),
                  pl.BlockSpec((N_EXPERTS, d), lambda i: (0, 0))],
        out_specs=[pl.BlockSpec((rows, d), lambda i: (i, 0)),
                   pl.BlockSpec((rows * d // LANES, LANES), lambda i: (i, 0)),
                   pl.BlockSpec((N_EXPERTS, rows), lambda i: (0, i))],
        out_shape=[jax.ShapeDtypeStruct((n, d), F32),
                   jax.ShapeDtypeStruct((n * d // LANES, LANES), F32),
                   jax.ShapeDtypeStruct((N_EXPERTS, n), F32)],
        compiler_params=_params("arbitrary"),
        name="out_projection",
    )(a, yh, x, mods3, out_g_attn.reshape(1, -1), out_g_hyena.reshape(1, -1), norm2_g.reshape(1, -1),
      w_out_bf16, w_router_t)


SEL_CHUNK = 256


def _cumsum_excl(mask_ref, out_ref, n):
    i = lax.broadcasted_iota(jnp.int32, (SEL_CHUNK, SEL_CHUNK), 0)
    j = lax.broadcasted_iota(jnp.int32, (SEL_CHUNK, SEL_CHUNK), 1)
    tri = (i < j).astype(BF16)
    carry = jnp.zeros((N_EXPERTS, 1), F32)
    for ch in range(n // SEL_CHUNK):
        sl = slice(ch * SEL_CHUNK, (ch + 1) * SEL_CHUNK)
        m = mask_ref[:, sl]
        out_ref[:, sl] = jnp.dot(m.astype(BF16), tri, preferred_element_type=F32) + carry
        carry = carry + jnp.sum(m, axis=1, keepdims=True)
    return carry


def _rank_kernel(aff_ref, pos_ref, rank_ref, mask_ref, *, n, cap):
    as_float = lambda bits: lax.bitcast_convert_type(bits, F32)

    def bit_step(i, thr):
        cand = thr | (jnp.int32(1) << (30 - i))
        cnt = jnp.sum((aff_ref[...] >= as_float(cand)).astype(jnp.int32), axis=1, keepdims=True)
        return jnp.where(cnt >= cap, cand, thr)

    thr = lax.fori_loop(0, 31, bit_step, jnp.zeros((N_EXPERTS, 1), jnp.int32))
    aff = aff_ref[...]
    gt = aff >= as_float(thr + 1)
    eq = (aff >= as_float(thr)) & jnp.logical_not(gt)
    need = cap - jnp.sum(gt.astype(jnp.int32), axis=1, keepdims=True)
    mask_ref[...] = eq.astype(F32)
    _cumsum_excl(mask_ref, rank_ref, n)
    sel = gt | (eq & (rank_ref[...] < need.astype(F32)))
    mask_ref[...] = sel.astype(F32)
    _cumsum_excl(mask_ref, rank_ref, n)
    pos_ref[...] = jnp.where(sel, rank_ref[...].astype(jnp.int32), -1)


def _invert_kernel(cs_ref, pos_ref, aff_ref, idx_ref, w_ref, *, n, cap):
    sblk = LANES
    nchunk = n // LANES
    lane_e = lax.broadcasted_iota(jnp.int32, (sblk, N_EXPERTS), 1)
    lane_t = lax.broadcasted_iota(jnp.int32, (sblk, LANES), 1)
    idx_ref[...] = jnp.zeros(idx_ref.shape, jnp.int32)
    w_ref[...] = jnp.zeros(w_ref.shape, F32)
    for e in range(N_EXPERTS):
        def slot_block(sb, c_lo, e=e):
            s0 = pl.multiple_of(sb * sblk, sblk)
            slot = lax.broadcasted_iota(jnp.int32, (sblk, LANES), 0) + s0
            c_lo = lax.while_loop(lambda c: (c < nchunk - 1) & (cs_ref[e, c + 1] <= s0), lambda c: c + 1, c_lo)
            c_hi = lax.while_loop(lambda c: (c < nchunk) & (cs_ref[e, c] < s0 + sblk), lambda c: c + 1, c_lo)

            def chunk(tcn, acc):
                acc_i, acc_w = acc
                t0 = pl.multiple_of(tcn * LANES, LANES)
                hit = pos_ref[e:e + 1, pl.ds(t0, LANES)] == slot
                acc_i = acc_i + jnp.where(hit, lane_t + t0, 0)
                acc_w = acc_w + jnp.where(hit, aff_ref[e:e + 1, pl.ds(t0, LANES)], 0.0)
                return acc_i, acc_w

            acc_i, acc_w = lax.fori_loop(
                c_lo, c_hi, chunk, (jnp.zeros((sblk, LANES), jnp.int32), jnp.zeros((sblk, LANES), F32)))
            col_i = jnp.sum(acc_i, axis=1, keepdims=True)
            col_w = jnp.sum(acc_w, axis=1, keepdims=True)
            rs = pl.ds(s0, sblk)
            idx_ref[rs, :] = jnp.where(lane_e == e, col_i, idx_ref[rs, :])
            w_ref[rs, :] = jnp.where(lane_e == e, col_w, w_ref[rs, :])
            return c_lo

        lax.fori_loop(0, cap // sblk, slot_block, 0)


def expert_select(aff_t):
    e, n = aff_t.shape
    cap = EC_FACTOR * n // N_EXPERTS
    full = pl.BlockSpec((e, n), lambda i: (0, 0))
    pos, rank = pl.pallas_call(
        functools.partial(_rank_kernel, n=n, cap=cap),
        grid=(1,),
        in_specs=[full],
        out_specs=[full, full],
        out_shape=[jax.ShapeDtypeStruct((e, n), jnp.int32), jax.ShapeDtypeStruct((e, n), F32)],
        scratch_shapes=[pltpu.VMEM((e, n), F32)],
        compiler_params=_params("arbitrary"),
        name="expert_rank",
    )(aff_t)
    rank = rank.astype(jnp.int32)
    total = jnp.full((e, 1), cap, jnp.int32)
    chunk_starts = jnp.concatenate([rank[:, ::LANES], total], axis=1)
    idx_t, w_t = pl.pallas_call(
        functools.partial(_invert_kernel, n=n, cap=cap),
        grid_spec=pltpu.PrefetchScalarGridSpec(
            num_scalar_prefetch=1,
            grid=(1,),
            in_specs=[pl.BlockSpec((e, n), lambda i, cs: (0, 0)), pl.BlockSpec((e, n), lambda i, cs: (0, 0))],
            out_specs=[pl.BlockSpec((cap, e), lambda i, cs: (0, 0)), pl.BlockSpec((cap, e), lambda i, cs: (0, 0))]),
        out_shape=[jax.ShapeDtypeStruct((cap, e), jnp.int32), jax.ShapeDtypeStruct((cap, e), F32)],
        compiler_params=_params("arbitrary"),
        name="expert_invert",
    )(chunk_starts, pos, aff_t)
    return idx_t.T, w_t.T, rank


FF_TILE = 256
FFN_ROWS = 512
Y_PAD = 64


TOK_SUB = D_MODEL // LANES


def _tok_copy(src_hbm, tok, xe_ref, slot, sem):
    return pltpu.make_async_copy(src_hbm.at[tok], xe_ref.at[:, slot, :], sem)


def _ffn_kernel(idx_ref, hp_hbm, hs_hbm, w1_ref, w3_ref, w2_ref, y_ref, xe_ref, xb_ref, sem,
                *, cap, gslots, per_step):
    e = pl.program_id(0)
    f = pl.program_id(1)
    ne = pl.num_programs(0)
    nf = pl.num_programs(1)
    rows = 2 * cap

    def issue(ex, s0, count):
        def body(k, c):
            s = s0 + k
            _tok_copy(hp_hbm, idx_ref[ex * gslots + s], xe_ref, s, sem).start()
            _tok_copy(hs_hbm, idx_ref[(ne + ex) * gslots + s], xe_ref, gslots + s, sem).start()
            return c

        lax.fori_loop(0, count, body, 0, unroll=8)

    def wait_all():
        def body(s, c):
            _tok_copy(hp_hbm, 0, xe_ref, s, sem).wait()
            _tok_copy(hs_hbm, 0, xe_ref, gslots + s, sem).wait()
            return c

        lax.fori_loop(0, gslots, body, 0, unroll=8)

    @pl.when((e == 0) & (f == 0))
    def _():
        issue(0, 0, gslots)

    @pl.when(f == 0)
    def _():
        wait_all()
        for g in range(2):
            for j in range(TOK_SUB):
                xb_ref[g * cap:(g + 1) * cap, j * LANES:(j + 1) * LANES] = (
                    xe_ref[j, g * gslots:g * gslots + cap, :].astype(BF16))
        y_ref[0] = jnp.zeros(y_ref.shape[1:], F32)

    w1b = w1_ref[0].astype(BF16)
    w3b = w3_ref[0].astype(BF16)
    w2b = w2_ref[0].astype(BF16)
    for rc in range(rows // FFN_ROWS):
        rs = slice(rc * FFN_ROWS, (rc + 1) * FFN_ROWS)
        xb = xb_ref[rs, :]
        h1 = jnp.dot(xb, w1b, preferred_element_type=F32)
        h3 = jnp.dot(xb, w3b, preferred_element_type=F32)
        hid = (_silu(h1) * h3).astype(BF16)
        y_ref[0, rs, :] += jnp.dot(hid, w2b, preferred_element_type=F32)

    issue(jnp.minimum(e + 1, ne - 1), f * per_step, per_step)

    @pl.when((e == ne - 1) & (f == nf - 1))
    def _():
        wait_all()


def expert_ffn(idx_flat, h2p, h2s, w1, w3, w2, *, cap, gslots):
    ne, d, ff = w1.shape
    rows = 2 * cap
    nf = ff // FF_TILE
    per_step = gslots // nf
    assert per_step * nf == gslots and gslots >= cap
    grid_spec = pltpu.PrefetchScalarGridSpec(
        num_scalar_prefetch=1,
        grid=(ne, nf),
        in_specs=[pl.BlockSpec(memory_space=pl.ANY),
                  pl.BlockSpec(memory_space=pl.ANY),
                  pl.BlockSpec((1, d, FF_TILE), lambda e, f, idx: (e, 0, f)),
                  pl.BlockSpec((1, d, FF_TILE), lambda e, f, idx: (e, 0, f)),
                  pl.BlockSpec((1, FF_TILE, d), lambda e, f, idx: (e, f, 0))],
        out_specs=pl.BlockSpec((1, rows + Y_PAD, d), lambda e, f, idx: (e, 0, 0)),
        scratch_shapes=[pltpu.VMEM((TOK_SUB, 2 * gslots, LANES), F32), pltpu.VMEM((rows, d), BF16),
                        pltpu.SemaphoreType.DMA(())],
    )
    return pl.pallas_call(
        functools.partial(_ffn_kernel, cap=cap, gslots=gslots, per_step=per_step),
        grid_spec=grid_spec,
        out_shape=jax.ShapeDtypeStruct((ne, rows + Y_PAD, d), F32),
        compiler_params=_params("arbitrary", "arbitrary"),
        name="expert_ffn",
    )(idx_flat, h2p, h2s, w1, w3, w2)


COMB_TILE = 512


COMB_GROUP = 8
STAGE_ROWS = COMB_TILE + Y_PAD


def _combine_kernel(idx_ref, off_ref, wt_ref, x1_ref, mod_ref, fg_ref, y_hbm, o_ref, acc_ref, stage_ref, sem,
                    *, row0):
    i = pl.program_id(0)
    base = i * COMB_TILE
    sub = TOK_SUB

    @pl.when(i == 0)
    def _():
        stage_ref[...] = jnp.zeros(stage_ref.shape, F32)

    acc_ref[...] = jnp.zeros(acc_ref.shape, F32)

    def span(e):
        a = off_ref[e, i]
        cnt = off_ref[e, i + 1] - a
        a8 = lax.shift_left(lax.shift_right_logical(a, 3), 3)
        lead = a - a8
        nchunk = lax.shift_right_logical(cnt + lead + (Y_PAD - 1), Y_PAD.bit_length() - 1)
        return a, cnt, a8, lead, nchunk

    def chunk_copy(e, a8, ci, j):
        slot = lax.bitwise_and(e, 1)
        src0 = pl.multiple_of(row0 + a8 + ci * Y_PAD, 8)
        dst0 = pl.multiple_of(ci * Y_PAD, Y_PAD)
        return pltpu.make_async_copy(y_hbm.at[e, pl.ds(src0, Y_PAD), pl.ds(j * LANES, LANES)],
                                     stage_ref.at[slot, pl.ds(dst0, Y_PAD), j, :], sem.at[slot])

    def fetch(e, start):
        _, _, a8, _, nchunk = span(e)

        def body(ci, c):
            for j in range(sub):
                cp = chunk_copy(e, a8, ci, j)
                cp.start() if start else cp.wait()
            return c

        lax.fori_loop(0, nchunk, body, 0)

    fetch(0, True)

    def per_expert(e, c):
        @pl.when(e + 1 < N_EXPERTS)
        def _():
            fetch(e + 1, True)

        fetch(e, False)
        a, cnt, _, lead, _ = span(e)
        slot = lax.bitwise_and(e, 1)

        def group(g, cc):
            s0 = g * COMB_GROUP
            dsts, vals = [], []
            for u in range(COMB_GROUP):
                s = s0 + u
                t = jnp.where(s < cnt, idx_ref[e, a + s] - base, COMB_TILE)
                dsts.append(pl.ds(pl.multiple_of(t * sub, sub), sub))
                vals.append(stage_ref[slot, lead + s] * wt_ref[e, a + s])
            olds = [acc_ref[d, :] for d in dsts]
            for d, old, v in zip(dsts, olds, vals):
                acc_ref[d, :] = old + v
            return cc

        lax.fori_loop(0, lax.shift_right_logical(cnt + (COMB_GROUP - 1), 3), group, 0)
        return c

    lax.fori_loop(0, N_EXPERTS, per_expert, 0)
    g2 = mod_ref[0][:, 5 * D_MODEL:6 * D_MODEL]
    acc = jnp.concatenate([acc_ref[pl.ds(j, COMB_TILE, stride=sub), :] for j in range(sub)], axis=1)
    x = x1_ref[...] + g2 * acc
    o_ref[...] = _rms(x, D_MODEL) * fg_ref[...]


def combine(idx, offs, wts, x1, mods3, final_g, y_all, *, row0, mod_row):
    n, d = x1.shape
    assert COMB_GROUP == 8
    grid_spec = pltpu.PrefetchScalarGridSpec(
        num_scalar_prefetch=3,
        grid=(n // COMB_TILE,),
        in_specs=[pl.BlockSpec((COMB_TILE, d), lambda i, *_: (i, 0)),
                  pl.BlockSpec((1, 1, N_MOD * d), lambda i, *_: (mod_row(i), 0, 0)),
                  pl.BlockSpec((1, d), lambda i, *_: (0, 0)),
                  pl.BlockSpec(memory_space=pl.ANY)],
        out_specs=pl.BlockSpec((COMB_TILE, d), lambda i, *_: (i, 0)),
        scratch_shapes=[pltpu.VMEM(((COMB_TILE + 1) * TOK_SUB, LANES), F32),
                        pltpu.VMEM((2, STAGE_ROWS, TOK_SUB, LANES), F32),
                        pltpu.SemaphoreType.DMA((2,))],
    )
    return pl.pallas_call(
        functools.partial(_combine_kernel, row0=row0),
        grid_spec=grid_spec,
        out_shape=jax.ShapeDtypeStruct((n, d), F32),
        compiler_params=_params("arbitrary"),
        name="combine",
    )(idx, offs, wts, x1, mods3, final_g.reshape(1, d), y_all)


def _tile_offsets(rank, cap):
    total = jnp.full((rank.shape[0], 1), cap, jnp.int32)
    return jnp.concatenate([rank[:, ::COMB_TILE], total], axis=1)


def kernel(x_prompt, x_sample, cache_k, cache_v, c, c_ctx, norm1_g, norm2_g, w_ada, b_ada, w_in, w_out,
           out_g_attn, out_g_hyena, rpb, conv_w, conv_b, filt_w1, filt_b1, filt_w2, filt_b2, filt_w3,
           filt_freq, hyena_bias, w_router, w1, w3, w2, final_g):
    depth = norm1_g.shape[0]
    assert depth == 1
    l = 0
    bp, tp, d = x_prompt.shape
    bs, ts, _ = x_sample.shape

    cond = jnp.zeros((MOD_ROWS, d), F32).at[0].set(c_ctx).at[1:1 + bs].set(c)
    mods3 = ada_mod(cond, w_ada[l], b_ada[l]).reshape(MOD_ROWS, 1, N_MOD * d)

    w_in_b = w_in[l].astype(BF16)
    w_out_b = w_out[l].astype(BF16)
    w_router_t = w_router[l].T
    rpb_ext = jnp.pad(rpb[l], ((0, 0), (0, 0), (RPB_PAD_L, RPB_PAD_R)), mode="edge")
    w1p = jnp.pad(filt_w1[l], ((0, LANES - FILTER_EMB), (0, 0)))

    ctx_row = lambda i: 0
    tt_s = 512
    lat_row = lambda i: 1 + i // (ts // tt_s)

    qkv_p, hy_p, state_k, state_v = in_projection(x_prompt, mods3, norm1_g[l], w_in_b, nb=2, tt=tp,
                                                  mod_row=ctx_row, with_state=True)
    qkv_s, hy_s = in_projection(x_sample, mods3, norm1_g[l], w_in_b, nb=1, tt=tt_s,
                                mod_row=lat_row, with_state=False)

    a_p = context_attention(qkv_p)
    a_s = latent_attention(qkv_s, cache_k[:, l:l + 1], cache_v[:, l:l + 1], rpb_ext)

    def hyena(hy, length):
        fwd_tab, inv_tab = _dft_tables(length)
        fwd_tab = jnp.asarray(fwd_tab)
        ka, kb = hyena_filter(length, w1p, filt_b1[l], filt_w2[l], filt_b2[l], filt_w3[l], filt_freq[l],
                              fwd_tab)
        return hyena_mixer(hy, conv_w[l], conv_b[l], fwd_tab.astype(BF16), jnp.asarray(inv_tab).astype(BF16),
                           ka, kb, hyena_bias[l])

    yh_p = hyena(hy_p, tp)
    yh_s = hyena(hy_s, ts)

    x1_p, h2_p, aff_p = out_projection(a_p, yh_p, x_prompt, mods3, out_g_attn[l], out_g_hyena[l],
                                       norm2_g[l], w_out_b, w_router_t, nb=2, tt=tp, mod_row=ctx_row)
    x1_s, h2_s, aff_s = out_projection(a_s, yh_s, x_sample, mods3, out_g_attn[l], out_g_hyena[l],
                                       norm2_g[l], w_out_b, w_router_t, nb=1, tt=tt_s, mod_row=lat_row)

    idx_p, wt_p, rank_p = expert_select(aff_p)
    idx_s, wt_s, rank_s = expert_select(aff_s)
    cap = idx_p.shape[1]
    nf = w1.shape[3] // FF_TILE
    gslots = -(-cap // (8 * nf)) * 8 * nf
    pad_to = lambda a, width: jnp.pad(a, ((0, 0), (0, width - a.shape[1])))
    idx_flat = jnp.concatenate([pad_to(idx_p, gslots), pad_to(idx_s, gslots)], axis=0).reshape(-1)
    tiles = lambda h2: h2.reshape(-1, TOK_SUB, LANES)
    y_all = expert_ffn(idx_flat, tiles(h2_p), tiles(h2_s), w1[l], w3[l], w2[l], cap=cap, gslots=gslots)

    comb_row_s = lambda i: 1 + i // (ts // COMB_TILE)
    cw = cap + COMB_GROUP
    y_p = combine(pad_to(idx_p, cw), _tile_offsets(rank_p, cap), pad_to(wt_p, cw), x1_p, mods3, final_g, y_all,
                  row0=0, mod_row=ctx_row)
    y_s = combine(pad_to(idx_s, cw), _tile_offsets(rank_s, cap), pad_to(wt_s, cw), x1_s, mods3, final_g, y_all,
                  row0=cap, mod_row=comb_row_s)

    return (y_p.reshape(bp, tp, d), y_s.reshape(bs, ts, d),
            state_k.reshape(bp, depth, N_HEADS, tp, HEAD_DIM), state_v.reshape(bp, depth, N_HEADS, tp, HEAD_DIM))
```

```python
import functools
import math

import numpy as np
import jax
import jax.numpy as jnp
from jax import lax
from jax.experimental import pallas as pl
from jax.experimental.pallas import tpu as pltpu

D_MODEL = 1024
GRID_W = 64
D_ATTN = 512
D_HYENA = 512
HEAD_DIM = 64
N_HEADS = 8
WIN_ROWS = 8
WIN_COLS = 16
FILTER_EMB = 33
FILTER_BANDS = 16
FILTER_HIDDEN = 64
DECAY_TARGET = 1e-2
SHORT_DECAY_PCT = 0.3
LONG_DECAY_PCT = 1.5
MOD_SHIFT = 0.05
N_EXPERTS = 16
EC_FACTOR = 2
EXPERT_FF = 2816
N_MOD = 6
EPS = 1e-6
NEG_INF = -1e30

LANES = 128
MOD_ROWS = 16
VMEM_LIMIT = 56 * 1024 * 1024
HI = lax.Precision.HIGHEST
BF16 = jnp.bfloat16
F32 = jnp.float32

_NT = (((1,), (1,)), ((), ()))


def _params(*sem):
    return pltpu.CompilerParams(dimension_semantics=sem, vmem_limit_bytes=VMEM_LIMIT)


def _rms(x, n):
    return x * lax.rsqrt(jnp.sum(x * x, axis=-1, keepdims=True) * (1.0 / n) + EPS)


def _silu(x):
    return x * (1.0 / (1.0 + jnp.exp(-x)))


def _ada_kernel(c_ref, w_ref, b_ref, o_ref):
    s = _silu(c_ref[...])
    o_ref[...] = jnp.dot(s, w_ref[...], precision=HI, preferred_element_type=F32) + b_ref[...]


def ada_mod(cond, w_ada, b_ada):
    n = w_ada.shape[1]
    tn = 1024
    return pl.pallas_call(
        _ada_kernel,
        grid=(n // tn,),
        in_specs=[pl.BlockSpec((MOD_ROWS, D_MODEL), lambda j: (0, 0)),
                  pl.BlockSpec((D_MODEL, tn), lambda j: (0, j)),
                  pl.BlockSpec((1, tn), lambda j: (0, j))],
        out_specs=pl.BlockSpec((MOD_ROWS, tn), lambda j: (0, j)),
        out_shape=jax.ShapeDtypeStruct((MOD_ROWS, n), F32),
        compiler_params=_params("arbitrary"),
        name="ada_mod",
    )(cond, w_ada, b_ada.reshape(1, n))


def _inproj_kernel(x_ref, mod_ref, g_ref, w_ref, qkv_ref, hy_ref, *state_refs, nb, tt):
    x = x_ref[...].reshape(nb * tt, D_MODEL)
    mod = mod_ref[0]
    sh1 = mod[:, 0:D_MODEL]
    sc1 = mod[:, D_MODEL:2 * D_MODEL]
    h = _rms(x, D_MODEL) * g_ref[...] * (1.0 + sc1) + sh1
    proj = jnp.dot(h.astype(BF16), w_ref[...], preferred_element_type=F32)
    q = proj[:, 0:D_ATTN] * (HEAD_DIM ** -0.5)
    k = proj[:, D_ATTN:2 * D_ATTN]
    v = proj[:, 2 * D_ATTN:3 * D_ATTN]
    qkv_ref[:, :, 0:D_ATTN] = q.reshape(nb, tt, D_ATTN)
    qkv_ref[:, :, D_ATTN:2 * D_ATTN] = k.reshape(nb, tt, D_ATTN)
    qkv_ref[:, :, 2 * D_ATTN:3 * D_ATTN] = v.reshape(nb, tt, D_ATTN)
    hy_ref[...] = proj[:, 3 * D_ATTN:].reshape(nb, tt, 3 * D_HYENA)
    if state_refs:
        sk_ref, sv_ref = state_refs
        for bi in range(nb):
            for hd in range(N_HEADS):
                sl = slice(hd * HEAD_DIM, (hd + 1) * HEAD_DIM)
                sk_ref[bi, hd] = k[bi * tt:(bi + 1) * tt, sl]
                sv_ref[bi, hd] = v[bi * tt:(bi + 1) * tt, sl]


def in_projection(x, mods3, norm_g, w_in_bf16, *, nb, tt, mod_row, with_state):
    b, t, d = x.shape
    steps_per_b = t // tt
    grid = (b // nb * steps_per_b,)
    xmap = lambda i: (i // steps_per_b, i % steps_per_b, 0)
    nproj = w_in_bf16.shape[1]
    out_shape = [jax.ShapeDtypeStruct((b, t, 3 * D_ATTN), F32),
                 jax.ShapeDtypeStruct((b, t, 3 * D_HYENA), F32)]
    out_specs = [pl.BlockSpec((nb, tt, 3 * D_ATTN), xmap),
                 pl.BlockSpec((nb, tt, 3 * D_HYENA), xmap)]
    if with_state:
        assert steps_per_b == 1
        smap = lambda i: (i, 0, 0, 0)
        for _ in range(2):
            out_shape.append(jax.ShapeDtypeStruct((b, N_HEADS, t, HEAD_DIM), F32))
            out_specs.append(pl.BlockSpec((nb, N_HEADS, tt, HEAD_DIM), smap))
    return pl.pallas_call(
        functools.partial(_inproj_kernel, nb=nb, tt=tt),
        grid=grid,
        in_specs=[pl.BlockSpec((nb, tt, d), xmap),
                  pl.BlockSpec((1, 1, N_MOD * d), lambda i: (mod_row(i), 0, 0)),
                  pl.BlockSpec((1, d), lambda i: (0, 0)),
                  pl.BlockSpec((d, nproj), lambda i: (0, 0))],
        out_specs=out_specs,
        out_shape=out_shape,
        compiler_params=_params("arbitrary"),
        name="in_projection",
    )(x, mods3, norm_g.reshape(1, d), w_in_bf16)


def _lane_mask(shape, half):
    lane = lax.broadcasted_iota(jnp.int32, shape, 1)
    return (lane < HEAD_DIM) if half == 0 else (lane >= HEAD_DIM)


def _ctx_attn_kernel(qkv_ref, o_ref):
    for j in range(N_HEADS // 2):
        lo = j * LANES
        qp = qkv_ref[0, :, lo:lo + LANES]
        kp = qkv_ref[0, :, D_ATTN + lo:D_ATTN + lo + LANES].astype(BF16)
        vp = qkv_ref[0, :, 2 * D_ATTN + lo:2 * D_ATTN + lo + LANES]
        out = None
        for half in range(2):
            m = _lane_mask(qp.shape, half)
            qm = jnp.where(m, qp, 0.0).astype(BF16)
            vm = jnp.where(m, vp, 0.0).astype(BF16)
            s = lax.dot_general(qm, kp, _NT, preferred_element_type=F32)
            p = jnp.exp(s - jnp.max(s, axis=-1, keepdims=True))
            l = jnp.sum(p, axis=-1, keepdims=True)
            o = jnp.dot(p.astype(BF16), vm, preferred_element_type=F32) * (1.0 / l)
            out = o if out is None else out + o
        o_ref[0, :, lo:lo + LANES] = out


def context_attention(qkv):
    b, t, _ = qkv.shape
    return pl.pallas_call(
        _ctx_attn_kernel,
        grid=(b,),
        in_specs=[pl.BlockSpec((1, t, 3 * D_ATTN), lambda i: (i, 0, 0))],
        out_specs=pl.BlockSpec((1, t, D_ATTN), lambda i: (i, 0, 0)),
        out_shape=jax.ShapeDtypeStruct((b, t, D_ATTN), F32),
        compiler_params=_params("arbitrary"),
        name="context_attention",
    )(qkv)


N_TPAIR = 2 * WIN_ROWS - 2
RPB_PAD_L = GRID_W - WIN_COLS
RPB_PAD_R = LANES - RPB_PAD_L - (2 * WIN_COLS - 1)


def _row_start(r, rows, wr):
    return min(max(r - wr // 2, 0), rows - wr)


def _build_bias(rpb_ref, tp_ref, m_ref, rows, wr):
    shape = (GRID_W, LANES)
    c = lax.broadcasted_iota(jnp.int32, shape, 0)
    lane = lax.broadcasted_iota(jnp.int32, shape, 1)
    kc = lane & (GRID_W - 1)
    ws = jnp.clip(c - WIN_COLS // 2, 0, GRID_W - WIN_COLS)
    col_ok = (kc >= ws) & (kc < ws + WIN_COLS)
    first = lane < GRID_W
    for hh in range(2):
        for dr in range(N_TPAIR):
            xa = jnp.broadcast_to(rpb_ref[hh, dr:dr + 1, :], shape)
            xb = jnp.broadcast_to(rpb_ref[hh, dr + 1:dr + 2, :], shape)
            ta = pltpu.roll(xa, GRID_W + 1, 1, stride=1, stride_axis=0)
            tb = pltpu.roll(xb, 1, 1, stride=1, stride_axis=0)
            tp_ref[hh, dr] = jnp.where(col_ok, jnp.where(first, ta, tb), NEG_INF)
        for r in range(rows):
            rs = _row_start(r, rows, wr)
            for jj in range(rows // 2):
                in_a = rs <= 2 * jj < rs + wr
                in_b = rs <= 2 * jj + 1 < rs + wr
                dr = 2 * jj - r + (WIN_ROWS - 1)
                if in_a and in_b:
                    blk = tp_ref[hh, dr]
                elif in_a:
                    blk = jnp.where(first, tp_ref[hh, dr], NEG_INF)
                elif in_b:
                    blk = jnp.where(first, NEG_INF, tp_ref[hh, dr])
                else:
                    blk = jnp.full(shape, NEG_INF, F32)
                m_ref[hh, r * GRID_W:(r + 1) * GRID_W, jj * LANES:(jj + 1) * LANES] = blk


def _key_range(qb, rows_per_blk, rows, wr):
    r0 = qb * rows_per_blk
    lo = min(_row_start(r, rows, wr) for r in range(r0, r0 + rows_per_blk))
    hi = max(_row_start(r, rows, wr) for r in range(r0, r0 + rows_per_blk)) + wr
    grp = 256 // GRID_W
    return (lo // grp) * 256, -(-hi // grp) * 256


def _lat_attn_kernel(q_ref, k_ref, v_ref, ck_ref, cv_ref, rpb_ref, o_ref, tp_ref, m_ref, *, rows, wr):
    @pl.when(pl.program_id(1) == 0)
    def _():
        _build_bias(rpb_ref, tp_ref, m_ref, rows, wr)

    l_tok = rows * GRID_W
    qblk = 256
    rows_per_blk = qblk // GRID_W
    sel_r = lax.broadcasted_iota(jnp.int32, (HEAD_DIM, LANES), 0)
    sel_c = lax.broadcasted_iota(jnp.int32, (HEAD_DIM, LANES), 1)
    for qb in range(l_tok // qblk):
        c0, c1 = _key_range(qb, rows_per_blk, rows, wr)
        qp = q_ref[0, qb * qblk:(qb + 1) * qblk, :]
        kp = k_ref[0, c0:c1, :].astype(BF16)
        vp = v_ref[0, c0:c1, :]
        out = None
        for hh in range(2):
            place = (sel_c == sel_r + hh * HEAD_DIM).astype(BF16)
            ck = jnp.dot(ck_ref[0, 0, hh].astype(BF16), place, preferred_element_type=F32).astype(BF16)
            cv = jnp.dot(cv_ref[0, 0, hh].astype(BF16), place, preferred_element_type=F32).astype(BF16)
            m = _lane_mask(qp.shape, hh)
            qm = jnp.where(m, qp, 0.0).astype(BF16)
            vm = jnp.where(_lane_mask(vp.shape, hh), vp, 0.0).astype(BF16)
            s_win = lax.dot_general(qm, kp, _NT, preferred_element_type=F32)
            s_win = s_win + m_ref[hh, qb * qblk:(qb + 1) * qblk, c0:c1]
            s_ctx = lax.dot_general(qm, ck, _NT, preferred_element_type=F32)
            mx = jnp.maximum(jnp.max(s_win, axis=-1, keepdims=True), jnp.max(s_ctx, axis=-1, keepdims=True))
            p_win = jnp.exp(s_win - mx)
            p_ctx = jnp.exp(s_ctx - mx)
            l = jnp.sum(p_win, axis=-1, keepdims=True) + jnp.sum(p_ctx, axis=-1, keepdims=True)
            o = (jnp.dot(p_win.astype(BF16), vm, preferred_element_type=F32)
                 + jnp.dot(p_ctx.astype(BF16), cv, preferred_element_type=F32)) * (1.0 / l)
            out = o if out is None else out + o
        o_ref[0, qb * qblk:(qb + 1) * qblk, :] = out


def latent_attention(qkv, cache_k, cache_v, rpb_ext):
    b, l_tok, _ = qkv.shape
    rows = l_tok // GRID_W
    wr = min(WIN_ROWS, rows)
    assert rows % 4 == 0 and wr == WIN_ROWS
    npair = N_HEADS // 2
    past = cache_k.shape[3]
    blk = lambda off: pl.BlockSpec((1, l_tok, LANES), lambda j, i: (i, 0, off + j))
    cspec = pl.BlockSpec((1, 1, 2, past, HEAD_DIM), lambda j, i: (i, 0, j, 0, 0))
    return pl.pallas_call(
        functools.partial(_lat_attn_kernel, rows=rows, wr=wr),
        grid=(npair, b),
        in_specs=[blk(0), blk(npair), blk(2 * npair), cspec, cspec,
                  pl.BlockSpec((2, 2 * WIN_ROWS - 1, LANES), lambda j, i: (j, 0, 0))],
        out_specs=pl.BlockSpec((1, l_tok, LANES), lambda j, i: (i, 0, j)),
        out_shape=jax.ShapeDtypeStruct((b, l_tok, D_ATTN), F32),
        scratch_shapes=[pltpu.VMEM((2, N_TPAIR, GRID_W, LANES), F32),
                        pltpu.VMEM((2, l_tok, l_tok), F32)],
        compiler_params=_params("arbitrary", "arbitrary"),
        name="latent_attention",
    )(qkv, qkv, qkv, cache_k, cache_v, rpb_ext)


def _dft_tables(length):
    n2 = 2 * length
    k = np.arange(length, dtype=np.float64)[:, None]
    t = np.arange(length, dtype=np.float64)[None, :]
    ang = 2.0 * np.pi * ((k * t) % n2) / n2
    fa = np.cos(ang)
    fb = -np.sin(ang)
    fb[0, :] = np.cos(np.pi * t[0])
    fwd = np.concatenate([fa, fb], axis=0)
    ga = 2.0 * np.cos(ang).T
    ga[:, 0] = 1.0
    gb = -2.0 * np.sin(ang).T
    gb[:, 0] = np.cos(np.pi * t[0])
    inv = np.concatenate([ga, gb], axis=1)
    return fwd.astype(np.float32), inv.astype(np.float32)


def _filter_consts(length):
    t = np.linspace(0.0, 1.0, length, dtype=np.float32)[:, None]
    w = (np.float32(2.0 * math.pi / length) * np.arange(length, dtype=np.float32))[:, None]
    bands = np.linspace(1e-4, FILTER_BANDS - 1, FILTER_BANDS, dtype=np.float32)[None, :]
    z = np.concatenate([t, np.cos(bands * w), -np.sin(bands * w)], axis=-1).astype(np.float32)
    zp = np.zeros((length, LANES), np.float32)
    zp[:, :FILTER_EMB] = z
    deltas = np.abs(np.linspace(math.log(DECAY_TARGET) / LONG_DECAY_PCT,
                                math.log(DECAY_TARGET) / SHORT_DECAY_PCT, D_HYENA, dtype=np.float32))
    window = (np.exp(-t * deltas) + np.float32(MOD_SHIFT)).astype(np.float32)
    return zp, window


def _split3(a, b):
    a_hi = a.astype(BF16)
    a_lo = (a - a_hi.astype(F32)).astype(BF16)
    b_hi = b.astype(BF16)
    b_lo = (b - b_hi.astype(F32)).astype(BF16)
    return (jnp.dot(a_hi, b_hi, preferred_element_type=F32)
            + jnp.dot(a_hi, b_lo, preferred_element_type=F32)
            + jnp.dot(a_lo, b_hi, preferred_element_type=F32))


def _filter_kernel(z_ref, win_ref, w1_ref, b1_ref, w2_ref, b2_ref, w3_ref, fr_ref, fwd_ref,
                   ka_ref, kb_ref, s_ref, d_ref, *, length):
    j = pl.program_id(0)

    @pl.when(j == 0)
    def _():
        fr = fr_ref[...]
        hid = jnp.sin(fr * (jnp.dot(z_ref[...], w1_ref[...], precision=HI, preferred_element_type=F32)
                            + b1_ref[...]))
        hid = jnp.sin(fr * (jnp.dot(hid, w2_ref[...], precision=HI, preferred_element_type=F32)
                            + b2_ref[...]))
        filt = jnp.dot(hid, w3_ref[...], precision=HI, preferred_element_type=F32)
        nc = 2 * D_HYENA
        win = win_ref[...]
        win2 = jnp.concatenate([win, win], axis=1)
        fw = filt[:, 0:nc] * win2
        row = lax.broadcasted_iota(jnp.int32, (length, nc), 0)
        bw = jnp.where(row == 0, 0.0, filt[:, nc:2 * nc] * win2)
        inv = 1.0 / jnp.sum(jnp.abs(fw) + jnp.abs(bw), axis=0, keepdims=True)
        s_ref[...] = (fw + bw) * inv
        d_ref[...] = (fw - bw) * inv

    tm = ka_ref.shape[0]
    nsteps = length // tm
    fa = fwd_ref[0]
    fb = fwd_ref[1]
    ka_ref[...] = _split3(fa, s_ref[...])
    kb = _split3(fb, d_ref[...])
    nyq = _split3(fb[0:8, :], s_ref[...])[0:1, :]
    row = lax.broadcasted_iota(jnp.int32, kb.shape, 0)
    kb_ref[...] = jnp.where((row == 0) & (j == 0), nyq, kb)
    del nsteps


def hyena_filter(length, w1p, b1, w2, b2, w3, freq, fwd_tab):
    zp, window = _filter_consts(length)
    nc = 2 * D_HYENA
    tm = 256
    fwd3 = fwd_tab.reshape(2, length, length)
    full = lambda a: pl.BlockSpec(a.shape, lambda j: (0,) * a.ndim)
    args = [jnp.asarray(zp), jnp.asarray(window), w1p, b1.reshape(1, -1), w2, b2.reshape(1, -1), w3,
            freq.reshape(1, -1)]
    return pl.pallas_call(
        functools.partial(_filter_kernel, length=length),
        grid=(length // tm,),
        in_specs=[full(a) for a in args] + [pl.BlockSpec((2, tm, length), lambda j: (0, j, 0))],
        out_specs=[pl.BlockSpec((tm, nc), lambda j: (j, 0))] * 2,
        out_shape=[jax.ShapeDtypeStruct((length, nc), F32)] * 2,
        scratch_shapes=[pltpu.VMEM((length, nc), F32), pltpu.VMEM((length, nc), F32)],
        compiler_params=_params("arbitrary"),
        name="hyena_filter",
    )(*args, fwd3)


def _long_conv(u, f_ref, g_ref, ka, kb, bias, length):
    x = jnp.dot(f_ref[...], u.astype(BF16), preferred_element_type=F32)
    xa = x[0:length]
    xb = x[length:2 * length]
    row = lax.broadcasted_iota(jnp.int32, xa.shape, 0)
    dc = row == 0
    ya = xa * ka - jnp.where(dc, 0.0, xb * kb)
    yb = jnp.where(dc, xb * kb, xa * kb + xb * ka)
    y = (jnp.dot(g_ref[:, 0:length], ya.astype(BF16), preferred_element_type=F32)
         + jnp.dot(g_ref[:, length:2 * length], yb.astype(BF16), preferred_element_type=F32))
    return y * (1.0 / (2 * length)) + u * bias


def _hyena_kernel(v_ref, x1_ref, x2_ref, cw_ref, cb_ref, f_ref, g_ref, ka_ref, kb_ref, hb_ref, o_ref,
                  *, length, tc):
    row = lax.broadcasted_iota(jnp.int32, (length, tc), 0)

    def short(u_ref, part):
        u = u_ref[0]
        up = jnp.where(row == 0, 0.0, pltpu.roll(u, 1, 0))
        dn = jnp.where(row == length - 1, 0.0, pltpu.roll(u, length - 1, 0))
        w = cw_ref[part]
        return up * w[0:1] + u * w[1:2] + dn * w[2:3] + cb_ref[part]

    v = short(v_ref, 0)
    x1 = short(x1_ref, 1)
    x2 = short(x2_ref, 2)
    z = x1 * _long_conv(v, f_ref, g_ref, ka_ref[:, 0:tc], kb_ref[:, 0:tc], hb_ref[0], length)
    y = x2 * _long_conv(z, f_ref, g_ref, ka_ref[:, tc:2 * tc], kb_ref[:, tc:2 * tc], hb_ref[1], length)
    o_ref[0] = y


def hyena_mixer(hy, conv_w, conv_b, fwd_bf16, inv_bf16, ka, kb, hy_bias):
    b, length, _ = hy.shape
    tc = 256
    nct = D_HYENA // tc
    regroup = lambda a: a.reshape(length, 2, nct, tc).transpose(2, 0, 1, 3).reshape(nct, length, 2 * tc)
    cw = conv_w.reshape(3, 3, nct, 1, tc).transpose(1, 2, 0, 3, 4).reshape(3, nct, 3, tc)
    cb = conv_b.reshape(3, nct, 1, tc)
    hb = hy_bias.reshape(2, nct, 1, tc).transpose(1, 0, 2, 3)
    part = lambda p: pl.BlockSpec((1, length, tc), lambda c, i: (i, 0, p * nct + c))
    return pl.pallas_call(
        functools.partial(_hyena_kernel, length=length, tc=tc),
        grid=(nct, b),
        in_specs=[part(0), part(1), part(2),
                  pl.BlockSpec((3, None, 3, tc), lambda c, i: (0, c, 0, 0)),
                  pl.BlockSpec((3, None, 1, tc), lambda c, i: (0, c, 0, 0)),
                  pl.BlockSpec((2 * length, length), lambda c, i: (0, 0)),
                  pl.BlockSpec((length, 2 * length), lambda c, i: (0, 0)),
                  pl.BlockSpec((None, length, 2 * tc), lambda c, i: (c, 0, 0)),
                  pl.BlockSpec((None, length, 2 * tc), lambda c, i: (c, 0, 0)),
                  pl.BlockSpec((None, 2, 1, tc), lambda c, i: (c, 0, 0, 0))],
        out_specs=pl.BlockSpec((1, length, tc), lambda c, i: (i, 0, c)),
        out_shape=jax.ShapeDtypeStruct((b, length, D_HYENA), F32),
        compiler_params=_params("arbitrary", "arbitrary"),
        name="hyena_mixer",
    )(hy, hy, hy, cw, cb, fwd_bf16, inv_bf16, regroup(ka), regroup(kb), hb)


def _outproj_kernel(a_ref, y_ref, x_ref, mod_ref, ga_ref, gh_ref, g2_ref, wo_ref, wr_ref,
                    x1_ref, h2_ref, aff_ref, *, nb, tt):
    rows = nb * tt
    a = a_ref[...].reshape(rows, D_ATTN)
    yh = y_ref[...].reshape(rows, D_HYENA)
    x = x_ref[...].reshape(rows, D_MODEL)
    mod = mod_ref[0]
    g1 = mod[:, 2 * D_MODEL:3 * D_MODEL]
    sh2 = mod[:, 3 * D_MODEL:4 * D_MODEL]
    sc2 = mod[:, 4 * D_MODEL:5 * D_MODEL]
    an = (_rms(a, D_ATTN) * ga_ref[...]).astype(BF16)
    yn = (_rms(yh, D_HYENA) * gh_ref[...]).astype(BF16)
    mix = (jnp.dot(an, wo_ref[0:D_ATTN, :], preferred_element_type=F32)
           + jnp.dot(yn, wo_ref[D_ATTN:, :], preferred_element_type=F32))
    x1 = x + g1 * mix
    h2 = _rms(x1, D_MODEL) * g2_ref[...] * (1.0 + sc2) + sh2
    x1_ref[...] = x1
    sub = D_MODEL // LANES
    for j in range(sub):
        h2_ref[pl.ds(j, rows, stride=sub), :] = h2[:, j * LANES:(j + 1) * LANES]
    logits = lax.dot_general(wr_ref[...], h2, _NT, precision=HI, preferred_element_type=F32)
    p = jnp.exp(logits - jnp.max(logits, axis=0, keepdims=True))
    aff_ref[...] = p / jnp.sum(p, axis=0, keepdims=True)


def out_projection(a, yh, x, mods3, out_g_attn, out_g_hyena, norm2_g, w_out_bf16, w_router_t,
                   *, nb, tt, mod_row):
    b, t, d = x.shape
    steps_per_b = t // tt
    n = b * t
    rows = nb * tt
    grid = (n // rows,)
    xmap = lambda i: (i // steps_per_b, i % steps_per_b, 0)
    vec = lambda w: pl.BlockSpec((1, w), lambda i: (0, 0))
    return pl.pallas_call(
        functools.partial(_outproj_kernel, nb=nb, tt=tt),
        grid=grid,
        in_specs=[pl.BlockSpec((nb, tt, D_ATTN), xmap),
                  pl.BlockSpec((nb, tt, D_HYENA), xmap),
                  pl.BlockSpec((nb, tt, d), xmap),
                  pl.BlockSpec((1, 1, N_MOD * d), lambda i: (mod_row(i), 0, 0)),
                  vec(D_ATTN), vec(D_HYENA), vec(d),
                  pl.BlockSpec((d, d), lambda i: (0, 0)),
                  pl.BlockSpec((N_EXPERTS, d), lambda i: (0, 0))],
        out_specs=[pl.BlockSpec((rows, d), lambda i: (i, 0)),
                   pl.BlockSpec((rows * d // LANES, LANES), lambda i: (i, 0)),
                   pl.BlockSpec((N_EXPERTS, rows), lambda i: (0, i))],
        out_shape=[jax.ShapeDtypeStruct((n, d), F32),
                   jax.ShapeDtypeStruct((n * d // LANES, LANES), F32),
                   jax.ShapeDtypeStruct((N_EXPERTS, n), F32)],
        compiler_params=_params("arbitrary"),
        name="out_projection",
    )(a, yh, x, mods3, out_g_attn.reshape(1, -1), out_g_hyena.reshape(1, -1), norm2_g.reshape(1, -1),
      w_out_bf16, w_router_t)


SEL_CHUNK = 256


def _cumsum_excl(mask_ref, out_ref, n):
    i = lax.broadcasted_iota(jnp.int32, (SEL_CHUNK, SEL_CHUNK), 0)
    j = lax.broadcasted_iota(jnp.int32, (SEL_CHUNK, SEL_CHUNK), 1)
    tri = (i < j).astype(BF16)
    carry = jnp.zeros((N_EXPERTS, 1), F32)
    for ch in range(n // SEL_CHUNK):
        sl = slice(ch * SEL_CHUNK, (ch + 1) * SEL_CHUNK)
        m = mask_ref[:, sl]
        out_ref[:, sl] = jnp.dot(m.astype(BF16), tri, preferred_element_type=F32) + carry
        carry = carry + jnp.sum(m, axis=1, keepdims=True)
    return carry


def _rank_kernel(aff_ref, pos_ref, rank_ref, mask_ref, *, n, cap):
    as_float = lambda bits: lax.bitcast_convert_type(bits, F32)

    def bit_step(i, thr):
        cand = thr | (jnp.int32(1) << (30 - i))
        cnt = jnp.sum((aff_ref[...] >= as_float(cand)).astype(jnp.int32), axis=1, keepdims=True)
        return jnp.where(cnt >= cap, cand, thr)

    thr = lax.fori_loop(0, 31, bit_step, jnp.zeros((N_EXPERTS, 1), jnp.int32))
    aff = aff_ref[...]
    gt = aff >= as_float(thr + 1)
    eq = (aff >= as_float(thr)) & jnp.logical_not(gt)
    need = cap - jnp.sum(gt.astype(jnp.int32), axis=1, keepdims=True)
    mask_ref[...] = eq.astype(F32)
    _cumsum_excl(mask_ref, rank_ref, n)
    sel = gt | (eq & (rank_ref[...] < need.astype(F32)))
    mask_ref[...] = sel.astype(F32)
    _cumsum_excl(mask_ref, rank_ref, n)
    pos_ref[...] = jnp.where(sel, rank_ref[...].astype(jnp.int32), -1)


def _invert_kernel(cs_ref, pos_ref, aff_ref, idx_ref, w_ref, *, n, cap):
    sblk = LANES
    nchunk = n // LANES
    lane_e = lax.broadcasted_iota(jnp.int32, (sblk, N_EXPERTS), 1)
    lane_t = lax.broadcasted_iota(jnp.int32, (sblk, LANES), 1)
    idx_ref[...] = jnp.zeros(idx_ref.shape, jnp.int32)
    w_ref[...] = jnp.zeros(w_ref.shape, F32)
    for e in range(N_EXPERTS):
        def slot_block(sb, c_lo, e=e):
            s0 = pl.multiple_of(sb * sblk, sblk)
            slot = lax.broadcasted_iota(jnp.int32, (sblk, LANES), 0) + s0
            c_lo = lax.while_loop(lambda c: (c < nchunk - 1) & (cs_ref[e, c + 1] <= s0), lambda c: c + 1, c_lo)
            c_hi = lax.while_loop(lambda c: (c < nchunk) & (cs_ref[e, c] < s0 + sblk), lambda c: c + 1, c_lo)

            def chunk(tcn, acc):
                acc_i, acc_w = acc
                t0 = pl.multiple_of(tcn * LANES, LANES)
                hit = pos_ref[e:e + 1, pl.ds(t0, LANES)] == slot
                acc_i = acc_i + jnp.where(hit, lane_t + t0, 0)
                acc_w = acc_w + jnp.where(hit, aff_ref[e:e + 1, pl.ds(t0, LANES)], 0.0)
                return acc_i, acc_w

            acc_i, acc_w = lax.fori_loop(
                c_lo, c_hi, chunk, (jnp.zeros((sblk, LANES), jnp.int32), jnp.zeros((sblk, LANES), F32)))
            col_i = jnp.sum(acc_i, axis=1, keepdims=True)
            col_w = jnp.sum(acc_w, axis=1, keepdims=True)
            rs = pl.ds(s0, sblk)
            idx_ref[rs, :] = jnp.where(lane_e == e, col_i, idx_ref[rs, :])
            w_ref[rs, :] = jnp.where(lane_e == e, col_w, w_ref[rs, :])
            return c_lo

        lax.fori_loop(0, cap // sblk, slot_block, 0)


def expert_select(aff_t):
    e, n = aff_t.shape
    cap = EC_FACTOR * n // N_EXPERTS
    full = pl.BlockSpec((e, n), lambda i: (0, 0))
    pos, rank = pl.pallas_call(
        functools.partial(_rank_kernel, n=n, cap=cap),
        grid=(1,),
        in_specs=[full],
        out_specs=[full, full],
        out_shape=[jax.ShapeDtypeStruct((e, n), jnp.int32), jax.ShapeDtypeStruct((e, n), F32)],
        scratch_shapes=[pltpu.VMEM((e, n), F32)],
        compiler_params=_params("arbitrary"),
        name="expert_rank",
    )(aff_t)
    rank = rank.astype(jnp.int32)
    total = jnp.full((e, 1), cap, jnp.int32)
    chunk_starts = jnp.concatenate([rank[:, ::LANES], total], axis=1)
    idx_t, w_t = pl.pallas_call(
        functools.partial(_invert_kernel, n=n, cap=cap),
        grid_spec=pltpu.PrefetchScalarGridSpec(
            num_scalar_prefetch=1,
            grid=(1,),
            in_specs=[pl.BlockSpec((e, n), lambda i, cs: (0, 0)), pl.BlockSpec((e, n), lambda i, cs: (0, 0))],
            out_specs=[pl.BlockSpec((cap, e), lambda i, cs: (0, 0)), pl.BlockSpec((cap, e), lambda i, cs: (0, 0))]),
        out_shape=[jax.ShapeDtypeStruct((cap, e), jnp.int32), jax.ShapeDtypeStruct((cap, e), F32)],
        compiler_params=_params("arbitrary"),
        name="expert_invert",
    )(chunk_starts, pos, aff_t)
    return idx_t.T, w_t.T, rank


FF_TILE = 256
FFN_ROWS = 512
Y_PAD = 64


TOK_SUB = D_MODEL // LANES


def _tok_copy(src_hbm, tok, xe_ref, slot, sem):
    return pltpu.make_async_copy(src_hbm.at[tok], xe_ref.at[:, slot, :], sem)


def _ffn_kernel(idx_ref, hp_hbm, hs_hbm, w1_ref, w3_ref, w2_ref, y_ref, xe_ref, xb_ref, sem,
                *, cap, gslots, per_step):
    e = pl.program_id(0)
    f = pl.program_id(1)
    ne = pl.num_programs(0)
    nf = pl.num_programs(1)
    rows = 2 * cap

    def start_pair(ex, s):
        _tok_copy(hp_hbm, idx_ref[ex * gslots + s], xe_ref, s, sem).start()
        _tok_copy(hs_hbm, idx_ref[(ne + ex) * gslots + s], xe_ref, gslots + s, sem).start()

    def wait_all():
        def body(s, c):
            _tok_copy(hp_hbm, 0, xe_ref, s, sem).wait()
            _tok_copy(hs_hbm, 0, xe_ref, gslots + s, sem).wait()
            return c

        lax.fori_loop(0, gslots, body, 0, unroll=8)

    @pl.when((e == 0) & (f == 0))
    def _():
        def body(s, c):
            start_pair(0, s)
            return c

        lax.fori_loop(0, gslots, body, 0, unroll=8)

    @pl.when(f == 0)
    def _():
        wait_all()
        for g in range(2):
            for j in range(TOK_SUB):
                xb_ref[g * cap:(g + 1) * cap, j * LANES:(j + 1) * LANES] = (
                    xe_ref[j, g * gslots:g * gslots + cap, :].astype(BF16))
        y_ref[0] = jnp.zeros(y_ref.shape[1:], F32)

    w1b = w1_ref[0].astype(BF16)
    w3b = w3_ref[0].astype(BF16)
    w2b = w2_ref[0].astype(BF16)
    nrc = rows // FFN_ROWS
    nxt = jnp.minimum(e + 1, ne - 1)
    for rc in range(nrc):
        rs = slice(rc * FFN_ROWS, (rc + 1) * FFN_ROWS)
        xb = xb_ref[rs, :]
        h1 = jnp.dot(xb, w1b, preferred_element_type=F32)
        h3 = jnp.dot(xb, w3b, preferred_element_type=F32)
        hid = (_silu(h1) * h3).astype(BF16)
        y_ref[0, rs, :] += jnp.dot(hid, w2b, preferred_element_type=F32)
        for k in range(rc * per_step // nrc, (rc + 1) * per_step // nrc):
            start_pair(nxt, f * per_step + k)

    @pl.when((e == ne - 1) & (f == nf - 1))
    def _():
        wait_all()


def expert_ffn(idx_flat, h2p, h2s, w1, w3, w2, *, cap, gslots):
    ne, d, ff = w1.shape
    rows = 2 * cap
    nf = ff // FF_TILE
    per_step = gslots // nf
    assert per_step * nf == gslots and gslots >= cap
    grid_spec = pltpu.PrefetchScalarGridSpec(
        num_scalar_prefetch=1,
        grid=(ne, nf),
        in_specs=[pl.BlockSpec(memory_space=pl.ANY),
                  pl.BlockSpec(memory_space=pl.ANY),
                  pl.BlockSpec((1, d, FF_TILE), lambda e, f, idx: (e, 0, f)),
                  pl.BlockSpec((1, d, FF_TILE), lambda e, f, idx: (e, 0, f)),
                  pl.BlockSpec((1, FF_TILE, d), lambda e, f, idx: (e, f, 0))],
        out_specs=pl.BlockSpec((1, rows + Y_PAD, d), lambda e, f, idx: (e, 0, 0)),
        scratch_shapes=[pltpu.VMEM((TOK_SUB, 2 * gslots, LANES), F32), pltpu.VMEM((rows, d), BF16),
                        pltpu.SemaphoreType.DMA(())],
    )
    return pl.pallas_call(
        functools.partial(_ffn_kernel, cap=cap, gslots=gslots, per_step=per_step),
        grid_spec=grid_spec,
        out_shape=jax.ShapeDtypeStruct((ne, rows + Y_PAD, d), F32),
        compiler_params=_params("arbitrary", "arbitrary"),
        name="expert_ffn",
    )(idx_flat, h2p, h2s, w1, w3, w2)


COMB_TILE = 512


COMB_GROUP = 8
STAGE_ROWS = COMB_TILE + Y_PAD


COMB_DEPTH = 8


def _combine_kernel(doff_ref, off_ref, wt_ref, x1_ref, mod_ref, fg_ref, y_hbm, o_ref, acc_ref, stage_ref, sem,
                    *, row0):
    i = pl.program_id(0)
    nspan = pl.num_programs(0) * N_EXPERTS
    sub = TOK_SUB
    ahead = COMB_DEPTH - 1

    def span(p):
        e = lax.bitwise_and(p, N_EXPERTS - 1)
        ti = lax.shift_right_logical(p, N_EXPERTS.bit_length() - 1)
        a = off_ref[e, ti]
        cnt = off_ref[e, ti + 1] - a
        a8 = lax.shift_left(lax.shift_right_logical(a, 3), 3)
        lead = a - a8
        nchunk = lax.shift_right_logical(cnt + lead + (Y_PAD - 1), Y_PAD.bit_length() - 1)
        return e, a, cnt, a8, lead, nchunk

    def fetch(p, start):
        e, _, _, a8, _, nchunk = span(p)
        slot = lax.bitwise_and(p, COMB_DEPTH - 1)

        def body(ci, c):
            src0 = pl.multiple_of(row0 + a8 + ci * Y_PAD, 8)
            dst0 = pl.multiple_of(ci * Y_PAD, Y_PAD)
            for j in range(sub):
                cp = pltpu.make_async_copy(y_hbm.at[e, pl.ds(src0, Y_PAD), pl.ds(j * LANES, LANES)],
                                           stage_ref.at[slot, pl.ds(dst0, Y_PAD), j, :], sem.at[slot])
                cp.start() if start else cp.wait()
            return c

        lax.fori_loop(0, nchunk, body, 0)

    @pl.when(i == 0)
    def _():
        for p in range(ahead):
            fetch(jnp.int32(p), True)

    acc_ref[...] = jnp.zeros(acc_ref.shape, F32)

    def add_rows(slot, e, a, lead, s0, count):
        dsts = [pl.ds(pl.multiple_of(doff_ref[e, a + s0 + u], sub), sub) for u in range(count)]
        vals = [stage_ref[slot, lead + s0 + u] * wt_ref[e, a + s0 + u] for u in range(count)]
        olds = [acc_ref[d, :] for d in dsts]
        for d, old, v in zip(dsts, olds, vals):
            acc_ref[d, :] = old + v

    def per_expert(ei, c):
        p = i * N_EXPERTS + ei

        @pl.when(p + ahead < nspan)
        def _():
            fetch(p + ahead, True)

        fetch(p, False)
        e, a, cnt, _, lead, _ = span(p)
        slot = lax.bitwise_and(p, COMB_DEPTH - 1)
        full = lax.shift_right_logical(cnt, 3)

        def group(g, cc):
            add_rows(slot, e, a, lead, g * COMB_GROUP, COMB_GROUP)
            return cc

        lax.fori_loop(0, full, group, 0)

        def single(s, cc):
            add_rows(slot, e, a, lead, s, 1)
            return cc

        lax.fori_loop(full * COMB_GROUP, cnt, single, 0)
        return c

    lax.fori_loop(0, N_EXPERTS, per_expert, 0)
    g2 = mod_ref[0][:, 5 * D_MODEL:6 * D_MODEL]
    acc = jnp.concatenate([acc_ref[pl.ds(j, COMB_TILE, stride=sub), :] for j in range(sub)], axis=1)
    x = x1_ref[...] + g2 * acc
    o_ref[...] = _rms(x, D_MODEL) * fg_ref[...]


def combine(idx, offs, wts, x1, mods3, final_g, y_all, *, row0, mod_row):
    n, d = x1.shape
    assert COMB_GROUP == 8 and COMB_DEPTH & (COMB_DEPTH - 1) == 0
    doff = (idx % COMB_TILE) * TOK_SUB
    grid_spec = pltpu.PrefetchScalarGridSpec(
        num_scalar_prefetch=3,
        grid=(n // COMB_TILE,),
        in_specs=[pl.BlockSpec((COMB_TILE, d), lambda i, *_: (i, 0)),
                  pl.BlockSpec((1, 1, N_MOD * d), lambda i, *_: (mod_row(i), 0, 0)),
                  pl.BlockSpec((1, d), lambda i, *_: (0, 0)),
                  pl.BlockSpec(memory_space=pl.ANY)],
        out_specs=pl.BlockSpec((COMB_TILE, d), lambda i, *_: (i, 0)),
        scratch_shapes=[pltpu.VMEM((COMB_TILE * TOK_SUB, LANES), F32),
                        pltpu.VMEM((COMB_DEPTH, STAGE_ROWS, TOK_SUB, LANES), F32),
                        pltpu.SemaphoreType.DMA((COMB_DEPTH,))],
    )
    return pl.pallas_call(
        functools.partial(_combine_kernel, row0=row0),
        grid_spec=grid_spec,
        out_shape=jax.ShapeDtypeStruct((n, d), F32),
        compiler_params=_params("arbitrary"),
        name="combine",
    )(doff, offs, wts, x1, mods3, final_g.reshape(1, d), y_all)


def _tile_offsets(rank, cap):
    total = jnp.full((rank.shape[0], 1), cap, jnp.int32)
    return jnp.concatenate([rank[:, ::COMB_TILE], total], axis=1)


def kernel(x_prompt, x_sample, cache_k, cache_v, c, c_ctx, norm1_g, norm2_g, w_ada, b_ada, w_in, w_out,
           out_g_attn, out_g_hyena, rpb, conv_w, conv_b, filt_w1, filt_b1, filt_w2, filt_b2, filt_w3,
           filt_freq, hyena_bias, w_router, w1, w3, w2, final_g):
    depth = norm1_g.shape[0]
    assert depth == 1
    l = 0
    bp, tp, d = x_prompt.shape
    bs, ts, _ = x_sample.shape

    cond = jnp.zeros((MOD_ROWS, d), F32).at[0].set(c_ctx).at[1:1 + bs].set(c)
    mods3 = ada_mod(cond, w_ada[l], b_ada[l]).reshape(MOD_ROWS, 1, N_MOD * d)

    w_in_b = w_in[l].astype(BF16)
    w_out_b = w_out[l].astype(BF16)
    w_router_t = w_router[l].T
    rpb_ext = jnp.pad(rpb[l], ((0, 0), (0, 0), (RPB_PAD_L, RPB_PAD_R)), mode="edge")
    w1p = jnp.pad(filt_w1[l], ((0, LANES - FILTER_EMB), (0, 0)))

    ctx_row = lambda i: 0
    tt_s = 512
    lat_row = lambda i: 1 + i // (ts // tt_s)

    qkv_p, hy_p, state_k, state_v = in_projection(x_prompt, mods3, norm1_g[l], w_in_b, nb=2, tt=tp,
                                                  mod_row=ctx_row, with_state=True)
    qkv_s, hy_s = in_projection(x_sample, mods3, norm1_g[l], w_in_b, nb=1, tt=tt_s,
                                mod_row=lat_row, with_state=False)

    a_p = context_attention(qkv_p)
    a_s = latent_attention(qkv_s, cache_k[:, l:l + 1], cache_v[:, l:l + 1], rpb_ext)

    def hyena(hy, length):
        fwd_tab, inv_tab = _dft_tables(length)
        fwd_tab = jnp.asarray(fwd_tab)
        ka, kb = hyena_filter(length, w1p, filt_b1[l], filt_w2[l], filt_b2[l], filt_w3[l], filt_freq[l],
                              fwd_tab)
        return hyena_mixer(hy, conv_w[l], conv_b[l], fwd_tab.astype(BF16), jnp.asarray(inv_tab).astype(BF16),
                           ka, kb, hyena_bias[l])

    yh_p = hyena(hy_p, tp)
    yh_s = hyena(hy_s, ts)

    x1_p, h2_p, aff_p = out_projection(a_p, yh_p, x_prompt, mods3, out_g_attn[l], out_g_hyena[l],
                                       norm2_g[l], w_out_b, w_router_t, nb=2, tt=tp, mod_row=ctx_row)
    x1_s, h2_s, aff_s = out_projection(a_s, yh_s, x_sample, mods3, out_g_attn[l], out_g_hyena[l],
                                       norm2_g[l], w_out_b, w_router_t, nb=1, tt=tt_s, mod_row=lat_row)

    idx_p, wt_p, rank_p = expert_select(aff_p)
    idx_s, wt_s, rank_s = expert_select(aff_s)
    cap = idx_p.shape[1]
    nf = w1.shape[3] // FF_TILE
    gslots = -(-cap // (8 * nf)) * 8 * nf
    pad_to = lambda a, width: jnp.pad(a, ((0, 0), (0, width - a.shape[1])))
    idx_flat = jnp.concatenate([pad_to(idx_p, gslots), pad_to(idx_s, gslots)], axis=0).reshape(-1)
    tiles = lambda h2: h2.reshape(-1, TOK_SUB, LANES)
    y_all = expert_ffn(idx_flat, tiles(h2_p), tiles(h2_s), w1[l], w3[l], w2[l], cap=cap, gslots=gslots)

    comb_row_s = lambda i: 1 + i // (ts // COMB_TILE)
    y_p = combine(idx_p, _tile_offsets(rank_p, cap), wt_p, x1_p, mods3, final_g, y_all, row0=0, mod_row=ctx_row)
    y_s = combine(idx_s, _tile_offsets(rank_s, cap), wt_s, x1_s, mods3, final_g, y_all, row0=cap,
                  mod_row=comb_row_s)

    return (y_p.reshape(bp, tp, d), y_s.reshape(bs, ts, d),
            state_k.reshape(bp, depth, N_HEADS, tp, HEAD_DIM), state_v.reshape(bp, depth, N_HEADS, tp, HEAD_DIM))
```

```python
import functools
import math

import numpy as np
import jax
import jax.numpy as jnp
from jax import lax
from jax.experimental import pallas as pl
from jax.experimental.pallas import tpu as pltpu

D_MODEL = 1024
GRID_W = 64
D_ATTN = 512
D_HYENA = 512
HEAD_DIM = 64
N_HEADS = 8
WIN_ROWS = 8
WIN_COLS = 16
FILTER_EMB = 33
FILTER_BANDS = 16
FILTER_HIDDEN = 64
DECAY_TARGET = 1e-2
SHORT_DECAY_PCT = 0.3
LONG_DECAY_PCT = 1.5
MOD_SHIFT = 0.05
N_EXPERTS = 16
EC_FACTOR = 2
EXPERT_FF = 2816
N_MOD = 6
EPS = 1e-6
NEG_INF = -1e30

LANES = 128
MOD_ROWS = 16
VMEM_LIMIT = 56 * 1024 * 1024
HI = lax.Precision.HIGHEST
BF16 = jnp.bfloat16
F32 = jnp.float32

_NT = (((1,), (1,)), ((), ()))


def _params(*sem):
    return pltpu.CompilerParams(dimension_semantics=sem, vmem_limit_bytes=VMEM_LIMIT)


def _rms(x, n):
    return x * lax.rsqrt(jnp.sum(x * x, axis=-1, keepdims=True) * (1.0 / n) + EPS)


def _silu(x):
    return x * (1.0 / (1.0 + jnp.exp(-x)))


def _ada_kernel(c_ref, w_ref, b_ref, o_ref):
    s = _silu(c_ref[...])
    o_ref[...] = jnp.dot(s, w_ref[...], precision=HI, preferred_element_type=F32) + b_ref[...]


def ada_mod(cond, w_ada, b_ada):
    n = w_ada.shape[1]
    tn = 1024
    return pl.pallas_call(
        _ada_kernel,
        grid=(n // tn,),
        in_specs=[pl.BlockSpec((MOD_ROWS, D_MODEL), lambda j: (0, 0)),
                  pl.BlockSpec((D_MODEL, tn), lambda j: (0, j)),
                  pl.BlockSpec((1, tn), lambda j: (0, j))],
        out_specs=pl.BlockSpec((MOD_ROWS, tn), lambda j: (0, j)),
        out_shape=jax.ShapeDtypeStruct((MOD_ROWS, n), F32),
        compiler_params=_params("arbitrary"),
        name="ada_mod",
    )(cond, w_ada, b_ada.reshape(1, n))


def _inproj_kernel(x_ref, mod_ref, g_ref, w_ref, qkv_ref, hy_ref, *state_refs, nb, tt):
    x = x_ref[...].reshape(nb * tt, D_MODEL)
    mod = mod_ref[0]
    sh1 = mod[:, 0:D_MODEL]
    sc1 = mod[:, D_MODEL:2 * D_MODEL]
    h = _rms(x, D_MODEL) * g_ref[...] * (1.0 + sc1) + sh1
    proj = jnp.dot(h.astype(BF16), w_ref[...], preferred_element_type=F32)
    q = proj[:, 0:D_ATTN] * (HEAD_DIM ** -0.5)
    k = proj[:, D_ATTN:2 * D_ATTN]
    v = proj[:, 2 * D_ATTN:3 * D_ATTN]
    qkv_ref[:, :, 0:D_ATTN] = q.reshape(nb, tt, D_ATTN)
    qkv_ref[:, :, D_ATTN:2 * D_ATTN] = k.reshape(nb, tt, D_ATTN)
    qkv_ref[:, :, 2 * D_ATTN:3 * D_ATTN] = v.reshape(nb, tt, D_ATTN)
    hy_ref[...] = proj[:, 3 * D_ATTN:].reshape(nb, tt, 3 * D_HYENA)
    if state_refs:
        sk_ref, sv_ref = state_refs
        for bi in range(nb):
            for hd in range(N_HEADS):
                sl = slice(hd * HEAD_DIM, (hd + 1) * HEAD_DIM)
                sk_ref[bi, hd] = k[bi * tt:(bi + 1) * tt, sl]
                sv_ref[bi, hd] = v[bi * tt:(bi + 1) * tt, sl]


def in_projection(x, mods3, norm_g, w_in_bf16, *, nb, tt, mod_row, with_state):
    b, t, d = x.shape
    steps_per_b = t // tt
    grid = (b // nb * steps_per_b,)
    xmap = lambda i: (i // steps_per_b, i % steps_per_b, 0)
    nproj = w_in_bf16.shape[1]
    out_shape = [jax.ShapeDtypeStruct((b, t, 3 * D_ATTN), F32),
                 jax.ShapeDtypeStruct((b, t, 3 * D_HYENA), F32)]
    out_specs = [pl.BlockSpec((nb, tt, 3 * D_ATTN), xmap),
                 pl.BlockSpec((nb, tt, 3 * D_HYENA), xmap)]
    if with_state:
        assert steps_per_b == 1
        smap = lambda i: (i, 0, 0, 0)
        for _ in range(2):
            out_shape.append(jax.ShapeDtypeStruct((b, N_HEADS, t, HEAD_DIM), F32))
            out_specs.append(pl.BlockSpec((nb, N_HEADS, tt, HEAD_DIM), smap))
    return pl.pallas_call(
        functools.partial(_inproj_kernel, nb=nb, tt=tt),
        grid=grid,
        in_specs=[pl.BlockSpec((nb, tt, d), xmap),
                  pl.BlockSpec((1, 1, N_MOD * d), lambda i: (mod_row(i), 0, 0)),
                  pl.BlockSpec((1, d), lambda i: (0, 0)),
                  pl.BlockSpec((d, nproj), lambda i: (0, 0))],
        out_specs=out_specs,
        out_shape=out_shape,
        compiler_params=_params("arbitrary"),
        name="in_projection",
    )(x, mods3, norm_g.reshape(1, d), w_in_bf16)


def _lane_mask(shape, half):
    lane = lax.broadcasted_iota(jnp.int32, shape, 1)
    return (lane < HEAD_DIM) if half == 0 else (lane >= HEAD_DIM)


def _ctx_attn_kernel(qkv_ref, o_ref):
    for j in range(N_HEADS // 2):
        lo = j * LANES
        qp = qkv_ref[0, :, lo:lo + LANES]
        kp = qkv_ref[0, :, D_ATTN + lo:D_ATTN + lo + LANES].astype(BF16)
        vp = qkv_ref[0, :, 2 * D_ATTN + lo:2 * D_ATTN + lo + LANES]
        out = None
        for half in range(2):
            m = _lane_mask(qp.shape, half)
            qm = jnp.where(m, qp, 0.0).astype(BF16)
            vm = jnp.where(m, vp, 0.0).astype(BF16)
            s = lax.dot_general(qm, kp, _NT, preferred_element_type=F32)
            p = jnp.exp(s - jnp.max(s, axis=-1, keepdims=True))
            l = jnp.sum(p, axis=-1, keepdims=True)
            o = jnp.dot(p.astype(BF16), vm, preferred_element_type=F32) * (1.0 / l)
            out = o if out is None else out + o
        o_ref[0, :, lo:lo + LANES] = out


def context_attention(qkv):
    b, t, _ = qkv.shape
    return pl.pallas_call(
        _ctx_attn_kernel,
        grid=(b,),
        in_specs=[pl.BlockSpec((1, t, 3 * D_ATTN), lambda i: (i, 0, 0))],
        out_specs=pl.BlockSpec((1, t, D_ATTN), lambda i: (i, 0, 0)),
        out_shape=jax.ShapeDtypeStruct((b, t, D_ATTN), F32),
        compiler_params=_params("arbitrary"),
        name="context_attention",
    )(qkv)


N_TPAIR = 2 * WIN_ROWS - 2
RPB_PAD_L = GRID_W - WIN_COLS
RPB_PAD_R = LANES - RPB_PAD_L - (2 * WIN_COLS - 1)


def _row_start(r, rows, wr):
    return min(max(r - wr // 2, 0), rows - wr)


def _build_bias(rpb_ref, tp_ref, m_ref, rows, wr):
    shape = (GRID_W, LANES)
    c = lax.broadcasted_iota(jnp.int32, shape, 0)
    lane = lax.broadcasted_iota(jnp.int32, shape, 1)
    kc = lane & (GRID_W - 1)
    ws = jnp.clip(c - WIN_COLS // 2, 0, GRID_W - WIN_COLS)
    col_ok = (kc >= ws) & (kc < ws + WIN_COLS)
    first = lane < GRID_W
    for hh in range(2):
        for dr in range(N_TPAIR):
            xa = jnp.broadcast_to(rpb_ref[hh, dr:dr + 1, :], shape)
            xb = jnp.broadcast_to(rpb_ref[hh, dr + 1:dr + 2, :], shape)
            ta = pltpu.roll(xa, GRID_W + 1, 1, stride=1, stride_axis=0)
            tb = pltpu.roll(xb, 1, 1, stride=1, stride_axis=0)
            tp_ref[hh, dr] = jnp.where(col_ok, jnp.where(first, ta, tb), NEG_INF)
        for r in range(rows):
            rs = _row_start(r, rows, wr)
            for jj in range(rows // 2):
                in_a = rs <= 2 * jj < rs + wr
                in_b = rs <= 2 * jj + 1 < rs + wr
                dr = 2 * jj - r + (WIN_ROWS - 1)
                if in_a and in_b:
                    blk = tp_ref[hh, dr]
                elif in_a:
                    blk = jnp.where(first, tp_ref[hh, dr], NEG_INF)
                elif in_b:
                    blk = jnp.where(first, NEG_INF, tp_ref[hh, dr])
                else:
                    blk = jnp.full(shape, NEG_INF, F32)
                m_ref[hh, r * GRID_W:(r + 1) * GRID_W, jj * LANES:(jj + 1) * LANES] = blk


def _key_range(qb, rows_per_blk, rows, wr):
    r0 = qb * rows_per_blk
    lo = min(_row_start(r, rows, wr) for r in range(r0, r0 + rows_per_blk))
    hi = max(_row_start(r, rows, wr) for r in range(r0, r0 + rows_per_blk)) + wr
    grp = 256 // GRID_W
    return (lo // grp) * 256, -(-hi // grp) * 256


def _lat_attn_kernel(q_ref, k_ref, v_ref, ck_ref, cv_ref, rpb_ref, o_ref, tp_ref, m_ref, *, rows, wr):
    @pl.when(pl.program_id(1) == 0)
    def _():
        _build_bias(rpb_ref, tp_ref, m_ref, rows, wr)

    l_tok = rows * GRID_W
    qblk = 256
    rows_per_blk = qblk // GRID_W
    sel_r = lax.broadcasted_iota(jnp.int32, (HEAD_DIM, LANES), 0)
    sel_c = lax.broadcasted_iota(jnp.int32, (HEAD_DIM, LANES), 1)
    for qb in range(l_tok // qblk):
        c0, c1 = _key_range(qb, rows_per_blk, rows, wr)
        qp = q_ref[0, qb * qblk:(qb + 1) * qblk, :]
        kp = k_ref[0, c0:c1, :].astype(BF16)
        vp = v_ref[0, c0:c1, :]
        out = None
        for hh in range(2):
            place = (sel_c == sel_r + hh * HEAD_DIM).astype(BF16)
            ck = jnp.dot(ck_ref[0, 0, hh].astype(BF16), place, preferred_element_type=F32).astype(BF16)
            cv = jnp.dot(cv_ref[0, 0, hh].astype(BF16), place, preferred_element_type=F32).astype(BF16)
            m = _lane_mask(qp.shape, hh)
            qm = jnp.where(m, qp, 0.0).astype(BF16)
            vm = jnp.where(_lane_mask(vp.shape, hh), vp, 0.0).astype(BF16)
            s_win = lax.dot_general(qm, kp, _NT, preferred_element_type=F32)
            s_win = s_win + m_ref[hh, qb * qblk:(qb + 1) * qblk, c0:c1]
            s_ctx = lax.dot_general(qm, ck, _NT, preferred_element_type=F32)
            mx = jnp.maximum(jnp.max(s_win, axis=-1, keepdims=True), jnp.max(s_ctx, axis=-1, keepdims=True))
            p_win = jnp.exp(s_win - mx)
            p_ctx = jnp.exp(s_ctx - mx)
            l = jnp.sum(p_win, axis=-1, keepdims=True) + jnp.sum(p_ctx, axis=-1, keepdims=True)
            o = (jnp.dot(p_win.astype(BF16), vm, preferred_element_type=F32)
                 + jnp.dot(p_ctx.astype(BF16), cv, preferred_element_type=F32)) * (1.0 / l)
            out = o if out is None else out + o
        o_ref[0, qb * qblk:(qb + 1) * qblk, :] = out


def latent_attention(qkv, cache_k, cache_v, rpb_ext):
    b, l_tok, _ = qkv.shape
    rows = l_tok // GRID_W
    wr = min(WIN_ROWS, rows)
    assert rows % 4 == 0 and wr == WIN_ROWS
    npair = N_HEADS // 2
    past = cache_k.shape[3]
    blk = lambda off: pl.BlockSpec((1, l_tok, LANES), lambda j, i: (i, 0, off + j))
    cspec = pl.BlockSpec((1, 1, 2, past, HEAD_DIM), lambda j, i: (i, 0, j, 0, 0))
    return pl.pallas_call(
        functools.partial(_lat_attn_kernel, rows=rows, wr=wr),
        grid=(npair, b),
        in_specs=[blk(0), blk(npair), blk(2 * npair), cspec, cspec,
                  pl.BlockSpec((2, 2 * WIN_ROWS - 1, LANES), lambda j, i: (j, 0, 0))],
        out_specs=pl.BlockSpec((1, l_tok, LANES), lambda j, i: (i, 0, j)),
        out_shape=jax.ShapeDtypeStruct((b, l_tok, D_ATTN), F32),
        scratch_shapes=[pltpu.VMEM((2, N_TPAIR, GRID_W, LANES), F32),
                        pltpu.VMEM((2, l_tok, l_tok), F32)],
        compiler_params=_params("arbitrary", "arbitrary"),
        name="latent_attention",
    )(qkv, qkv, qkv, cache_k, cache_v, rpb_ext)


def _dft_tables(length):
    n2 = 2 * length
    k = np.arange(length, dtype=np.float64)[:, None]
    t = np.arange(length, dtype=np.float64)[None, :]
    ang = 2.0 * np.pi * ((k * t) % n2) / n2
    fa = np.cos(ang)
    fb = -np.sin(ang)
    fb[0, :] = np.cos(np.pi * t[0])
    fwd = np.concatenate([fa, fb], axis=0)
    ga = 2.0 * np.cos(ang).T
    ga[:, 0] = 1.0
    gb = -2.0 * np.sin(ang).T
    gb[:, 0] = np.cos(np.pi * t[0])
    inv = np.concatenate([ga, gb], axis=1)
    return fwd.astype(np.float32), inv.astype(np.float32)


def _filter_consts(length):
    t = np.linspace(0.0, 1.0, length, dtype=np.float32)[:, None]
    w = (np.float32(2.0 * math.pi / length) * np.arange(length, dtype=np.float32))[:, None]
    bands = np.linspace(1e-4, FILTER_BANDS - 1, FILTER_BANDS, dtype=np.float32)[None, :]
    z = np.concatenate([t, np.cos(bands * w), -np.sin(bands * w)], axis=-1).astype(np.float32)
    zp = np.zeros((length, LANES), np.float32)
    zp[:, :FILTER_EMB] = z
    deltas = np.abs(np.linspace(math.log(DECAY_TARGET) / LONG_DECAY_PCT,
                                math.log(DECAY_TARGET) / SHORT_DECAY_PCT, D_HYENA, dtype=np.float32))
    window = (np.exp(-t * deltas) + np.float32(MOD_SHIFT)).astype(np.float32)
    return zp, window


def _split3(a, b):
    a_hi = a.astype(BF16)
    a_lo = (a - a_hi.astype(F32)).astype(BF16)
    b_hi = b.astype(BF16)
    b_lo = (b - b_hi.astype(F32)).astype(BF16)
    return (jnp.dot(a_hi, b_hi, preferred_element_type=F32)
            + jnp.dot(a_hi, b_lo, preferred_element_type=F32)
            + jnp.dot(a_lo, b_hi, preferred_element_type=F32))


def _filter_kernel(z_ref, win_ref, w1_ref, b1_ref, w2_ref, b2_ref, w3_ref, fr_ref, fwd_ref,
                   ka_ref, kb_ref, s_ref, d_ref, *, length):
    j = pl.program_id(0)

    @pl.when(j == 0)
    def _():
        fr = fr_ref[...]
        hid = jnp.sin(fr * (jnp.dot(z_ref[...], w1_ref[...], precision=HI, preferred_element_type=F32)
                            + b1_ref[...]))
        hid = jnp.sin(fr * (jnp.dot(hid, w2_ref[...], precision=HI, preferred_element_type=F32)
                            + b2_ref[...]))
        filt = jnp.dot(hid, w3_ref[...], precision=HI, preferred_element_type=F32)
        nc = 2 * D_HYENA
        win = win_ref[...]
        win2 = jnp.concatenate([win, win], axis=1)
        fw = filt[:, 0:nc] * win2
        row = lax.broadcasted_iota(jnp.int32, (length, nc), 0)
        bw = jnp.where(row == 0, 0.0, filt[:, nc:2 * nc] * win2)
        inv = 1.0 / jnp.sum(jnp.abs(fw) + jnp.abs(bw), axis=0, keepdims=True)
        s_ref[...] = (fw + bw) * inv
        d_ref[...] = (fw - bw) * inv

    tm = ka_ref.shape[0]
    nsteps = length // tm
    fa = fwd_ref[0]
    fb = fwd_ref[1]
    ka_ref[...] = _split3(fa, s_ref[...])
    kb = _split3(fb, d_ref[...])
    nyq = _split3(fb[0:8, :], s_ref[...])[0:1, :]
    row = lax.broadcasted_iota(jnp.int32, kb.shape, 0)
    kb_ref[...] = jnp.where((row == 0) & (j == 0), nyq, kb)
    del nsteps


def hyena_filter(length, w1p, b1, w2, b2, w3, freq, fwd_tab):
    zp, window = _filter_consts(length)
    nc = 2 * D_HYENA
    tm = 256
    fwd3 = fwd_tab.reshape(2, length, length)
    full = lambda a: pl.BlockSpec(a.shape, lambda j: (0,) * a.ndim)
    args = [jnp.asarray(zp), jnp.asarray(window), w1p, b1.reshape(1, -1), w2, b2.reshape(1, -1), w3,
            freq.reshape(1, -1)]
    return pl.pallas_call(
        functools.partial(_filter_kernel, length=length),
        grid=(length // tm,),
        in_specs=[full(a) for a in args] + [pl.BlockSpec((2, tm, length), lambda j: (0, j, 0))],
        out_specs=[pl.BlockSpec((tm, nc), lambda j: (j, 0))] * 2,
        out_shape=[jax.ShapeDtypeStruct((length, nc), F32)] * 2,
        scratch_shapes=[pltpu.VMEM((length, nc), F32), pltpu.VMEM((length, nc), F32)],
        compiler_params=_params("arbitrary"),
        name="hyena_filter",
    )(*args, fwd3)


def _dft_forward(u, f_ref):
    return jnp.dot(f_ref[...], u.astype(BF16), preferred_element_type=F32)


def _spectrum_product(x, ka, kb, length):
    xa = x[0:length]
    xb = x[length:2 * length]
    row = lax.broadcasted_iota(jnp.int32, xa.shape, 0)
    dc = row == 0
    ya = xa * ka - jnp.where(dc, 0.0, xb * kb)
    yb = jnp.where(dc, xb * kb, xa * kb + xb * ka)
    return ya.astype(BF16), yb.astype(BF16)


def _dft_inverse(ya, yb, g_ref, length):
    y = (jnp.dot(g_ref[:, 0:length], ya, preferred_element_type=F32)
         + jnp.dot(g_ref[:, length:2 * length], yb, preferred_element_type=F32))
    return y * (1.0 / (2 * length))


def _hyena_kernel(v_ref, x1_ref, x2_ref, cw_ref, cb_ref, f_ref, g_ref, ka_ref, kb_ref, hb_ref, o_ref,
                  *, length, tc, nb):
    row = lax.broadcasted_iota(jnp.int32, (length, tc), 0)

    def short(u, part):
        up = jnp.where(row == 0, 0.0, pltpu.roll(u, 1, 0))
        dn = jnp.where(row == length - 1, 0.0, pltpu.roll(u, length - 1, 0))
        w = cw_ref[part]
        return up * w[0:1] + u * w[1:2] + dn * w[2:3] + cb_ref[part]

    bs = range(nb)
    u = [short(v_ref[bi], 0) for bi in bs]
    for order in range(2):
        ka = ka_ref[:, order * tc:(order + 1) * tc]
        kb = kb_ref[:, order * tc:(order + 1) * tc]
        spec = [_dft_forward(u[bi], f_ref) for bi in bs]
        spec = [_spectrum_product(x, ka, kb, length) for x in spec]
        conv = [_dft_inverse(ya, yb, g_ref, length) + u[bi] * hb_ref[order] for bi, (ya, yb) in zip(bs, spec)]
        gate_ref = x1_ref if order == 0 else x2_ref
        u = [short(gate_ref[bi], order + 1) * conv[bi] for bi in bs]
    for bi in bs:
        o_ref[bi] = u[bi]


def hyena_mixer(hy, conv_w, conv_b, fwd_bf16, inv_bf16, ka, kb, hy_bias, *, nb):
    b, length, _ = hy.shape
    tc = 256
    nct = D_HYENA // tc
    regroup = lambda a: a.reshape(length, 2, nct, tc).transpose(2, 0, 1, 3).reshape(nct, length, 2 * tc)
    cw = conv_w.reshape(3, 3, nct, 1, tc).transpose(1, 2, 0, 3, 4).reshape(3, nct, 3, tc)
    cb = conv_b.reshape(3, nct, 1, tc)
    hb = hy_bias.reshape(2, nct, 1, tc).transpose(1, 0, 2, 3)
    part = lambda p: pl.BlockSpec((nb, length, tc), lambda c, i: (i, 0, p * nct + c))
    return pl.pallas_call(
        functools.partial(_hyena_kernel, length=length, tc=tc, nb=nb),
        grid=(nct, b // nb),
        in_specs=[part(0), part(1), part(2),
                  pl.BlockSpec((3, None, 3, tc), lambda c, i: (0, c, 0, 0)),
                  pl.BlockSpec((3, None, 1, tc), lambda c, i: (0, c, 0, 0)),
                  pl.BlockSpec((2 * length, length), lambda c, i: (0, 0)),
                  pl.BlockSpec((length, 2 * length), lambda c, i: (0, 0)),
                  pl.BlockSpec((None, length, 2 * tc), lambda c, i: (c, 0, 0)),
                  pl.BlockSpec((None, length, 2 * tc), lambda c, i: (c, 0, 0)),
                  pl.BlockSpec((None, 2, 1, tc), lambda c, i: (c, 0, 0, 0))],
        out_specs=pl.BlockSpec((nb, length, tc), lambda c, i: (i, 0, c)),
        out_shape=jax.ShapeDtypeStruct((b, length, D_HYENA), F32),
        compiler_params=_params("arbitrary", "arbitrary"),
        name="hyena_mixer",
    )(hy, hy, hy, cw, cb, fwd_bf16, inv_bf16, regroup(ka), regroup(kb), hb)


def _outproj_kernel(a_ref, y_ref, x_ref, mod_ref, ga_ref, gh_ref, g2_ref, wo_ref, wr_ref,
                    x1_ref, h2_ref, aff_ref, *, nb, tt):
    rows = nb * tt
    a = a_ref[...].reshape(rows, D_ATTN)
    yh = y_ref[...].reshape(rows, D_HYENA)
    x = x_ref[...].reshape(rows, D_MODEL)
    mod = mod_ref[0]
    g1 = mod[:, 2 * D_MODEL:3 * D_MODEL]
    sh2 = mod[:, 3 * D_MODEL:4 * D_MODEL]
    sc2 = mod[:, 4 * D_MODEL:5 * D_MODEL]
    an = (_rms(a, D_ATTN) * ga_ref[...]).astype(BF16)
    yn = (_rms(yh, D_HYENA) * gh_ref[...]).astype(BF16)
    mix = (jnp.dot(an, wo_ref[0:D_ATTN, :], preferred_element_type=F32)
           + jnp.dot(yn, wo_ref[D_ATTN:, :], preferred_element_type=F32))
    x1 = x + g1 * mix
    h2 = _rms(x1, D_MODEL) * g2_ref[...] * (1.0 + sc2) + sh2
    x1_ref[...] = x1
    sub = D_MODEL // LANES
    for j in range(sub):
        h2_ref[pl.ds(j, rows, stride=sub), :] = h2[:, j * LANES:(j + 1) * LANES]
    logits = lax.dot_general(wr_ref[...], h2, _NT, precision=HI, preferred_element_type=F32)
    p = jnp.exp(logits - jnp.max(logits, axis=0, keepdims=True))
    aff_ref[...] = p / jnp.sum(p, axis=0, keepdims=True)


def out_projection(a, yh, x, mods3, out_g_attn, out_g_hyena, norm2_g, w_out_bf16, w_router_t,
                   *, nb, tt, mod_row):
    b, t, d = x.shape
    steps_per_b = t // tt
    n = b * t
    rows = nb * tt
    grid = (n // rows,)
    xmap = lambda i: (i // steps_per_b, i % steps_per_b, 0)
    vec = lambda w: pl.BlockSpec((1, w), lambda i: (0, 0))
    return pl.pallas_call(
        functools.partial(_outproj_kernel, nb=nb, tt=tt),
        grid=grid,
        in_specs=[pl.BlockSpec((nb, tt, D_ATTN), xmap),
                  pl.BlockSpec((nb, tt, D_HYENA), xmap),
                  pl.BlockSpec((nb, tt, d), xmap),
                  pl.BlockSpec((1, 1, N_MOD * d), lambda i: (mod_row(i), 0, 0)),
                  vec(D_ATTN), vec(D_HYENA), vec(d),
                  pl.BlockSpec((d, d), lambda i: (0, 0)),
                  pl.BlockSpec((N_EXPERTS, d), lambda i: (0, 0))],
        out_specs=[pl.BlockSpec((rows, d), lambda i: (i, 0)),
                   pl.BlockSpec((rows * d // LANES, LANES), lambda i: (i, 0)),
                   pl.BlockSpec((N_EXPERTS, rows), lambda i: (0, i))],
        out_shape=[jax.ShapeDtypeStruct((n, d), F32),
                   jax.ShapeDtypeStruct((n * d // LANES, LANES), F32),
                   jax.ShapeDtypeStruct((N_EXPERTS, n), F32)],
        compiler_params=_params("arbitrary"),
        name="out_projection",
    )(a, yh, x, mods3, out_g_attn.reshape(1, -1), out_g_hyena.reshape(1, -1), norm2_g.reshape(1, -1),
      w_out_bf16, w_router_t)


SEL_CHUNK = 256


def _cumsum_excl(mask_ref, out_ref, n):
    i = lax.broadcasted_iota(jnp.int32, (SEL_CHUNK, SEL_CHUNK), 0)
    j = lax.broadcasted_iota(jnp.int32, (SEL_CHUNK, SEL_CHUNK), 1)
    tri = (i < j).astype(BF16)
    carry = jnp.zeros((N_EXPERTS, 1), F32)
    for ch in range(n // SEL_CHUNK):
        sl = slice(ch * SEL_CHUNK, (ch + 1) * SEL_CHUNK)
        m = mask_ref[:, sl]
        out_ref[:, sl] = jnp.dot(m.astype(BF16), tri, preferred_element_type=F32) + carry
        carry = carry + jnp.sum(m, axis=1, keepdims=True)
    return carry


def _rank_kernel(aff_ref, pos_ref, rank_ref, mask_ref, *, n, cap):
    as_float = lambda bits: lax.bitcast_convert_type(bits, F32)

    def bit_step(i, thr):
        cand = thr | (jnp.int32(1) << (30 - i))
        cnt = jnp.sum((aff_ref[...] >= as_float(cand)).astype(jnp.int32), axis=1, keepdims=True)
        return jnp.where(cnt >= cap, cand, thr)

    thr = lax.fori_loop(0, 31, bit_step, jnp.zeros((N_EXPERTS, 1), jnp.int32))
    aff = aff_ref[...]
    gt = aff >= as_float(thr + 1)
    eq = (aff >= as_float(thr)) & jnp.logical_not(gt)
    need = cap - jnp.sum(gt.astype(jnp.int32), axis=1, keepdims=True)
    mask_ref[...] = eq.astype(F32)
    _cumsum_excl(mask_ref, rank_ref, n)
    sel = gt | (eq & (rank_ref[...] < need.astype(F32)))
    mask_ref[...] = sel.astype(F32)
    _cumsum_excl(mask_ref, rank_ref, n)
    pos_ref[...] = jnp.where(sel, rank_ref[...].astype(jnp.int32), -1)


def _invert_kernel(cs_ref, pos_ref, aff_ref, idx_ref, w_ref, *, n, cap):
    sblk = LANES
    nchunk = n // LANES
    lane_e = lax.broadcasted_iota(jnp.int32, (sblk, N_EXPERTS), 1)
    lane_t = lax.broadcasted_iota(jnp.int32, (sblk, LANES), 1)
    idx_ref[...] = jnp.zeros(idx_ref.shape, jnp.int32)
    w_ref[...] = jnp.zeros(w_ref.shape, F32)
    for e in range(N_EXPERTS):
        def slot_block(sb, c_lo, e=e):
            s0 = pl.multiple_of(sb * sblk, sblk)
            slot = lax.broadcasted_iota(jnp.int32, (sblk, LANES), 0) + s0
            c_lo = lax.while_loop(lambda c: (c < nchunk - 1) & (cs_ref[e, c + 1] <= s0), lambda c: c + 1, c_lo)
            c_hi = lax.while_loop(lambda c: (c < nchunk) & (cs_ref[e, c] < s0 + sblk), lambda c: c + 1, c_lo)

            def chunk(tcn, acc):
                acc_i, acc_w = acc
                t0 = pl.multiple_of(tcn * LANES, LANES)
                hit = pos_ref[e:e + 1, pl.ds(t0, LANES)] == slot
                acc_i = acc_i + jnp.where(hit, lane_t + t0, 0)
                acc_w = acc_w + jnp.where(hit, aff_ref[e:e + 1, pl.ds(t0, LANES)], 0.0)
                return acc_i, acc_w

            acc_i, acc_w = lax.fori_loop(
                c_lo, c_hi, chunk, (jnp.zeros((sblk, LANES), jnp.int32), jnp.zeros((sblk, LANES), F32)))
            col_i = jnp.sum(acc_i, axis=1, keepdims=True)
            col_w = jnp.sum(acc_w, axis=1, keepdims=True)
            rs = pl.ds(s0, sblk)
            idx_ref[rs, :] = jnp.where(lane_e == e, col_i, idx_ref[rs, :])
            w_ref[rs, :] = jnp.where(lane_e == e, col_w, w_ref[rs, :])
            return c_lo

        lax.fori_loop(0, cap // sblk, slot_block, 0)


def expert_select(aff_t):
    e, n = aff_t.shape
    cap = EC_FACTOR * n // N_EXPERTS
    full = pl.BlockSpec((e, n), lambda i: (0, 0))
    pos, rank = pl.pallas_call(
        functools.partial(_rank_kernel, n=n, cap=cap),
        grid=(1,),
        in_specs=[full],
        out_specs=[full, full],
        out_shape=[jax.ShapeDtypeStruct((e, n), jnp.int32), jax.ShapeDtypeStruct((e, n), F32)],
        scratch_shapes=[pltpu.VMEM((e, n), F32)],
        compiler_params=_params("arbitrary"),
        name="expert_rank",
    )(aff_t)
    rank = rank.astype(jnp.int32)
    total = jnp.full((e, 1), cap, jnp.int32)
    chunk_starts = jnp.concatenate([rank[:, ::LANES], total], axis=1)
    idx_t, w_t = pl.pallas_call(
        functools.partial(_invert_kernel, n=n, cap=cap),
        grid_spec=pltpu.PrefetchScalarGridSpec(
            num_scalar_prefetch=1,
            grid=(1,),
            in_specs=[pl.BlockSpec((e, n), lambda i, cs: (0, 0)), pl.BlockSpec((e, n), lambda i, cs: (0, 0))],
            out_specs=[pl.BlockSpec((cap, e), lambda i, cs: (0, 0)), pl.BlockSpec((cap, e), lambda i, cs: (0, 0))]),
        out_shape=[jax.ShapeDtypeStruct((cap, e), jnp.int32), jax.ShapeDtypeStruct((cap, e), F32)],
        compiler_params=_params("arbitrary"),
        name="expert_invert",
    )(chunk_starts, pos, aff_t)
    return idx_t.T, w_t.T, rank


FF_TILE = 256
FFN_ROWS = 512
Y_PAD = 64


TOK_SUB = D_MODEL // LANES


def _tok_copy(src_hbm, tok, xe_ref, slot, sem):
    return pltpu.make_async_copy(src_hbm.at[tok], xe_ref.at[:, slot, :], sem)


def _ffn_kernel(idx_ref, hp_hbm, hs_hbm, w1_ref, w3_ref, w2_ref, y_ref, xe_ref, xb_ref, sem,
                *, cap, gslots, per_step):
    e = pl.program_id(0)
    f = pl.program_id(1)
    ne = pl.num_programs(0)
    nf = pl.num_programs(1)
    rows = 2 * cap

    def start_pair(ex, s):
        _tok_copy(hp_hbm, idx_ref[ex * gslots + s], xe_ref, s, sem).start()
        _tok_copy(hs_hbm, idx_ref[(ne + ex) * gslots + s], xe_ref, gslots + s, sem).start()

    def wait_all():
        def body(s, c):
            _tok_copy(hp_hbm, 0, xe_ref, s, sem).wait()
            _tok_copy(hs_hbm, 0, xe_ref, gslots + s, sem).wait()
            return c

        lax.fori_loop(0, gslots, body, 0, unroll=8)

    @pl.when((e == 0) & (f == 0))
    def _():
        def body(s, c):
            start_pair(0, s)
            return c

        lax.fori_loop(0, gslots, body, 0, unroll=8)

    @pl.when(f == 0)
    def _():
        wait_all()
        for g in range(2):
            for j in range(TOK_SUB):
                xb_ref[g * cap:(g + 1) * cap, j * LANES:(j + 1) * LANES] = (
                    xe_ref[j, g * gslots:g * gslots + cap, :].astype(BF16))
        y_ref[0] = jnp.zeros(y_ref.shape[1:], F32)

    w1b = w1_ref[0].astype(BF16)
    w3b = w3_ref[0].astype(BF16)
    w2b = w2_ref[0].astype(BF16)
    nrc = rows // FFN_ROWS
    nxt = jnp.minimum(e + 1, ne - 1)
    for rc in range(nrc):
        rs = slice(rc * FFN_ROWS, (rc + 1) * FFN_ROWS)
        xb = xb_ref[rs, :]
        h1 = jnp.dot(xb, w1b, preferred_element_type=F32)
        h3 = jnp.dot(xb, w3b, preferred_element_type=F32)
        hid = (_silu(h1) * h3).astype(BF16)
        y_ref[0, rs, :] += jnp.dot(hid, w2b, preferred_element_type=F32)
        for k in range(rc * per_step // nrc, (rc + 1) * per_step // nrc):
            start_pair(nxt, f * per_step + k)

    @pl.when((e == ne - 1) & (f == nf - 1))
    def _():
        wait_all()


def expert_ffn(idx_flat, h2p, h2s, w1, w3, w2, *, cap, gslots):
    ne, d, ff = w1.shape
    rows = 2 * cap
    nf = ff // FF_TILE
    per_step = gslots // nf
    assert per_step * nf == gslots and gslots >= cap
    grid_spec = pltpu.PrefetchScalarGridSpec(
        num_scalar_prefetch=1,
        grid=(ne, nf),
        in_specs=[pl.BlockSpec(memory_space=pl.ANY),
                  pl.BlockSpec(memory_space=pl.ANY),
                  pl.BlockSpec((1, d, FF_TILE), lambda e, f, idx: (e, 0, f)),
                  pl.BlockSpec((1, d, FF_TILE), lambda e, f, idx: (e, 0, f)),
                  pl.BlockSpec((1, FF_TILE, d), lambda e, f, idx: (e, f, 0))],
        out_specs=pl.BlockSpec((1, rows + Y_PAD, d), lambda e, f, idx: (e, 0, 0)),
        scratch_shapes=[pltpu.VMEM((TOK_SUB, 2 * gslots, LANES), F32), pltpu.VMEM((rows, d), BF16),
                        pltpu.SemaphoreType.DMA(())],
    )
    return pl.pallas_call(
        functools.partial(_ffn_kernel, cap=cap, gslots=gslots, per_step=per_step),
        grid_spec=grid_spec,
        out_shape=jax.ShapeDtypeStruct((ne, rows + Y_PAD, d), F32),
        compiler_params=_params("arbitrary", "arbitrary"),
        name="expert_ffn",
    )(idx_flat, h2p, h2s, w1, w3, w2)


COMB_TILE = 512


COMB_GROUP = 8
STAGE_ROWS = COMB_TILE + Y_PAD


COMB_DEPTH = 8


def _combine_kernel(doff_ref, off_ref, wt_ref, x1_ref, mod_ref, fg_ref, y_hbm, o_ref, acc_ref, stage_ref, sem,
                    *, row0, cap):
    i = pl.program_id(0)
    nspan = pl.num_programs(0) * N_EXPERTS
    sub = TOK_SUB
    ahead = COMB_DEPTH - 1

    def span(p):
        e = lax.bitwise_and(p, N_EXPERTS - 1)
        ti = lax.shift_right_logical(p, N_EXPERTS.bit_length() - 1)
        a = off_ref[e, ti]
        cnt = off_ref[e, ti + 1] - a
        a8 = lax.shift_left(lax.shift_right_logical(a, 3), 3)
        lead = a - a8
        nchunk = lax.shift_right_logical(cnt + lead + (Y_PAD - 1), Y_PAD.bit_length() - 1)
        return e, a, cnt, a8, lead, nchunk

    def fetch(p, start):
        e, _, _, a8, _, nchunk = span(p)
        slot = lax.bitwise_and(p, COMB_DEPTH - 1)

        def body(ci, c):
            src0 = pl.multiple_of(row0 + a8 + ci * Y_PAD, 8)
            dst0 = pl.multiple_of(ci * Y_PAD, Y_PAD)
            for j in range(sub):
                cp = pltpu.make_async_copy(
                    y_hbm.at[e, pl.ds(src0, Y_PAD), pl.ds(j * LANES, LANES)],
                    stage_ref.at[pl.ds(pl.multiple_of(slot * STAGE_ROWS + dst0, Y_PAD), Y_PAD), j, :], sem.at[slot])
                cp.start() if start else cp.wait()
            return c

        lax.fori_loop(0, nchunk, body, 0)

    @pl.when(i == 0)
    def _():
        for p in range(ahead):
            fetch(jnp.int32(p), True)

    acc_ref[...] = jnp.zeros(acc_ref.shape, F32)

    def add_rows(tab0, row0_stage, count):
        dsts = [pl.ds(pl.multiple_of(doff_ref[tab0 + u], sub), sub) for u in range(count)]
        vals = [stage_ref[row0_stage + u] * wt_ref[tab0 + u] for u in range(count)]
        olds = [acc_ref[d, :] for d in dsts]
        for d, old, v in zip(dsts, olds, vals):
            acc_ref[d, :] = old + v

    def per_expert(ei, c):
        p = i * N_EXPERTS + ei

        @pl.when(p + ahead < nspan)
        def _():
            fetch(p + ahead, True)

        fetch(p, False)
        e, a, cnt, _, lead, _ = span(p)
        tab = e * cap + a
        srow = lax.bitwise_and(p, COMB_DEPTH - 1) * STAGE_ROWS + lead
        full = lax.shift_right_logical(cnt, 3)

        def group(g, cc):
            add_rows(tab + g * COMB_GROUP, srow + g * COMB_GROUP, COMB_GROUP)
            return cc

        lax.fori_loop(0, full, group, 0)

        def single(s, cc):
            add_rows(tab + s, srow + s, 1)
            return cc

        lax.fori_loop(full * COMB_GROUP, cnt, single, 0)
        return c

    lax.fori_loop(0, N_EXPERTS, per_expert, 0)
    g2 = mod_ref[0][:, 5 * D_MODEL:6 * D_MODEL]
    acc = jnp.concatenate([acc_ref[pl.ds(j, COMB_TILE, stride=sub), :] for j in range(sub)], axis=1)
    x = x1_ref[...] + g2 * acc
    o_ref[...] = _rms(x, D_MODEL) * fg_ref[...]


def combine(idx, offs, wts, x1, mods3, final_g, y_all, *, row0, mod_row):
    n, d = x1.shape
    assert COMB_GROUP == 8 and COMB_DEPTH & (COMB_DEPTH - 1) == 0
    doff = (idx % COMB_TILE) * TOK_SUB
    grid_spec = pltpu.PrefetchScalarGridSpec(
        num_scalar_prefetch=3,
        grid=(n // COMB_TILE,),
        in_specs=[pl.BlockSpec((COMB_TILE, d), lambda i, *_: (i, 0)),
                  pl.BlockSpec((1, 1, N_MOD * d), lambda i, *_: (mod_row(i), 0, 0)),
                  pl.BlockSpec((1, d), lambda i, *_: (0, 0)),
                  pl.BlockSpec(memory_space=pl.ANY)],
        out_specs=pl.BlockSpec((COMB_TILE, d), lambda i, *_: (i, 0)),
        scratch_shapes=[pltpu.VMEM((COMB_TILE * TOK_SUB, LANES), F32),
                        pltpu.VMEM((COMB_DEPTH * STAGE_ROWS, TOK_SUB, LANES), F32),
                        pltpu.SemaphoreType.DMA((COMB_DEPTH,))],
    )
    return pl.pallas_call(
        functools.partial(_combine_kernel, row0=row0, cap=idx.shape[1]),
        grid_spec=grid_spec,
        out_shape=jax.ShapeDtypeStruct((n, d), F32),
        compiler_params=_params("arbitrary"),
        name="combine",
    )(doff.reshape(-1), offs, wts.reshape(-1), x1, mods3, final_g.reshape(1, d), y_all)


def _tile_offsets(rank, cap):
    total = jnp.full((rank.shape[0], 1), cap, jnp.int32)
    return jnp.concatenate([rank[:, ::COMB_TILE], total], axis=1)


def kernel(x_prompt, x_sample, cache_k, cache_v, c, c_ctx, norm1_g, norm2_g, w_ada, b_ada, w_in, w_out,
           out_g_attn, out_g_hyena, rpb, conv_w, conv_b, filt_w1, filt_b1, filt_w2, filt_b2, filt_w3,
           filt_freq, hyena_bias, w_router, w1, w3, w2, final_g):
    depth = norm1_g.shape[0]
    assert depth == 1
    l = 0
    bp, tp, d = x_prompt.shape
    bs, ts, _ = x_sample.shape

    cond = jnp.zeros((MOD_ROWS, d), F32).at[0].set(c_ctx).at[1:1 + bs].set(c)
    mods3 = ada_mod(cond, w_ada[l], b_ada[l]).reshape(MOD_ROWS, 1, N_MOD * d)

    w_in_b = w_in[l].astype(BF16)
    w_out_b = w_out[l].astype(BF16)
    w_router_t = w_router[l].T
    rpb_ext = jnp.pad(rpb[l], ((0, 0), (0, 0), (RPB_PAD_L, RPB_PAD_R)), mode="edge")
    w1p = jnp.pad(filt_w1[l], ((0, LANES - FILTER_EMB), (0, 0)))

    ctx_row = lambda i: 0
    tt_s = 512
    lat_row = lambda i: 1 + i // (ts // tt_s)

    qkv_p, hy_p, state_k, state_v = in_projection(x_prompt, mods3, norm1_g[l], w_in_b, nb=2, tt=tp,
                                                  mod_row=ctx_row, with_state=True)
    qkv_s, hy_s = in_projection(x_sample, mods3, norm1_g[l], w_in_b, nb=1, tt=tt_s,
                                mod_row=lat_row, with_state=False)

    a_p = context_attention(qkv_p)
    a_s = latent_attention(qkv_s, cache_k[:, l:l + 1], cache_v[:, l:l + 1], rpb_ext)

    def hyena(hy, length, nb):
        fwd_tab, inv_tab = _dft_tables(length)
        fwd_tab = jnp.asarray(fwd_tab)
        ka, kb = hyena_filter(length, w1p, filt_b1[l], filt_w2[l], filt_b2[l], filt_w3[l], filt_freq[l],
                              fwd_tab)
        return hyena_mixer(hy, conv_w[l], conv_b[l], fwd_tab.astype(BF16), jnp.asarray(inv_tab).astype(BF16),
                           ka, kb, hyena_bias[l], nb=nb)

    yh_p = hyena(hy_p, tp, 8)
    yh_s = hyena(hy_s, ts, 2)

    x1_p, h2_p, aff_p = out_projection(a_p, yh_p, x_prompt, mods3, out_g_attn[l], out_g_hyena[l],
                                       norm2_g[l], w_out_b, w_router_t, nb=2, tt=tp, mod_row=ctx_row)
    x1_s, h2_s, aff_s = out_projection(a_s, yh_s, x_sample, mods3, out_g_attn[l], out_g_hyena[l],
                                       norm2_g[l], w_out_b, w_router_t, nb=1, tt=tt_s, mod_row=lat_row)

    idx_p, wt_p, rank_p = expert_select(aff_p)
    idx_s, wt_s, rank_s = expert_select(aff_s)
    cap = idx_p.shape[1]
    nf = w1.shape[3] // FF_TILE
    gslots = -(-cap // (8 * nf)) * 8 * nf
    pad_to = lambda a, width: jnp.pad(a, ((0, 0), (0, width - a.shape[1])))
    idx_flat = jnp.concatenate([pad_to(idx_p, gslots), pad_to(idx_s, gslots)], axis=0).reshape(-1)
    tiles = lambda h2: h2.reshape(-1, TOK_SUB, LANES)
    y_all = expert_ffn(idx_flat, tiles(h2_p), tiles(h2_s), w1[l], w3[l], w2[l], cap=cap, gslots=gslots)

    comb_row_s = lambda i: 1 + i // (ts // COMB_TILE)
    y_p = combine(idx_p, _tile_offsets(rank_p, cap), wt_p, x1_p, mods3, final_g, y_all, row0=0, mod_row=ctx_row)
    y_s = combine(idx_s, _tile_offsets(rank_s, cap), wt_s, x1_s, mods3, final_g, y_all, row0=cap,
                  mod_row=comb_row_s)

    return (y_p.reshape(bp, tp, d), y_s.reshape(bs, ts, d),
            state_k.reshape(bp, depth, N_HEADS, tp, HEAD_DIM), state_v.reshape(bp, depth, N_HEADS, tp, HEAD_DIM))
```

```python
import functools
import math

import numpy as np
import jax
import jax.numpy as jnp
from jax import lax
from jax.experimental import pallas as pl
from jax.experimental.pallas import tpu as pltpu

D_MODEL = 1024
GRID_W = 64
D_ATTN = 512
D_HYENA = 512
HEAD_DIM = 64
N_HEADS = 8
WIN_ROWS = 8
WIN_COLS = 16
FILTER_EMB = 33
FILTER_BANDS = 16
FILTER_HIDDEN = 64
DECAY_TARGET = 1e-2
SHORT_DECAY_PCT = 0.3
LONG_DECAY_PCT = 1.5
MOD_SHIFT = 0.05
N_EXPERTS = 16
EC_FACTOR = 2
EXPERT_FF = 2816
N_MOD = 6
EPS = 1e-6
NEG_INF = -1e30

LANES = 128
MOD_ROWS = 16
VMEM_LIMIT = 56 * 1024 * 1024
HI = lax.Precision.HIGHEST
BF16 = jnp.bfloat16
F32 = jnp.float32

_NT = (((1,), (1,)), ((), ()))


def _params(*sem):
    return pltpu.CompilerParams(dimension_semantics=sem, vmem_limit_bytes=VMEM_LIMIT)


def _rms(x, n):
    return x * lax.rsqrt(jnp.sum(x * x, axis=-1, keepdims=True) * (1.0 / n) + EPS)


def _silu(x):
    return x * (1.0 / (1.0 + jnp.exp(-x)))


def _ada_kernel(c_ref, w_ref, b_ref, o_ref):
    s = _silu(c_ref[...])
    o_ref[...] = jnp.dot(s, w_ref[...], precision=HI, preferred_element_type=F32) + b_ref[...]


def ada_mod(cond, w_ada, b_ada):
    n = w_ada.shape[1]
    tn = 1024
    return pl.pallas_call(
        _ada_kernel,
        grid=(n // tn,),
        in_specs=[pl.BlockSpec((MOD_ROWS, D_MODEL), lambda j: (0, 0)),
                  pl.BlockSpec((D_MODEL, tn), lambda j: (0, j)),
                  pl.BlockSpec((1, tn), lambda j: (0, j))],
        out_specs=pl.BlockSpec((MOD_ROWS, tn), lambda j: (0, j)),
        out_shape=jax.ShapeDtypeStruct((MOD_ROWS, n), F32),
        compiler_params=_params("arbitrary"),
        name="ada_mod",
    )(cond, w_ada, b_ada.reshape(1, n))


def _inproj_kernel(x_ref, mod_ref, g_ref, w_ref, qkv_ref, hy_ref, *state_refs, nb, tt):
    x = x_ref[...].reshape(nb * tt, D_MODEL)
    mod = mod_ref[0]
    sh1 = mod[:, 0:D_MODEL]
    sc1 = mod[:, D_MODEL:2 * D_MODEL]
    h = _rms(x, D_MODEL) * g_ref[...] * (1.0 + sc1) + sh1
    proj = jnp.dot(h.astype(BF16), w_ref[...], preferred_element_type=F32)
    q = proj[:, 0:D_ATTN] * (HEAD_DIM ** -0.5)
    k = proj[:, D_ATTN:2 * D_ATTN]
    v = proj[:, 2 * D_ATTN:3 * D_ATTN]
    qkv_ref[:, :, 0:D_ATTN] = q.reshape(nb, tt, D_ATTN)
    qkv_ref[:, :, D_ATTN:2 * D_ATTN] = k.reshape(nb, tt, D_ATTN)
    qkv_ref[:, :, 2 * D_ATTN:3 * D_ATTN] = v.reshape(nb, tt, D_ATTN)
    hy_ref[...] = proj[:, 3 * D_ATTN:].reshape(nb, tt, 3 * D_HYENA)
    if state_refs:
        sk_ref, sv_ref = state_refs
        for bi in range(nb):
            for hd in range(N_HEADS):
                sl = slice(hd * HEAD_DIM, (hd + 1) * HEAD_DIM)
                sk_ref[bi, hd] = k[bi * tt:(bi + 1) * tt, sl]
                sv_ref[bi, hd] = v[bi * tt:(bi + 1) * tt, sl]


def in_projection(x, mods3, norm_g, w_in_bf16, *, nb, tt, mod_row, with_state):
    b, t, d = x.shape
    steps_per_b = t // tt
    grid = (b // nb * steps_per_b,)
    xmap = lambda i: (i // steps_per_b, i % steps_per_b, 0)
    nproj = w_in_bf16.shape[1]
    out_shape = [jax.ShapeDtypeStruct((b, t, 3 * D_ATTN), F32),
                 jax.ShapeDtypeStruct((b, t, 3 * D_HYENA), F32)]
    out_specs = [pl.BlockSpec((nb, tt, 3 * D_ATTN), xmap),
                 pl.BlockSpec((nb, tt, 3 * D_HYENA), xmap)]
    if with_state:
        assert steps_per_b == 1
        smap = lambda i: (i, 0, 0, 0)
        for _ in range(2):
            out_shape.append(jax.ShapeDtypeStruct((b, N_HEADS, t, HEAD_DIM), F32))
            out_specs.append(pl.BlockSpec((nb, N_HEADS, tt, HEAD_DIM), smap))
    return pl.pallas_call(
        functools.partial(_inproj_kernel, nb=nb, tt=tt),
        grid=grid,
        in_specs=[pl.BlockSpec((nb, tt, d), xmap),
                  pl.BlockSpec((1, 1, N_MOD * d), lambda i: (mod_row(i), 0, 0)),
                  pl.BlockSpec((1, d), lambda i: (0, 0)),
                  pl.BlockSpec((d, nproj), lambda i: (0, 0))],
        out_specs=out_specs,
        out_shape=out_shape,
        compiler_params=_params("arbitrary"),
        name="in_projection",
    )(x, mods3, norm_g.reshape(1, d), w_in_bf16)


def _lane_mask(shape, half):
    lane = lax.broadcasted_iota(jnp.int32, shape, 1)
    return (lane < HEAD_DIM) if half == 0 else (lane >= HEAD_DIM)


def _ctx_attn_kernel(qkv_ref, o_ref):
    for j in range(N_HEADS // 2):
        lo = j * LANES
        qp = qkv_ref[0, :, lo:lo + LANES]
        kp = qkv_ref[0, :, D_ATTN + lo:D_ATTN + lo + LANES].astype(BF16)
        vp = qkv_ref[0, :, 2 * D_ATTN + lo:2 * D_ATTN + lo + LANES]
        out = None
        for half in range(2):
            m = _lane_mask(qp.shape, half)
            qm = jnp.where(m, qp, 0.0).astype(BF16)
            vm = jnp.where(m, vp, 0.0).astype(BF16)
            s = lax.dot_general(qm, kp, _NT, preferred_element_type=F32)
            p = jnp.exp(s - jnp.max(s, axis=-1, keepdims=True))
            l = jnp.sum(p, axis=-1, keepdims=True)
            o = jnp.dot(p.astype(BF16), vm, preferred_element_type=F32) * (1.0 / l)
            out = o if out is None else out + o
        o_ref[0, :, lo:lo + LANES] = out


def context_attention(qkv):
    b, t, _ = qkv.shape
    return pl.pallas_call(
        _ctx_attn_kernel,
        grid=(b,),
        in_specs=[pl.BlockSpec((1, t, 3 * D_ATTN), lambda i: (i, 0, 0))],
        out_specs=pl.BlockSpec((1, t, D_ATTN), lambda i: (i, 0, 0)),
        out_shape=jax.ShapeDtypeStruct((b, t, D_ATTN), F32),
        compiler_params=_params("arbitrary"),
        name="context_attention",
    )(qkv)


N_TPAIR = 2 * WIN_ROWS - 2
RPB_PAD_L = GRID_W - WIN_COLS
RPB_PAD_R = LANES - RPB_PAD_L - (2 * WIN_COLS - 1)


def _row_start(r, rows, wr):
    return min(max(r - wr // 2, 0), rows - wr)


def _build_bias(rpb_ref, tp_ref, m_ref, rows, wr):
    shape = (GRID_W, LANES)
    c = lax.broadcasted_iota(jnp.int32, shape, 0)
    lane = lax.broadcasted_iota(jnp.int32, shape, 1)
    kc = lane & (GRID_W - 1)
    ws = jnp.clip(c - WIN_COLS // 2, 0, GRID_W - WIN_COLS)
    col_ok = (kc >= ws) & (kc < ws + WIN_COLS)
    first = lane < GRID_W
    for hh in range(2):
        for dr in range(N_TPAIR):
            xa = jnp.broadcast_to(rpb_ref[hh, dr:dr + 1, :], shape)
            xb = jnp.broadcast_to(rpb_ref[hh, dr + 1:dr + 2, :], shape)
            ta = pltpu.roll(xa, GRID_W + 1, 1, stride=1, stride_axis=0)
            tb = pltpu.roll(xb, 1, 1, stride=1, stride_axis=0)
            tp_ref[hh, dr] = jnp.where(col_ok, jnp.where(first, ta, tb), NEG_INF)
        for r in range(rows):
            rs = _row_start(r, rows, wr)
            for jj in range(rows // 2):
                in_a = rs <= 2 * jj < rs + wr
                in_b = rs <= 2 * jj + 1 < rs + wr
                dr = 2 * jj - r + (WIN_ROWS - 1)
                if in_a and in_b:
                    blk = tp_ref[hh, dr]
                elif in_a:
                    blk = jnp.where(first, tp_ref[hh, dr], NEG_INF)
                elif in_b:
                    blk = jnp.where(first, NEG_INF, tp_ref[hh, dr])
                else:
                    blk = jnp.full(shape, NEG_INF, F32)
                m_ref[hh, r * GRID_W:(r + 1) * GRID_W, jj * LANES:(jj + 1) * LANES] = blk


def _key_range(qb, rows_per_blk, rows, wr):
    r0 = qb * rows_per_blk
    lo = min(_row_start(r, rows, wr) for r in range(r0, r0 + rows_per_blk))
    hi = max(_row_start(r, rows, wr) for r in range(r0, r0 + rows_per_blk)) + wr
    grp = 256 // GRID_W
    return (lo // grp) * 256, -(-hi // grp) * 256


def _lat_attn_kernel(q_ref, k_ref, v_ref, ck_ref, cv_ref, rpb_ref, o_ref, tp_ref, m_ref, *, rows, wr):
    @pl.when(pl.program_id(1) == 0)
    def _():
        _build_bias(rpb_ref, tp_ref, m_ref, rows, wr)

    l_tok = rows * GRID_W
    qblk = 256
    rows_per_blk = qblk // GRID_W
    sel_r = lax.broadcasted_iota(jnp.int32, (HEAD_DIM, LANES), 0)
    sel_c = lax.broadcasted_iota(jnp.int32, (HEAD_DIM, LANES), 1)

    def lane_pair(c_ref):
        parts = [jnp.dot(c_ref[0, 0, hh].astype(BF16), (sel_c == sel_r + hh * HEAD_DIM).astype(BF16),
                         preferred_element_type=F32) for hh in range(2)]
        return (parts[0] + parts[1]).astype(BF16)

    ck = lane_pair(ck_ref)
    cv = lane_pair(cv_ref)
    for qb in range(l_tok // qblk):
        c0, c1 = _key_range(qb, rows_per_blk, rows, wr)
        rs = slice(qb * qblk, (qb + 1) * qblk)
        qp = q_ref[0, rs, :]
        kp = k_ref[0, c0:c1, :].astype(BF16)
        vp = v_ref[0, c0:c1, :].astype(BF16)
        q2 = jnp.concatenate([jnp.where(_lane_mask(qp.shape, hh), qp, 0.0) for hh in range(2)], axis=0).astype(BF16)
        bias = jnp.concatenate([m_ref[hh, rs, c0:c1] for hh in range(2)], axis=0)
        s_win = lax.dot_general(q2, kp, _NT, preferred_element_type=F32) + bias
        s_ctx = lax.dot_general(q2, ck, _NT, preferred_element_type=F32)
        mx = jnp.maximum(jnp.max(s_win, axis=-1, keepdims=True), jnp.max(s_ctx, axis=-1, keepdims=True))
        p_win = jnp.exp(s_win - mx)
        p_ctx = jnp.exp(s_ctx - mx)
        l = jnp.sum(p_win, axis=-1, keepdims=True) + jnp.sum(p_ctx, axis=-1, keepdims=True)
        o2 = (jnp.dot(p_win.astype(BF16), vp, preferred_element_type=F32)
              + jnp.dot(p_ctx.astype(BF16), cv, preferred_element_type=F32)) * (1.0 / l)
        o_ref[0, rs, :] = jnp.where(_lane_mask(qp.shape, 0), o2[0:qblk], o2[qblk:2 * qblk])


def latent_attention(qkv, cache_k, cache_v, rpb_ext, layer):
    b, l_tok, _ = qkv.shape
    rows = l_tok // GRID_W
    wr = min(WIN_ROWS, rows)
    assert rows % 4 == 0 and wr == WIN_ROWS
    npair = N_HEADS // 2
    past = cache_k.shape[3]
    blk = lambda off: pl.BlockSpec((1, l_tok, LANES), lambda j, i: (i, 0, off + j))
    cspec = pl.BlockSpec((1, 1, 2, past, HEAD_DIM), lambda j, i: (i, layer, j, 0, 0))
    return pl.pallas_call(
        functools.partial(_lat_attn_kernel, rows=rows, wr=wr),
        grid=(npair, b),
        in_specs=[blk(0), blk(npair), blk(2 * npair), cspec, cspec,
                  pl.BlockSpec((2, 2 * WIN_ROWS - 1, LANES), lambda j, i: (j, 0, 0))],
        out_specs=pl.BlockSpec((1, l_tok, LANES), lambda j, i: (i, 0, j)),
        out_shape=jax.ShapeDtypeStruct((b, l_tok, D_ATTN), F32),
        scratch_shapes=[pltpu.VMEM((2, N_TPAIR, GRID_W, LANES), F32),
                        pltpu.VMEM((2, l_tok, l_tok), F32)],
        compiler_params=_params("arbitrary", "arbitrary"),
        name="latent_attention",
    )(qkv, qkv, qkv, cache_k, cache_v, rpb_ext)


HY_TC = 256


def _dft_tables(length):
    n2 = 2 * length
    k = np.arange(length, dtype=np.float64)[:, None]
    t = np.arange(length, dtype=np.float64)[None, :]
    ang = 2.0 * np.pi * ((k * t) % n2) / n2
    fa = np.cos(ang)
    fb = -np.sin(ang)
    fb[0, :] = np.cos(np.pi * t[0])
    fwd = np.concatenate([fa, fb], axis=0)
    ga = 2.0 * np.cos(ang).T
    ga[:, 0] = 1.0
    gb = -2.0 * np.sin(ang).T
    gb[:, 0] = np.cos(np.pi * t[0])
    inv = np.concatenate([ga, gb], axis=1)
    return fwd.astype(np.float32), inv.astype(np.float32)


def _filter_consts(length):
    t = np.linspace(0.0, 1.0, length, dtype=np.float32)[:, None]
    w = (np.float32(2.0 * math.pi / length) * np.arange(length, dtype=np.float32))[:, None]
    bands = np.linspace(1e-4, FILTER_BANDS - 1, FILTER_BANDS, dtype=np.float32)[None, :]
    z = np.concatenate([t, np.cos(bands * w), -np.sin(bands * w)], axis=-1).astype(np.float32)
    zp = np.zeros((length, LANES), np.float32)
    zp[:, :FILTER_EMB] = z
    deltas = np.abs(np.linspace(math.log(DECAY_TARGET) / LONG_DECAY_PCT,
                                math.log(DECAY_TARGET) / SHORT_DECAY_PCT, D_HYENA, dtype=np.float32))
    window = (np.exp(-t * deltas) + np.float32(MOD_SHIFT)).astype(np.float32)
    return zp, window


def _split3(a, b):
    a_hi = a.astype(BF16)
    a_lo = (a - a_hi.astype(F32)).astype(BF16)
    b_hi = b.astype(BF16)
    b_lo = (b - b_hi.astype(F32)).astype(BF16)
    return (jnp.dot(a_hi, b_hi, preferred_element_type=F32)
            + jnp.dot(a_hi, b_lo, preferred_element_type=F32)
            + jnp.dot(a_lo, b_hi, preferred_element_type=F32))


def _filter_kernel(z_ref, win_ref, w1_ref, b1_ref, w2_ref, b2_ref, w3_ref, fr_ref, fwd_ref,
                   ka_ref, kb_ref, s_ref, d_ref, *, length):
    j = pl.program_id(0)

    @pl.when(j == 0)
    def _():
        fr = fr_ref[...]
        hid = jnp.sin(fr * (jnp.dot(z_ref[...], w1_ref[...], precision=HI, preferred_element_type=F32)
                            + b1_ref[...]))
        hid = jnp.sin(fr * (jnp.dot(hid, w2_ref[...], precision=HI, preferred_element_type=F32)
                            + b2_ref[...]))
        filt = jnp.dot(hid, w3_ref[...], precision=HI, preferred_element_type=F32)
        nc = 2 * D_HYENA
        win = win_ref[...]
        win2 = jnp.concatenate([win, win], axis=1)
        fw = filt[:, 0:nc] * win2
        row = lax.broadcasted_iota(jnp.int32, (length, nc), 0)
        bw = jnp.where(row == 0, 0.0, filt[:, nc:2 * nc] * win2)
        inv = 1.0 / jnp.sum(jnp.abs(fw) + jnp.abs(bw), axis=0, keepdims=True)
        s_ref[...] = (fw + bw) * inv
        d_ref[...] = (fw - bw) * inv

    fa = fwd_ref[0]
    fb = fwd_ref[1]
    ka = _split3(fa, s_ref[...])
    kb = _split3(fb, d_ref[...])
    nyq = _split3(fb[0:8, :], s_ref[...])[0:1, :]
    row = lax.broadcasted_iota(jnp.int32, kb.shape, 0)
    kb = jnp.where((row == 0) & (j == 0), nyq, kb)
    for c in range(D_HYENA // HY_TC):
        lo, hi = c * HY_TC, (c + 1) * HY_TC
        ka_ref[c] = jnp.concatenate([ka[:, lo:hi], ka[:, D_HYENA + lo:D_HYENA + hi]], axis=1)
        kb_ref[c] = jnp.concatenate([kb[:, lo:hi], kb[:, D_HYENA + lo:D_HYENA + hi]], axis=1)


def hyena_filter(length, w1p, b1, w2, b2, w3, freq, fwd_tab):
    zp, window = _filter_consts(length)
    nc = 2 * D_HYENA
    nct = D_HYENA // HY_TC
    tm = 256
    fwd3 = fwd_tab.reshape(2, length, length)
    full = lambda a: pl.BlockSpec(a.shape, lambda j: (0,) * a.ndim)
    args = [jnp.asarray(zp), jnp.asarray(window), w1p, b1.reshape(1, -1), w2, b2.reshape(1, -1), w3,
            freq.reshape(1, -1)]
    return pl.pallas_call(
        functools.partial(_filter_kernel, length=length),
        grid=(length // tm,),
        in_specs=[full(a) for a in args] + [pl.BlockSpec((2, tm, length), lambda j: (0, j, 0))],
        out_specs=[pl.BlockSpec((nct, tm, 2 * HY_TC), lambda j: (0, j, 0))] * 2,
        out_shape=[jax.ShapeDtypeStruct((nct, length, 2 * HY_TC), F32)] * 2,
        scratch_shapes=[pltpu.VMEM((length, nc), F32), pltpu.VMEM((length, nc), F32)],
        compiler_params=_params("arbitrary"),
        name="hyena_filter",
    )(*args, fwd3)


def _dft_forward(u, f_ref):
    return jnp.dot(f_ref[...], u.astype(BF16), preferred_element_type=F32)


def _spectrum_product(x, ka, kb, length):
    xa = x[0:length]
    xb = x[length:2 * length]
    row = lax.broadcasted_iota(jnp.int32, xa.shape, 0)
    dc = row == 0
    ya = xa * ka - jnp.where(dc, 0.0, xb * kb)
    yb = jnp.where(dc, xb * kb, xa * kb + xb * ka)
    return ya.astype(BF16), yb.astype(BF16)


def _dft_inverse(ya, yb, g_ref, length):
    y = (jnp.dot(g_ref[:, 0:length], ya, preferred_element_type=F32)
         + jnp.dot(g_ref[:, length:2 * length], yb, preferred_element_type=F32))
    return y * (1.0 / (2 * length))


def _hyena_kernel(v_ref, x1_ref, x2_ref, cw_ref, cb_ref, f_ref, g_ref, ka_ref, kb_ref, hb_ref, o_ref,
                  *, length, tc, nb):
    row = lax.broadcasted_iota(jnp.int32, (length, tc), 0)

    def short(u, part):
        up = jnp.where(row == 0, 0.0, pltpu.roll(u, 1, 0))
        dn = jnp.where(row == length - 1, 0.0, pltpu.roll(u, length - 1, 0))
        w = cw_ref[part]
        return up * w[0:1] + u * w[1:2] + dn * w[2:3] + cb_ref[part]

    bs = range(nb)
    u = [short(v_ref[bi], 0) for bi in bs]
    for order in range(2):
        ka = ka_ref[:, order * tc:(order + 1) * tc]
        kb = kb_ref[:, order * tc:(order + 1) * tc]
        spec = [_dft_forward(u[bi], f_ref) for bi in bs]
        spec = [_spectrum_product(x, ka, kb, length) for x in spec]
        conv = [_dft_inverse(ya, yb, g_ref, length) + u[bi] * hb_ref[order] for bi, (ya, yb) in zip(bs, spec)]
        gate_ref = x1_ref if order == 0 else x2_ref
        u = [short(gate_ref[bi], order + 1) * conv[bi] for bi in bs]
    for bi in bs:
        o_ref[bi] = u[bi]


def hyena_mixer(hy, conv_w, conv_b, fwd_bf16, inv_bf16, ka, kb, hy_bias, *, nb):
    b, length, _ = hy.shape
    tc = HY_TC
    nct = D_HYENA // tc
    cw = conv_w.reshape(3, 3, nct, 1, tc).transpose(1, 2, 0, 3, 4).reshape(3, nct, 3, tc)
    cb = conv_b.reshape(3, nct, 1, tc)
    hb = hy_bias.reshape(2, nct, 1, tc).transpose(1, 0, 2, 3)
    part = lambda p: pl.BlockSpec((nb, length, tc), lambda c, i: (i, 0, p * nct + c))
    return pl.pallas_call(
        functools.partial(_hyena_kernel, length=length, tc=tc, nb=nb),
        grid=(nct, b // nb),
        in_specs=[part(0), part(1), part(2),
                  pl.BlockSpec((3, None, 3, tc), lambda c, i: (0, c, 0, 0)),
                  pl.BlockSpec((3, None, 1, tc), lambda c, i: (0, c, 0, 0)),
                  pl.BlockSpec((2 * length, length), lambda c, i: (0, 0)),
                  pl.BlockSpec((length, 2 * length), lambda c, i: (0, 0)),
                  pl.BlockSpec((None, length, 2 * tc), lambda c, i: (c, 0, 0)),
                  pl.BlockSpec((None, length, 2 * tc), lambda c, i: (c, 0, 0)),
                  pl.BlockSpec((None, 2, 1, tc), lambda c, i: (c, 0, 0, 0))],
        out_specs=pl.BlockSpec((nb, length, tc), lambda c, i: (i, 0, c)),
        out_shape=jax.ShapeDtypeStruct((b, length, D_HYENA), F32),
        compiler_params=_params("arbitrary", "arbitrary"),
        name="hyena_mixer",
    )(hy, hy, hy, cw, cb, fwd_bf16, inv_bf16, ka, kb, hb)


def _outproj_kernel(a_ref, y_ref, x_ref, mod_ref, ga_ref, gh_ref, g2_ref, wo_ref, wr_ref,
                    x1_ref, h2_ref, aff_ref, *, nb, tt):
    rows = nb * tt
    a = a_ref[...].reshape(rows, D_ATTN)
    yh = y_ref[...].reshape(rows, D_HYENA)
    x = x_ref[...].reshape(rows, D_MODEL)
    mod = mod_ref[0]
    g1 = mod[:, 2 * D_MODEL:3 * D_MODEL]
    sh2 = mod[:, 3 * D_MODEL:4 * D_MODEL]
    sc2 = mod[:, 4 * D_MODEL:5 * D_MODEL]
    an = (_rms(a, D_ATTN) * ga_ref[...]).astype(BF16)
    yn = (_rms(yh, D_HYENA) * gh_ref[...]).astype(BF16)
    mix = (jnp.dot(an, wo_ref[0:D_ATTN, :], preferred_element_type=F32)
           + jnp.dot(yn, wo_ref[D_ATTN:, :], preferred_element_type=F32))
    x1 = x + g1 * mix
    h2 = _rms(x1, D_MODEL) * g2_ref[...] * (1.0 + sc2) + sh2
    x1_ref[...] = x1
    sub = D_MODEL // LANES
    for j in range(sub):
        h2_ref[pl.ds(j, rows, stride=sub), :] = h2[:, j * LANES:(j + 1) * LANES]
    logits = lax.dot_general(wr_ref[...], h2, _NT, precision=HI, preferred_element_type=F32)
    p = jnp.exp(logits - jnp.max(logits, axis=0, keepdims=True))
    aff_ref[...] = p / jnp.sum(p, axis=0, keepdims=True)


def out_projection(a, yh, x, mods3, out_g_attn, out_g_hyena, norm2_g, w_out_bf16, w_router_t,
                   *, nb, tt, mod_row):
    b, t, d = x.shape
    steps_per_b = t // tt
    n = b * t
    rows = nb * tt
    grid = (n // rows,)
    xmap = lambda i: (i // steps_per_b, i % steps_per_b, 0)
    vec = lambda w: pl.BlockSpec((1, w), lambda i: (0, 0))
    return pl.pallas_call(
        functools.partial(_outproj_kernel, nb=nb, tt=tt),
        grid=grid,
        in_specs=[pl.BlockSpec((nb, tt, D_ATTN), xmap),
                  pl.BlockSpec((nb, tt, D_HYENA), xmap),
                  pl.BlockSpec((nb, tt, d), xmap),
                  pl.BlockSpec((1, 1, N_MOD * d), lambda i: (mod_row(i), 0, 0)),
                  vec(D_ATTN), vec(D_HYENA), vec(d),
                  pl.BlockSpec((d, d), lambda i: (0, 0)),
                  pl.BlockSpec((N_EXPERTS, d), lambda i: (0, 0))],
        out_specs=[pl.BlockSpec((rows, d), lambda i: (i, 0)),
                   pl.BlockSpec((rows * d // LANES, LANES), lambda i: (i, 0)),
                   pl.BlockSpec((N_EXPERTS, rows), lambda i: (0, i))],
        out_shape=[jax.ShapeDtypeStruct((n, d), F32),
                   jax.ShapeDtypeStruct((n * d // LANES, LANES), F32),
                   jax.ShapeDtypeStruct((N_EXPERTS, n), F32)],
        compiler_params=_params("arbitrary"),
        name="out_projection",
    )(a, yh, x, mods3, out_g_attn.reshape(1, -1), out_g_hyena.reshape(1, -1), norm2_g.reshape(1, -1),
      w_out_bf16, w_router_t)


SEL_CHUNK = 256


def _cumsum_excl(mask_ref, out_ref, n):
    i = lax.broadcasted_iota(jnp.int32, (SEL_CHUNK, SEL_CHUNK), 0)
    j = lax.broadcasted_iota(jnp.int32, (SEL_CHUNK, SEL_CHUNK), 1)
    tri = (i < j).astype(BF16)
    carry = jnp.zeros((N_EXPERTS, 1), F32)
    for ch in range(n // SEL_CHUNK):
        sl = slice(ch * SEL_CHUNK, (ch + 1) * SEL_CHUNK)
        m = mask_ref[:, sl]
        out_ref[:, sl] = jnp.dot(m.astype(BF16), tri, preferred_element_type=F32) + carry
        carry = carry + jnp.sum(m, axis=1, keepdims=True)
    return carry


def _rank_kernel(aff_ref, pos_ref, rank_ref, mask_ref, *, n, cap):
    as_float = lambda bits: lax.bitcast_convert_type(bits, F32)

    def bit_step(i, thr):
        cand = thr | (jnp.int32(1) << (30 - i))
        cnt = jnp.sum((aff_ref[...] >= as_float(cand)).astype(jnp.int32), axis=1, keepdims=True)
        return jnp.where(cnt >= cap, cand, thr)

    thr = lax.fori_loop(0, 31, bit_step, jnp.zeros((N_EXPERTS, 1), jnp.int32))
    aff = aff_ref[...]
    gt = aff >= as_float(thr + 1)
    eq = (aff >= as_float(thr)) & jnp.logical_not(gt)
    need = cap - jnp.sum(gt.astype(jnp.int32), axis=1, keepdims=True)
    mask_ref[...] = eq.astype(F32)
    _cumsum_excl(mask_ref, rank_ref, n)
    sel = gt | (eq & (rank_ref[...] < need.astype(F32)))
    mask_ref[...] = sel.astype(F32)
    _cumsum_excl(mask_ref, rank_ref, n)
    pos_ref[...] = jnp.where(sel, rank_ref[...].astype(jnp.int32), -1)


def _invert_kernel(cs_ref, pos_ref, aff_ref, idx_ref, w_ref, part_i_ref, part_w_ref, *, n, cap):
    sblk = LANES
    nchunk = n // LANES
    lane_e = lax.broadcasted_iota(jnp.int32, (cap, N_EXPERTS), 1)
    lane_t = lax.broadcasted_iota(jnp.int32, (sblk, LANES), 1)
    idx_ref[...] = jnp.zeros(idx_ref.shape, jnp.int32)
    w_ref[...] = jnp.zeros(w_ref.shape, F32)
    for e in range(N_EXPERTS):
        def slot_block(sb, c_lo, e=e):
            s0 = pl.multiple_of(sb * sblk, sblk)
            slot = lax.broadcasted_iota(jnp.int32, (sblk, LANES), 0) + s0
            c_lo = lax.while_loop(lambda c: (c < nchunk - 1) & (cs_ref[e, c + 1] <= s0), lambda c: c + 1, c_lo)
            c_hi = lax.while_loop(lambda c: (c < nchunk) & (cs_ref[e, c] < s0 + sblk), lambda c: c + 1, c_lo)

            def chunk(tcn, acc):
                acc_i, acc_w = acc
                t0 = pl.multiple_of(tcn * LANES, LANES)
                hit = pos_ref[e:e + 1, pl.ds(t0, LANES)] == slot
                acc_i = acc_i + jnp.where(hit, lane_t + t0, 0)
                acc_w = acc_w + jnp.where(hit, aff_ref[e:e + 1, pl.ds(t0, LANES)], 0.0)
                return acc_i, acc_w

            acc_i, acc_w = lax.fori_loop(
                c_lo, c_hi, chunk, (jnp.zeros((sblk, LANES), jnp.int32), jnp.zeros((sblk, LANES), F32)))
            part_i_ref[pl.ds(s0, sblk), :] = acc_i
            part_w_ref[pl.ds(s0, sblk), :] = acc_w
            return c_lo

        lax.fori_loop(0, cap // sblk, slot_block, 0)
        col_i = jnp.sum(part_i_ref[...], axis=1, keepdims=True)
        col_w = jnp.sum(part_w_ref[...], axis=1, keepdims=True)
        idx_ref[...] = jnp.where(lane_e == e, col_i, idx_ref[...])
        w_ref[...] = jnp.where(lane_e == e, col_w, w_ref[...])


def expert_select(aff_t):
    e, n = aff_t.shape
    cap = EC_FACTOR * n // N_EXPERTS
    full = pl.BlockSpec((e, n), lambda i: (0, 0))
    pos, rank = pl.pallas_call(
        functools.partial(_rank_kernel, n=n, cap=cap),
        grid=(1,),
        in_specs=[full],
        out_specs=[full, full],
        out_shape=[jax.ShapeDtypeStruct((e, n), jnp.int32), jax.ShapeDtypeStruct((e, n), F32)],
        scratch_shapes=[pltpu.VMEM((e, n), F32)],
        compiler_params=_params("arbitrary"),
        name="expert_rank",
    )(aff_t)
    rank = rank.astype(jnp.int32)
    total = jnp.full((e, 1), cap, jnp.int32)
    chunk_starts = jnp.concatenate([rank[:, ::LANES], total], axis=1)
    idx_t, w_t = pl.pallas_call(
        functools.partial(_invert_kernel, n=n, cap=cap),
        grid_spec=pltpu.PrefetchScalarGridSpec(
            num_scalar_prefetch=1,
            grid=(1,),
            in_specs=[pl.BlockSpec((e, n), lambda i, cs: (0, 0)), pl.BlockSpec((e, n), lambda i, cs: (0, 0))],
            out_specs=[pl.BlockSpec((cap, e), lambda i, cs: (0, 0)), pl.BlockSpec((cap, e), lambda i, cs: (0, 0))],
            scratch_shapes=[pltpu.VMEM((cap, LANES), jnp.int32), pltpu.VMEM((cap, LANES), F32)]),
        out_shape=[jax.ShapeDtypeStruct((cap, e), jnp.int32), jax.ShapeDtypeStruct((cap, e), F32)],
        compiler_params=_params("arbitrary"),
        name="expert_invert",
    )(chunk_starts, pos, aff_t)
    return idx_t.T, w_t.T, rank


FF_TILE = 256
FFN_ROWS = 512
Y_PAD = 64


TOK_SUB = D_MODEL // LANES


def _tok_copy(src_hbm, tok, xe_ref, slot, sem):
    return pltpu.make_async_copy(src_hbm.at[tok], xe_ref.at[:, slot, :], sem)


def _ffn_kernel(idx_ref, hp_hbm, hs_hbm, w1_ref, w3_ref, w2_ref, y_ref, xe_ref, xb_ref, sem,
                *, cap, gslots, per_step):
    e = pl.program_id(0)
    f = pl.program_id(1)
    ne = pl.num_programs(0)
    nf = pl.num_programs(1)
    rows = 2 * cap

    def start_pair(ex, s):
        _tok_copy(hp_hbm, idx_ref[ex * gslots + s], xe_ref, s, sem).start()
        _tok_copy(hs_hbm, idx_ref[(ne + ex) * gslots + s], xe_ref, gslots + s, sem).start()

    def wait_all():
        def body(s, c):
            _tok_copy(hp_hbm, 0, xe_ref, s, sem).wait()
            _tok_copy(hs_hbm, 0, xe_ref, gslots + s, sem).wait()
            return c

        lax.fori_loop(0, gslots, body, 0, unroll=8)

    @pl.when((e == 0) & (f == 0))
    def _():
        def body(s, c):
            start_pair(0, s)
            return c

        lax.fori_loop(0, gslots, body, 0, unroll=8)

    @pl.when(f == 0)
    def _():
        wait_all()
        for g in range(2):
            for j in range(TOK_SUB):
                xb_ref[g * cap:(g + 1) * cap, j * LANES:(j + 1) * LANES] = (
                    xe_ref[j, g * gslots:g * gslots + cap, :].astype(BF16))
        y_ref[0] = jnp.zeros(y_ref.shape[1:], F32)

    w1b = w1_ref[0].astype(BF16)
    w3b = w3_ref[0].astype(BF16)
    w2b = w2_ref[0].astype(BF16)
    nrc = rows // FFN_ROWS
    nxt = jnp.minimum(e + 1, ne - 1)
    for rc in range(nrc):
        rs = slice(rc * FFN_ROWS, (rc + 1) * FFN_ROWS)
        xb = xb_ref[rs, :]
        h1 = jnp.dot(xb, w1b, preferred_element_type=F32)
        h3 = jnp.dot(xb, w3b, preferred_element_type=F32)
        hid = (_silu(h1) * h3).astype(BF16)
        y_ref[0, rs, :] += jnp.dot(hid, w2b, preferred_element_type=F32)
        for k in range(rc * per_step // nrc, (rc + 1) * per_step // nrc):
            start_pair(nxt, f * per_step + k)

    @pl.when((e == ne - 1) & (f == nf - 1))
    def _():
        wait_all()


def expert_ffn(idx_flat, h2p, h2s, w1, w3, w2, *, cap, gslots):
    ne, d, ff = w1.shape
    rows = 2 * cap
    nf = ff // FF_TILE
    per_step = gslots // nf
    assert per_step * nf == gslots and gslots >= cap
    grid_spec = pltpu.PrefetchScalarGridSpec(
        num_scalar_prefetch=1,
        grid=(ne, nf),
        in_specs=[pl.BlockSpec(memory_space=pl.ANY),
                  pl.BlockSpec(memory_space=pl.ANY),
                  pl.BlockSpec((1, d, FF_TILE), lambda e, f, idx: (e, 0, f)),
                  pl.BlockSpec((1, d, FF_TILE), lambda e, f, idx: (e, 0, f)),
                  pl.BlockSpec((1, FF_TILE, d), lambda e, f, idx: (e, f, 0))],
        out_specs=pl.BlockSpec((1, rows + Y_PAD, d), lambda e, f, idx: (e, 0, 0)),
        scratch_shapes=[pltpu.VMEM((TOK_SUB, 2 * gslots, LANES), F32), pltpu.VMEM((rows, d), BF16),
                        pltpu.SemaphoreType.DMA(())],
    )
    return pl.pallas_call(
        functools.partial(_ffn_kernel, cap=cap, gslots=gslots, per_step=per_step),
        grid_spec=grid_spec,
        out_shape=jax.ShapeDtypeStruct((ne, rows + Y_PAD, d), F32),
        compiler_params=_params("arbitrary", "arbitrary"),
        name="expert_ffn",
    )(idx_flat, h2p, h2s, w1, w3, w2)


COMB_TILE = 512


COMB_GROUP = 8
STAGE_ROWS = COMB_TILE + Y_PAD


COMB_DEPTH = 8


def _combine_kernel(doff_ref, off_ref, wt_ref, x1_ref, mod_ref, fg_ref, y_hbm, o_ref, acc_ref, stage_ref, sem,
                    *, row0, cap):
    i = pl.program_id(0)
    nspan = pl.num_programs(0) * N_EXPERTS
    sub = TOK_SUB
    ahead = COMB_DEPTH - 1

    def span(p):
        e = lax.bitwise_and(p, N_EXPERTS - 1)
        ti = lax.shift_right_logical(p, N_EXPERTS.bit_length() - 1)
        a = off_ref[e, ti]
        cnt = off_ref[e, ti + 1] - a
        a8 = lax.shift_left(lax.shift_right_logical(a, 3), 3)
        lead = a - a8
        nchunk = lax.shift_right_logical(cnt + lead + (Y_PAD - 1), Y_PAD.bit_length() - 1)
        return e, a, cnt, a8, lead, nchunk

    def fetch(p, start):
        e, _, _, a8, _, nchunk = span(p)
        slot = lax.bitwise_and(p, COMB_DEPTH - 1)

        def body(ci, c):
            src0 = pl.multiple_of(row0 + a8 + ci * Y_PAD, 8)
            dst0 = pl.multiple_of(ci * Y_PAD, Y_PAD)
            for j in range(sub):
                cp = pltpu.make_async_copy(
                    y_hbm.at[e, pl.ds(src0, Y_PAD), pl.ds(j * LANES, LANES)],
                    stage_ref.at[pl.ds(pl.multiple_of(slot * STAGE_ROWS + dst0, Y_PAD), Y_PAD), j, :], sem.at[slot])
                cp.start() if start else cp.wait()
            return c

        lax.fori_loop(0, nchunk, body, 0)

    @pl.when(i == 0)
    def _():
        for p in range(ahead):
            fetch(jnp.int32(p), True)

    acc_ref[...] = jnp.zeros(acc_ref.shape, F32)

    def add_rows(tab0, row0_stage, count):
        dsts = [pl.ds(pl.multiple_of(doff_ref[tab0 + u], sub), sub) for u in range(count)]
        vals = [stage_ref[row0_stage + u] * wt_ref[tab0 + u] for u in range(count)]
        olds = [acc_ref[d, :] for d in dsts]
        for d, old, v in zip(dsts, olds, vals):
            acc_ref[d, :] = old + v

    def per_expert(ei, c):
        p = i * N_EXPERTS + ei

        @pl.when(p + ahead < nspan)
        def _():
            fetch(p + ahead, True)

        fetch(p, False)
        e, a, cnt, _, lead, _ = span(p)
        tab = e * cap + a
        srow = lax.bitwise_and(p, COMB_DEPTH - 1) * STAGE_ROWS + lead
        full = lax.shift_right_logical(cnt, 3)

        def group(g, cc):
            add_rows(tab + g * COMB_GROUP, srow + g * COMB_GROUP, COMB_GROUP)
            return cc

        lax.fori_loop(0, full, group, 0)

        def single(s, cc):
            add_rows(tab + s, srow + s, 1)
            return cc

        lax.fori_loop(full * COMB_GROUP, cnt, single, 0)
        return c

    lax.fori_loop(0, N_EXPERTS, per_expert, 0)
    g2 = mod_ref[0][:, 5 * D_MODEL:6 * D_MODEL]
    acc = jnp.concatenate([acc_ref[pl.ds(j, COMB_TILE, stride=sub), :] for j in range(sub)], axis=1)
    x = x1_ref[...] + g2 * acc
    o_ref[...] = _rms(x, D_MODEL) * fg_ref[...]


def combine(idx, offs, wts, x1, mods3, final_g, y_all, *, row0, mod_row):
    n, d = x1.shape
    assert COMB_GROUP == 8 and COMB_DEPTH & (COMB_DEPTH - 1) == 0
    doff = (idx % COMB_TILE) * TOK_SUB
    grid_spec = pltpu.PrefetchScalarGridSpec(
        num_scalar_prefetch=3,
        grid=(n // COMB_TILE,),
        in_specs=[pl.BlockSpec((COMB_TILE, d), lambda i, *_: (i, 0)),
                  pl.BlockSpec((1, 1, N_MOD * d), lambda i, *_: (mod_row(i), 0, 0)),
                  pl.BlockSpec((1, d), lambda i, *_: (0, 0)),
                  pl.BlockSpec(memory_space=pl.ANY)],
        out_specs=pl.BlockSpec((COMB_TILE, d), lambda i, *_: (i, 0)),
        scratch_shapes=[pltpu.VMEM((COMB_TILE * TOK_SUB, LANES), F32),
                        pltpu.VMEM((COMB_DEPTH * STAGE_ROWS, TOK_SUB, LANES), F32),
                        pltpu.SemaphoreType.DMA((COMB_DEPTH,))],
    )
    return pl.pallas_call(
        functools.partial(_combine_kernel, row0=row0, cap=idx.shape[1]),
        grid_spec=grid_spec,
        out_shape=jax.ShapeDtypeStruct((n, d), F32),
        compiler_params=_params("arbitrary"),
        name="combine",
    )(doff.reshape(-1), offs, wts.reshape(-1), x1, mods3, final_g.reshape(1, d), y_all)


def _tile_offsets(rank, cap):
    total = jnp.full((rank.shape[0], 1), cap, jnp.int32)
    return jnp.concatenate([rank[:, ::COMB_TILE], total], axis=1)


def kernel(x_prompt, x_sample, cache_k, cache_v, c, c_ctx, norm1_g, norm2_g, w_ada, b_ada, w_in, w_out,
           out_g_attn, out_g_hyena, rpb, conv_w, conv_b, filt_w1, filt_b1, filt_w2, filt_b2, filt_w3,
           filt_freq, hyena_bias, w_router, w1, w3, w2, final_g):
    depth = norm1_g.shape[0]
    assert depth == 1
    l = 0
    bp, tp, d = x_prompt.shape
    bs, ts, _ = x_sample.shape

    cond = jnp.zeros((MOD_ROWS, d), F32).at[0].set(c_ctx).at[1:1 + bs].set(c)
    mods3 = ada_mod(cond, w_ada[l], b_ada[l]).reshape(MOD_ROWS, 1, N_MOD * d)

    w_in_b = w_in[l].astype(BF16)
    w_out_b = w_out[l].astype(BF16)
    w_router_t = w_router[l].T
    rpb_ext = jnp.pad(rpb[l], ((0, 0), (0, 0), (RPB_PAD_L, RPB_PAD_R)), mode="edge")
    w1p = jnp.pad(filt_w1[l], ((0, LANES - FILTER_EMB), (0, 0)))

    ctx_row = lambda i: 0
    tt_s = 512
    lat_row = lambda i: 1 + i // (ts // tt_s)

    qkv_p, hy_p, state_k, state_v = in_projection(x_prompt, mods3, norm1_g[l], w_in_b, nb=2, tt=tp,
                                                  mod_row=ctx_row, with_state=True)
    qkv_s, hy_s = in_projection(x_sample, mods3, norm1_g[l], w_in_b, nb=1, tt=tt_s,
                                mod_row=lat_row, with_state=False)

    a_p = context_attention(qkv_p)
    a_s = latent_attention(qkv_s, cache_k, cache_v, rpb_ext, l)

    def hyena(hy, length, nb):
        fwd_tab, inv_tab = _dft_tables(length)
        fwd_tab = jnp.asarray(fwd_tab)
        ka, kb = hyena_filter(length, w1p, filt_b1[l], filt_w2[l], filt_b2[l], filt_w3[l], filt_freq[l],
                              fwd_tab)
        return hyena_mixer(hy, conv_w[l], conv_b[l], fwd_tab.astype(BF16), jnp.asarray(inv_tab).astype(BF16),
                           ka, kb, hyena_bias[l], nb=nb)

    yh_p = hyena(hy_p, tp, 8)
    yh_s = hyena(hy_s, ts, 2)

    x1_p, h2_p, aff_p = out_projection(a_p, yh_p, x_prompt, mods3, out_g_attn[l], out_g_hyena[l],
                                       norm2_g[l], w_out_b, w_router_t, nb=2, tt=tp, mod_row=ctx_row)
    x1_s, h2_s, aff_s = out_projection(a_s, yh_s, x_sample, mods3, out_g_attn[l], out_g_hyena[l],
                                       norm2_g[l], w_out_b, w_router_t, nb=1, tt=tt_s, mod_row=lat_row)

    idx_p, wt_p, rank_p = expert_select(aff_p)
    idx_s, wt_s, rank_s = expert_select(aff_s)
    cap = idx_p.shape[1]
    nf = w1.shape[3] // FF_TILE
    gslots = -(-cap // (8 * nf)) * 8 * nf
    pad_to = lambda a, width: jnp.pad(a, ((0, 0), (0, width - a.shape[1])))
    idx_flat = jnp.concatenate([pad_to(idx_p, gslots), pad_to(idx_s, gslots)], axis=0).reshape(-1)
    tiles = lambda h2: h2.reshape(-1, TOK_SUB, LANES)
    y_all = expert_ffn(idx_flat, tiles(h2_p), tiles(h2_s), w1[l], w3[l], w2[l], cap=cap, gslots=gslots)

    comb_row_s = lambda i: 1 + i // (ts // COMB_TILE)
    y_p = combine(idx_p, _tile_offsets(rank_p, cap), wt_p, x1_p, mods3, final_g, y_all, row0=0, mod_row=ctx_row)
    y_s = combine(idx_s, _tile_offsets(rank_s, cap), wt_s, x1_s, mods3, final_g, y_all, row0=cap,
                  mod_row=comb_row_s)

    return (y_p.reshape(bp, tp, d), y_s.reshape(bs, ts, d),
            state_k.reshape(bp, depth, N_HEADS, tp, HEAD_DIM), state_v.reshape(bp, depth, N_HEADS, tp, HEAD_DIM))
```

```python
import functools
import math

import numpy as np
import jax
import jax.numpy as jnp
from jax import lax
from jax.experimental import pallas as pl
from jax.experimental.pallas import tpu as pltpu

D_MODEL = 1024
GRID_W = 64
D_ATTN = 512
D_HYENA = 512
HEAD_DIM = 64
N_HEADS = 8
WIN_ROWS = 8
WIN_COLS = 16
FILTER_EMB = 33
FILTER_BANDS = 16
FILTER_HIDDEN = 64
DECAY_TARGET = 1e-2
SHORT_DECAY_PCT = 0.3
LONG_DECAY_PCT = 1.5
MOD_SHIFT = 0.05
N_EXPERTS = 16
EC_FACTOR = 2
EXPERT_FF = 2816
N_MOD = 6
EPS = 1e-6
NEG_INF = -1e30

LANES = 128
MOD_ROWS = 16
VMEM_LIMIT = 56 * 1024 * 1024
HI = lax.Precision.HIGHEST
BF16 = jnp.bfloat16
F32 = jnp.float32

_NT = (((1,), (1,)), ((), ()))


def _params(*sem):
    return pltpu.CompilerParams(dimension_semantics=sem, vmem_limit_bytes=VMEM_LIMIT)


def _rms(x, n):
    return x * lax.rsqrt(jnp.sum(x * x, axis=-1, keepdims=True) * (1.0 / n) + EPS)


def _silu(x):
    return x * (1.0 / (1.0 + jnp.exp(-x)))


def _ada_kernel(c_ref, w_ref, b_ref, o_ref):
    s = _silu(c_ref[...])
    o_ref[...] = jnp.dot(s, w_ref[...], precision=HI, preferred_element_type=F32) + b_ref[...]


def ada_mod(cond, w_ada, b_ada):
    n = w_ada.shape[1]
    tn = 1024
    return pl.pallas_call(
        _ada_kernel,
        grid=(n // tn,),
        in_specs=[pl.BlockSpec((MOD_ROWS, D_MODEL), lambda j: (0, 0)),
                  pl.BlockSpec((D_MODEL, tn), lambda j: (0, j)),
                  pl.BlockSpec((1, tn), lambda j: (0, j))],
        out_specs=pl.BlockSpec((MOD_ROWS, tn), lambda j: (0, j)),
        out_shape=jax.ShapeDtypeStruct((MOD_ROWS, n), F32),
        compiler_params=_params("arbitrary"),
        name="ada_mod",
    )(cond, w_ada, b_ada.reshape(1, n))


def _inproj_kernel(x_ref, mod_ref, g_ref, w_ref, qkv_ref, hy_ref, *state_refs, nb, tt):
    x = x_ref[...].reshape(nb * tt, D_MODEL)
    mod = mod_ref[0]
    sh1 = mod[:, 0:D_MODEL]
    sc1 = mod[:, D_MODEL:2 * D_MODEL]
    h = _rms(x, D_MODEL) * g_ref[...] * (1.0 + sc1) + sh1
    proj = jnp.dot(h.astype(BF16), w_ref[...], preferred_element_type=F32)
    q = proj[:, 0:D_ATTN] * (HEAD_DIM ** -0.5)
    k = proj[:, D_ATTN:2 * D_ATTN]
    v = proj[:, 2 * D_ATTN:3 * D_ATTN]
    qkv_ref[:, :, 0:D_ATTN] = q.reshape(nb, tt, D_ATTN)
    qkv_ref[:, :, D_ATTN:2 * D_ATTN] = k.reshape(nb, tt, D_ATTN)
    qkv_ref[:, :, 2 * D_ATTN:3 * D_ATTN] = v.reshape(nb, tt, D_ATTN)
    hy_ref[...] = proj[:, 3 * D_ATTN:].reshape(nb, tt, 3 * D_HYENA)
    if state_refs:
        sk_ref, sv_ref = state_refs
        for bi in range(nb):
            for hd in range(N_HEADS):
                sl = slice(hd * HEAD_DIM, (hd + 1) * HEAD_DIM)
                sk_ref[bi, hd] = k[bi * tt:(bi + 1) * tt, sl]
                sv_ref[bi, hd] = v[bi * tt:(bi + 1) * tt, sl]


def in_projection(x, mods3, norm_g, w_in_bf16, *, nb, tt, mod_row, with_state):
    b, t, d = x.shape
    steps_per_b = t // tt
    grid = (b // nb * steps_per_b,)
    xmap = lambda i: (i // steps_per_b, i % steps_per_b, 0)
    nproj = w_in_bf16.shape[1]
    out_shape = [jax.ShapeDtypeStruct((b, t, 3 * D_ATTN), F32),
                 jax.ShapeDtypeStruct((b, t, 3 * D_HYENA), F32)]
    out_specs = [pl.BlockSpec((nb, tt, 3 * D_ATTN), xmap),
                 pl.BlockSpec((nb, tt, 3 * D_HYENA), xmap)]
    if with_state:
        assert steps_per_b == 1
        smap = lambda i: (i, 0, 0, 0)
        for _ in range(2):
            out_shape.append(jax.ShapeDtypeStruct((b, N_HEADS, t, HEAD_DIM), F32))
            out_specs.append(pl.BlockSpec((nb, N_HEADS, tt, HEAD_DIM), smap))
    return pl.pallas_call(
        functools.partial(_inproj_kernel, nb=nb, tt=tt),
        grid=grid,
        in_specs=[pl.BlockSpec((nb, tt, d), xmap),
                  pl.BlockSpec((1, 1, N_MOD * d), lambda i: (mod_row(i), 0, 0)),
                  pl.BlockSpec((1, d), lambda i: (0, 0)),
                  pl.BlockSpec((d, nproj), lambda i: (0, 0))],
        out_specs=out_specs,
        out_shape=out_shape,
        compiler_params=_params("arbitrary"),
        name="in_projection",
    )(x, mods3, norm_g.reshape(1, d), w_in_bf16)


def _lane_mask(shape, half):
    lane = lax.broadcasted_iota(jnp.int32, shape, 1)
    return (lane < HEAD_DIM) if half == 0 else (lane >= HEAD_DIM)


def _ctx_attn_kernel(qkv_ref, o_ref, *, nb):
    t = qkv_ref.shape[1]
    for bi in range(nb):
        for j in range(N_HEADS // 2):
            lo = j * LANES
            qp = qkv_ref[bi, :, lo:lo + LANES]
            kp = qkv_ref[bi, :, D_ATTN + lo:D_ATTN + lo + LANES].astype(BF16)
            vp = qkv_ref[bi, :, 2 * D_ATTN + lo:2 * D_ATTN + lo + LANES].astype(BF16)
            q2 = jnp.concatenate([jnp.where(_lane_mask(qp.shape, hh), qp, 0.0) for hh in range(2)],
                                 axis=0).astype(BF16)
            s = lax.dot_general(q2, kp, _NT, preferred_element_type=F32)
            p = jnp.exp(s - jnp.max(s, axis=-1, keepdims=True))
            l = jnp.sum(p, axis=-1, keepdims=True)
            o2 = jnp.dot(p.astype(BF16), vp, preferred_element_type=F32) * (1.0 / l)
            o_ref[bi, :, lo:lo + LANES] = jnp.where(_lane_mask(qp.shape, 0), o2[0:t], o2[t:2 * t])


def context_attention(qkv, *, nb):
    b, t, _ = qkv.shape
    return pl.pallas_call(
        functools.partial(_ctx_attn_kernel, nb=nb),
        grid=(b // nb,),
        in_specs=[pl.BlockSpec((nb, t, 3 * D_ATTN), lambda i: (i, 0, 0))],
        out_specs=pl.BlockSpec((nb, t, D_ATTN), lambda i: (i, 0, 0)),
        out_shape=jax.ShapeDtypeStruct((b, t, D_ATTN), F32),
        compiler_params=_params("arbitrary"),
        name="context_attention",
    )(qkv)


N_TPAIR = 2 * WIN_ROWS - 2
RPB_PAD_L = GRID_W - WIN_COLS
RPB_PAD_R = LANES - RPB_PAD_L - (2 * WIN_COLS - 1)


def _row_start(r, rows, wr):
    return min(max(r - wr // 2, 0), rows - wr)


def _build_bias(rpb_ref, tp_ref, m_ref, rows, wr):
    shape = (GRID_W, LANES)
    c = lax.broadcasted_iota(jnp.int32, shape, 0)
    lane = lax.broadcasted_iota(jnp.int32, shape, 1)
    kc = lane & (GRID_W - 1)
    ws = jnp.clip(c - WIN_COLS // 2, 0, GRID_W - WIN_COLS)
    col_ok = (kc >= ws) & (kc < ws + WIN_COLS)
    first = lane < GRID_W
    for hh in range(2):
        for dr in range(N_TPAIR):
            xa = jnp.broadcast_to(rpb_ref[hh, dr:dr + 1, :], shape)
            xb = jnp.broadcast_to(rpb_ref[hh, dr + 1:dr + 2, :], shape)
            ta = pltpu.roll(xa, GRID_W + 1, 1, stride=1, stride_axis=0)
            tb = pltpu.roll(xb, 1, 1, stride=1, stride_axis=0)
            tp_ref[hh, dr] = jnp.where(col_ok, jnp.where(first, ta, tb), NEG_INF)
        for r in range(rows):
            rs = _row_start(r, rows, wr)
            for jj in range(rows // 2):
                in_a = rs <= 2 * jj < rs + wr
                in_b = rs <= 2 * jj + 1 < rs + wr
                dr = 2 * jj - r + (WIN_ROWS - 1)
                if in_a and in_b:
                    blk = tp_ref[hh, dr]
                elif in_a:
                    blk = jnp.where(first, tp_ref[hh, dr], NEG_INF)
                elif in_b:
                    blk = jnp.where(first, NEG_INF, tp_ref[hh, dr])
                else:
                    blk = jnp.full(shape, NEG_INF, F32)
                m_ref[hh, r * GRID_W:(r + 1) * GRID_W, jj * LANES:(jj + 1) * LANES] = blk


def _key_range(qb, rows_per_blk, rows, wr):
    r0 = qb * rows_per_blk
    lo = min(_row_start(r, rows, wr) for r in range(r0, r0 + rows_per_blk))
    hi = max(_row_start(r, rows, wr) for r in range(r0, r0 + rows_per_blk)) + wr
    grp = 256 // GRID_W
    return (lo // grp) * 256, -(-hi // grp) * 256


def _lat_attn_kernel(q_ref, k_ref, v_ref, ck_ref, cv_ref, rpb_ref, o_ref, tp_ref, m_ref, *, rows, wr):
    @pl.when(pl.program_id(1) == 0)
    def _():
        _build_bias(rpb_ref, tp_ref, m_ref, rows, wr)

    l_tok = rows * GRID_W
    qblk = 256
    rows_per_blk = qblk // GRID_W
    sel_r = lax.broadcasted_iota(jnp.int32, (HEAD_DIM, LANES), 0)
    sel_c = lax.broadcasted_iota(jnp.int32, (HEAD_DIM, LANES), 1)

    def lane_pair(c_ref):
        parts = [jnp.dot(c_ref[0, 0, hh].astype(BF16), (sel_c == sel_r + hh * HEAD_DIM).astype(BF16),
                         preferred_element_type=F32) for hh in range(2)]
        return (parts[0] + parts[1]).astype(BF16)

    ck = lane_pair(ck_ref)
    cv = lane_pair(cv_ref)
    for qb in range(l_tok // qblk):
        c0, c1 = _key_range(qb, rows_per_blk, rows, wr)
        rs = slice(qb * qblk, (qb + 1) * qblk)
        qp = q_ref[0, rs, :]
        kp = k_ref[0, c0:c1, :].astype(BF16)
        vp = v_ref[0, c0:c1, :].astype(BF16)
        q2 = jnp.concatenate([jnp.where(_lane_mask(qp.shape, hh), qp, 0.0) for hh in range(2)], axis=0).astype(BF16)
        bias = jnp.concatenate([m_ref[hh, rs, c0:c1] for hh in range(2)], axis=0)
        s_win = lax.dot_general(q2, kp, _NT, preferred_element_type=F32) + bias
        s_ctx = lax.dot_general(q2, ck, _NT, preferred_element_type=F32)
        mx = jnp.maximum(jnp.max(s_win, axis=-1, keepdims=True), jnp.max(s_ctx, axis=-1, keepdims=True))
        p_win = jnp.exp(s_win - mx)
        p_ctx = jnp.exp(s_ctx - mx)
        l = jnp.sum(p_win, axis=-1, keepdims=True) + jnp.sum(p_ctx, axis=-1, keepdims=True)
        o2 = (jnp.dot(p_win.astype(BF16), vp, preferred_element_type=F32)
              + jnp.dot(p_ctx.astype(BF16), cv, preferred_element_type=F32)) * (1.0 / l)
        o_ref[0, rs, :] = jnp.where(_lane_mask(qp.shape, 0), o2[0:qblk], o2[qblk:2 * qblk])


def latent_attention(qkv, cache_k, cache_v, rpb_ext, layer):
    b, l_tok, _ = qkv.shape
    rows = l_tok // GRID_W
    wr = min(WIN_ROWS, rows)
    assert rows % 4 == 0 and wr == WIN_ROWS
    npair = N_HEADS // 2
    past = cache_k.shape[3]
    blk = lambda off: pl.BlockSpec((1, l_tok, LANES), lambda j, i: (i, 0, off + j))
    cspec = pl.BlockSpec((1, 1, 2, past, HEAD_DIM), lambda j, i: (i, layer, j, 0, 0))
    return pl.pallas_call(
        functools.partial(_lat_attn_kernel, rows=rows, wr=wr),
        grid=(npair, b),
        in_specs=[blk(0), blk(npair), blk(2 * npair), cspec, cspec,
                  pl.BlockSpec((2, 2 * WIN_ROWS - 1, LANES), lambda j, i: (j, 0, 0))],
        out_specs=pl.BlockSpec((1, l_tok, LANES), lambda j, i: (i, 0, j)),
        out_shape=jax.ShapeDtypeStruct((b, l_tok, D_ATTN), F32),
        scratch_shapes=[pltpu.VMEM((2, N_TPAIR, GRID_W, LANES), F32),
                        pltpu.VMEM((2, l_tok, l_tok), F32)],
        compiler_params=_params("arbitrary", "arbitrary"),
        name="latent_attention",
    )(qkv, qkv, qkv, cache_k, cache_v, rpb_ext)


HY_TC = 256


def _dft_tables(length):
    n2 = 2 * length
    k = np.arange(length, dtype=np.float64)[:, None]
    t = np.arange(length, dtype=np.float64)[None, :]
    ang = 2.0 * np.pi * ((k * t) % n2) / n2
    fa = np.cos(ang)
    fb = -np.sin(ang)
    fb[0, :] = np.cos(np.pi * t[0])
    fwd = np.concatenate([fa, fb], axis=0)
    ga = 2.0 * np.cos(ang).T
    ga[:, 0] = 1.0
    gb = -2.0 * np.sin(ang).T
    gb[:, 0] = np.cos(np.pi * t[0])
    inv = np.concatenate([ga, gb], axis=1)
    return fwd.astype(np.float32), inv.astype(np.float32)


def _filter_consts(length):
    t = np.linspace(0.0, 1.0, length, dtype=np.float32)[:, None]
    w = (np.float32(2.0 * math.pi / length) * np.arange(length, dtype=np.float32))[:, None]
    bands = np.linspace(1e-4, FILTER_BANDS - 1, FILTER_BANDS, dtype=np.float32)[None, :]
    z = np.concatenate([t, np.cos(bands * w), -np.sin(bands * w)], axis=-1).astype(np.float32)
    zp = np.zeros((length, LANES), np.float32)
    zp[:, :FILTER_EMB] = z
    deltas = np.abs(np.linspace(math.log(DECAY_TARGET) / LONG_DECAY_PCT,
                                math.log(DECAY_TARGET) / SHORT_DECAY_PCT, D_HYENA, dtype=np.float32))
    window = (np.exp(-t * deltas) + np.float32(MOD_SHIFT)).astype(np.float32)
    return zp, window


def _split3(a, b):
    a_hi = a.astype(BF16)
    a_lo = (a - a_hi.astype(F32)).astype(BF16)
    b_hi = b.astype(BF16)
    b_lo = (b - b_hi.astype(F32)).astype(BF16)
    return (jnp.dot(a_hi, b_hi, preferred_element_type=F32)
            + jnp.dot(a_hi, b_lo, preferred_element_type=F32)
            + jnp.dot(a_lo, b_hi, preferred_element_type=F32))


def _filter_kernel(z_ref, win_ref, w1_ref, b1_ref, w2_ref, b2_ref, w3_ref, fr_ref, fwd_ref,
                   ka_ref, kb_ref, s_ref, d_ref, *, length):
    j = pl.program_id(0)

    @pl.when(j == 0)
    def _():
        fr = fr_ref[...]
        hid = jnp.sin(fr * (jnp.dot(z_ref[...], w1_ref[...], precision=HI, preferred_element_type=F32)
                            + b1_ref[...]))
        hid = jnp.sin(fr * (jnp.dot(hid, w2_ref[...], precision=HI, preferred_element_type=F32)
                            + b2_ref[...]))
        filt = jnp.dot(hid, w3_ref[...], precision=HI, preferred_element_type=F32)
        nc = 2 * D_HYENA
        win = win_ref[...]
        win2 = jnp.concatenate([win, win], axis=1)
        fw = filt[:, 0:nc] * win2
        row = lax.broadcasted_iota(jnp.int32, (length, nc), 0)
        bw = jnp.where(row == 0, 0.0, filt[:, nc:2 * nc] * win2)
        inv = 1.0 / jnp.sum(jnp.abs(fw) + jnp.abs(bw), axis=0, keepdims=True)
        s_ref[...] = (fw + bw) * inv
        d_ref[...] = (fw - bw) * inv

    fa = fwd_ref[0]
    fb = fwd_ref[1]
    ka = _split3(fa, s_ref[...])
    kb = _split3(fb, d_ref[...])
    nyq = _split3(fb[0:8, :], s_ref[...])[0:1, :]
    row = lax.broadcasted_iota(jnp.int32, kb.shape, 0)
    kb = jnp.where((row == 0) & (j == 0), nyq, kb)
    for c in range(D_HYENA // HY_TC):
        lo, hi = c * HY_TC, (c + 1) * HY_TC
        ka_ref[c] = jnp.concatenate([ka[:, lo:hi], ka[:, D_HYENA + lo:D_HYENA + hi]], axis=1)
        kb_ref[c] = jnp.concatenate([kb[:, lo:hi], kb[:, D_HYENA + lo:D_HYENA + hi]], axis=1)


def hyena_filter(length, w1p, b1, w2, b2, w3, freq, fwd_tab):
    zp, window = _filter_consts(length)
    nc = 2 * D_HYENA
    nct = D_HYENA // HY_TC
    tm = 256
    fwd3 = fwd_tab.reshape(2, length, length)
    full = lambda a: pl.BlockSpec(a.shape, lambda j: (0,) * a.ndim)
    args = [jnp.asarray(zp), jnp.asarray(window), w1p, b1.reshape(1, -1), w2, b2.reshape(1, -1), w3,
            freq.reshape(1, -1)]
    return pl.pallas_call(
        functools.partial(_filter_kernel, length=length),
        grid=(length // tm,),
        in_specs=[full(a) for a in args] + [pl.BlockSpec((2, tm, length), lambda j: (0, j, 0))],
        out_specs=[pl.BlockSpec((nct, tm, 2 * HY_TC), lambda j: (0, j, 0))] * 2,
        out_shape=[jax.ShapeDtypeStruct((nct, length, 2 * HY_TC), F32)] * 2,
        scratch_shapes=[pltpu.VMEM((length, nc), F32), pltpu.VMEM((length, nc), F32)],
        compiler_params=_params("arbitrary"),
        name="hyena_filter",
    )(*args, fwd3)


def _dft_forward(u, f_ref):
    return jnp.dot(f_ref[...], u.astype(BF16), preferred_element_type=F32)


def _spectrum_product(x, ka, kb, length):
    xa = x[0:length]
    xb = x[length:2 * length]
    row = lax.broadcasted_iota(jnp.int32, xa.shape, 0)
    dc = row == 0
    ya = xa * ka - jnp.where(dc, 0.0, xb * kb)
    yb = jnp.where(dc, xb * kb, xa * kb + xb * ka)
    return ya.astype(BF16), yb.astype(BF16)


def _dft_inverse(ya, yb, g_ref, length):
    y = (jnp.dot(g_ref[:, 0:length], ya, preferred_element_type=F32)
         + jnp.dot(g_ref[:, length:2 * length], yb, preferred_element_type=F32))
    return y * (1.0 / (2 * length))


def _hyena_kernel(v_ref, x1_ref, x2_ref, cw_ref, cb_ref, f_ref, g_ref, ka_ref, kb_ref, hb_ref, o_ref,
                  *, length, tc, nb):
    row = lax.broadcasted_iota(jnp.int32, (length, tc), 0)

    def short(u, part):
        up = jnp.where(row == 0, 0.0, pltpu.roll(u, 1, 0))
        dn = jnp.where(row == length - 1, 0.0, pltpu.roll(u, length - 1, 0))
        w = cw_ref[part]
        return up * w[0:1] + u * w[1:2] + dn * w[2:3] + cb_ref[part]

    bs = range(nb)
    u = [short(v_ref[bi], 0) for bi in bs]
    for order in range(2):
        ka = ka_ref[:, order * tc:(order + 1) * tc]
        kb = kb_ref[:, order * tc:(order + 1) * tc]
        spec = [_dft_forward(u[bi], f_ref) for bi in bs]
        spec = [_spectrum_product(x, ka, kb, length) for x in spec]
        conv = [_dft_inverse(ya, yb, g_ref, length) + u[bi] * hb_ref[order] for bi, (ya, yb) in zip(bs, spec)]
        gate_ref = x1_ref if order == 0 else x2_ref
        u = [short(gate_ref[bi], order + 1) * conv[bi] for bi in bs]
    for bi in bs:
        o_ref[bi] = u[bi]


def hyena_mixer(hy, conv_w, conv_b, fwd_bf16, inv_bf16, ka, kb, hy_bias, *, nb):
    b, length, _ = hy.shape
    tc = HY_TC
    nct = D_HYENA // tc
    cw = conv_w.reshape(3, 3, nct, 1, tc).transpose(1, 2, 0, 3, 4).reshape(3, nct, 3, tc)
    cb = conv_b.reshape(3, nct, 1, tc)
    hb = hy_bias.reshape(2, nct, 1, tc).transpose(1, 0, 2, 3)
    part = lambda p: pl.BlockSpec((nb, length, tc), lambda c, i: (i, 0, p * nct + c))
    return pl.pallas_call(
        functools.partial(_hyena_kernel, length=length, tc=tc, nb=nb),
        grid=(nct, b // nb),
        in_specs=[part(0), part(1), part(2),
                  pl.BlockSpec((3, None, 3, tc), lambda c, i: (0, c, 0, 0)),
                  pl.BlockSpec((3, None, 1, tc), lambda c, i: (0, c, 0, 0)),
                  pl.BlockSpec((2 * length, length), lambda c, i: (0, 0)),
                  pl.BlockSpec((length, 2 * length), lambda c, i: (0, 0)),
                  pl.BlockSpec((None, length, 2 * tc), lambda c, i: (c, 0, 0)),
                  pl.BlockSpec((None, length, 2 * tc), lambda c, i: (c, 0, 0)),
                  pl.BlockSpec((None, 2, 1, tc), lambda c, i: (c, 0, 0, 0))],
        out_specs=pl.BlockSpec((nb, length, tc), lambda c, i: (i, 0, c)),
        out_shape=jax.ShapeDtypeStruct((b, length, D_HYENA), F32),
        compiler_params=_params("arbitrary", "arbitrary"),
        name="hyena_mixer",
    )(hy, hy, hy, cw, cb, fwd_bf16, inv_bf16, ka, kb, hb)


def _outproj_kernel(a_ref, y_ref, x_ref, mod_ref, ga_ref, gh_ref, g2_ref, wo_ref, wr_ref,
                    x1_ref, h2_ref, aff_ref, *, nb, tt):
    rows = nb * tt
    a = a_ref[...].reshape(rows, D_ATTN)
    yh = y_ref[...].reshape(rows, D_HYENA)
    x = x_ref[...].reshape(rows, D_MODEL)
    mod = mod_ref[0]
    g1 = mod[:, 2 * D_MODEL:3 * D_MODEL]
    sh2 = mod[:, 3 * D_MODEL:4 * D_MODEL]
    sc2 = mod[:, 4 * D_MODEL:5 * D_MODEL]
    an = (_rms(a, D_ATTN) * ga_ref[...]).astype(BF16)
    yn = (_rms(yh, D_HYENA) * gh_ref[...]).astype(BF16)
    mix = (jnp.dot(an, wo_ref[0:D_ATTN, :], preferred_element_type=F32)
           + jnp.dot(yn, wo_ref[D_ATTN:, :], preferred_element_type=F32))
    x1 = x + g1 * mix
    h2 = _rms(x1, D_MODEL) * g2_ref[...] * (1.0 + sc2) + sh2
    x1_ref[...] = x1
    sub = D_MODEL // LANES
    for j in range(sub):
        h2_ref[pl.ds(j, rows, stride=sub), :] = h2[:, j * LANES:(j + 1) * LANES]
    logits = lax.dot_general(wr_ref[...], h2, _NT, precision=HI, preferred_element_type=F32)
    p = jnp.exp(logits - jnp.max(logits, axis=0, keepdims=True))
    aff_ref[...] = p / jnp.sum(p, axis=0, keepdims=True)


def out_projection(a, yh, x, mods3, out_g_attn, out_g_hyena, norm2_g, w_out_bf16, w_router_t,
                   *, nb, tt, mod_row):
    b, t, d = x.shape
    steps_per_b = t // tt
    n = b * t
    rows = nb * tt
    grid = (n // rows,)
    xmap = lambda i: (i // steps_per_b, i % steps_per_b, 0)
    vec = lambda w: pl.BlockSpec((1, w), lambda i: (0, 0))
    return pl.pallas_call(
        functools.partial(_outproj_kernel, nb=nb, tt=tt),
        grid=grid,
        in_specs=[pl.BlockSpec((nb, tt, D_ATTN), xmap),
                  pl.BlockSpec((nb, tt, D_HYENA), xmap),
                  pl.BlockSpec((nb, tt, d), xmap),
                  pl.BlockSpec((1, 1, N_MOD * d), lambda i: (mod_row(i), 0, 0)),
                  vec(D_ATTN), vec(D_HYENA), vec(d),
                  pl.BlockSpec((d, d), lambda i: (0, 0)),
                  pl.BlockSpec((N_EXPERTS, d), lambda i: (0, 0))],
        out_specs=[pl.BlockSpec((rows, d), lambda i: (i, 0)),
                   pl.BlockSpec((rows * d // LANES, LANES), lambda i: (i, 0)),
                   pl.BlockSpec((N_EXPERTS, rows), lambda i: (0, i))],
        out_shape=[jax.ShapeDtypeStruct((n, d), F32),
                   jax.ShapeDtypeStruct((n * d // LANES, LANES), F32),
                   jax.ShapeDtypeStruct((N_EXPERTS, n), F32)],
        compiler_params=_params("arbitrary"),
        name="out_projection",
    )(a, yh, x, mods3, out_g_attn.reshape(1, -1), out_g_hyena.reshape(1, -1), norm2_g.reshape(1, -1),
      w_out_bf16, w_router_t)


SEL_CHUNK = 256


def _cumsum_excl(mask_ref, out_ref, n):
    i = lax.broadcasted_iota(jnp.int32, (SEL_CHUNK, SEL_CHUNK), 0)
    j = lax.broadcasted_iota(jnp.int32, (SEL_CHUNK, SEL_CHUNK), 1)
    tri = (i < j).astype(BF16)
    carry = jnp.zeros((N_EXPERTS, 1), F32)
    for ch in range(n // SEL_CHUNK):
        sl = slice(ch * SEL_CHUNK, (ch + 1) * SEL_CHUNK)
        m = mask_ref[:, sl]
        out_ref[:, sl] = jnp.dot(m.astype(BF16), tri, preferred_element_type=F32) + carry
        carry = carry + jnp.sum(m, axis=1, keepdims=True)
    return carry


def _rank_kernel(aff_ref, pos_ref, rank_ref, mask_ref, *, n, cap):
    as_float = lambda bits: lax.bitcast_convert_type(bits, F32)

    def bit_step(i, thr):
        cand = thr | (jnp.int32(1) << (30 - i))
        cnt = jnp.sum((aff_ref[...] >= as_float(cand)).astype(jnp.int32), axis=1, keepdims=True)
        return jnp.where(cnt >= cap, cand, thr)

    thr = lax.fori_loop(0, 31, bit_step, jnp.zeros((N_EXPERTS, 1), jnp.int32))
    aff = aff_ref[...]
    gt = aff >= as_float(thr + 1)
    eq = (aff >= as_float(thr)) & jnp.logical_not(gt)
    need = cap - jnp.sum(gt.astype(jnp.int32), axis=1, keepdims=True)
    mask_ref[...] = eq.astype(F32)
    _cumsum_excl(mask_ref, rank_ref, n)
    sel = gt | (eq & (rank_ref[...] < need.astype(F32)))
    mask_ref[...] = sel.astype(F32)
    _cumsum_excl(mask_ref, rank_ref, n)
    pos_ref[...] = jnp.where(sel, rank_ref[...].astype(jnp.int32), -1)


def _invert_kernel(rng_ref, pos_ref, aff_ref, idx_ref, w_ref, part_i_ref, part_w_ref, *, n, cap):
    sblk = LANES
    nblk = cap // sblk
    lane_e = lax.broadcasted_iota(jnp.int32, (cap, N_EXPERTS), 1)
    lane_t = lax.broadcasted_iota(jnp.int32, (sblk, LANES), 1)
    idx_ref[...] = jnp.zeros(idx_ref.shape, jnp.int32)
    w_ref[...] = jnp.zeros(w_ref.shape, F32)
    for e in range(N_EXPERTS):
        def slot_block(sb, carry, e=e):
            s0 = pl.multiple_of(sb * sblk, sblk)
            slot = lax.broadcasted_iota(jnp.int32, (sblk, LANES), 0) + s0

            def chunk(tcn, acc):
                acc_i, acc_w = acc
                t0 = pl.multiple_of(tcn * LANES, LANES)
                hit = pos_ref[e:e + 1, pl.ds(t0, LANES)] == slot
                acc_i = acc_i + jnp.where(hit, lane_t + t0, 0)
                acc_w = acc_w + jnp.where(hit, aff_ref[e:e + 1, pl.ds(t0, LANES)], 0.0)
                return acc_i, acc_w

            acc_i, acc_w = lax.fori_loop(
                rng_ref[e, sb], rng_ref[e, nblk + sb], chunk,
                (jnp.zeros((sblk, LANES), jnp.int32), jnp.zeros((sblk, LANES), F32)))
            part_i_ref[pl.ds(s0, sblk), :] = acc_i
            part_w_ref[pl.ds(s0, sblk), :] = acc_w
            return carry

        lax.fori_loop(0, nblk, slot_block, 0)
        col_i = jnp.sum(part_i_ref[...], axis=1, keepdims=True)
        col_w = jnp.sum(part_w_ref[...], axis=1, keepdims=True)
        idx_ref[...] = jnp.where(lane_e == e, col_i, idx_ref[...])
        w_ref[...] = jnp.where(lane_e == e, col_w, w_ref[...])


def expert_select(aff_t):
    e, n = aff_t.shape
    cap = EC_FACTOR * n // N_EXPERTS
    full = pl.BlockSpec((e, n), lambda i: (0, 0))
    pos, rank = pl.pallas_call(
        functools.partial(_rank_kernel, n=n, cap=cap),
        grid=(1,),
        in_specs=[full],
        out_specs=[full, full],
        out_shape=[jax.ShapeDtypeStruct((e, n), jnp.int32), jax.ShapeDtypeStruct((e, n), F32)],
        scratch_shapes=[pltpu.VMEM((e, n), F32)],
        compiler_params=_params("arbitrary"),
        name="expert_rank",
    )(aff_t)
    rank = rank.astype(jnp.int32)
    total = jnp.full((e, 1), cap, jnp.int32)
    bounds = jnp.concatenate([rank[:, ::LANES], total], axis=1)
    blk0 = jnp.arange(0, cap, LANES, dtype=jnp.int32)[None, :, None]
    first = jnp.sum(bounds[:, None, 1:] <= blk0, axis=2, dtype=jnp.int32)
    last = jnp.sum(bounds[:, None, :-1] < blk0 + LANES, axis=2, dtype=jnp.int32)
    chunk_ranges = jnp.concatenate([jnp.minimum(first, n // LANES - 1), last], axis=1)
    idx_t, w_t = pl.pallas_call(
        functools.partial(_invert_kernel, n=n, cap=cap),
        grid_spec=pltpu.PrefetchScalarGridSpec(
            num_scalar_prefetch=1,
            grid=(1,),
            in_specs=[pl.BlockSpec((e, n), lambda i, cs: (0, 0)), pl.BlockSpec((e, n), lambda i, cs: (0, 0))],
            out_specs=[pl.BlockSpec((cap, e), lambda i, cs: (0, 0)), pl.BlockSpec((cap, e), lambda i, cs: (0, 0))],
            scratch_shapes=[pltpu.VMEM((cap, LANES), jnp.int32), pltpu.VMEM((cap, LANES), F32)]),
        out_shape=[jax.ShapeDtypeStruct((cap, e), jnp.int32), jax.ShapeDtypeStruct((cap, e), F32)],
        compiler_params=_params("arbitrary"),
        name="expert_invert",
    )(chunk_ranges, pos, aff_t)
    return idx_t.T, w_t.T, rank


FF_TILE = 256
FFN_ROWS = 512
Y_PAD = 64


TOK_SUB = D_MODEL // LANES


def _tok_copy(src_hbm, tok, xe_ref, slot, sem):
    return pltpu.make_async_copy(src_hbm.at[tok], xe_ref.at[:, slot, :], sem)


def _ffn_kernel(idx_ref, hp_hbm, hs_hbm, w1a_ref, w3a_ref, w2a_ref, w1b_ref, w3b_ref, w2b_ref, y_ref,
                xe_ref, xb_ref, sem, *, cap, gslots, per_step, ff_tiles):
    e = pl.program_id(0)
    f = pl.program_id(1)
    ne = pl.num_programs(0)
    nf = pl.num_programs(1)
    rows = 2 * cap

    def start_pair(ex, s):
        _tok_copy(hp_hbm, idx_ref[ex * gslots + s], xe_ref, s, sem).start()
        _tok_copy(hs_hbm, idx_ref[(ne + ex) * gslots + s], xe_ref, gslots + s, sem).start()

    def wait_all():
        def body(s, c):
            _tok_copy(hp_hbm, 0, xe_ref, s, sem).wait()
            _tok_copy(hs_hbm, 0, xe_ref, gslots + s, sem).wait()
            return c

        lax.fori_loop(0, gslots, body, 0, unroll=8)

    @pl.when((e == 0) & (f == 0))
    def _():
        def body(s, c):
            start_pair(0, s)
            return c

        lax.fori_loop(0, gslots, body, 0, unroll=8)

    @pl.when(f == 0)
    def _():
        wait_all()
        for g in range(2):
            for j in range(TOK_SUB):
                xb_ref[g * cap:(g + 1) * cap, j * LANES:(j + 1) * LANES] = (
                    xe_ref[j, g * gslots:g * gslots + cap, :].astype(BF16))
        y_ref[0] = jnp.zeros(y_ref.shape[1:], F32)

    nrc = rows // FFN_ROWS
    nxt = jnp.minimum(e + 1, ne - 1)

    def ff_tile(w1_ref, w3_ref, w2_ref, with_gather):
        w1b = w1_ref[0].astype(BF16)
        w3b = w3_ref[0].astype(BF16)
        w2b = w2_ref[0].astype(BF16)
        for rc in range(nrc):
            rs = slice(rc * FFN_ROWS, (rc + 1) * FFN_ROWS)
            xb = xb_ref[rs, :]
            h1 = jnp.dot(xb, w1b, preferred_element_type=F32)
            h3 = jnp.dot(xb, w3b, preferred_element_type=F32)
            hid = (_silu(h1) * h3).astype(BF16)
            y_ref[0, rs, :] += jnp.dot(hid, w2b, preferred_element_type=F32)
            if with_gather:
                for k in range(rc * per_step // nrc, (rc + 1) * per_step // nrc):
                    start_pair(nxt, f * per_step + k)

    ff_tile(w1a_ref, w3a_ref, w2a_ref, True)

    @pl.when(2 * f + 1 < ff_tiles)
    def _():
        ff_tile(w1b_ref, w3b_ref, w2b_ref, False)

    @pl.when((e == ne - 1) & (f == nf - 1))
    def _():
        wait_all()


def ffn_steps(ff):
    return -(-(ff // FF_TILE) // 2)


def expert_ffn(idx_flat, h2p, h2s, w1, w3, w2, *, cap, gslots):
    ne, d, ff = w1.shape
    rows = 2 * cap
    ff_tiles = ff // FF_TILE
    nf = ffn_steps(ff)
    per_step = gslots // nf
    assert per_step * nf == gslots and gslots >= cap
    first = lambda e, f: 2 * f
    second = lambda e, f: jnp.minimum(2 * f + 1, ff_tiles - 1)
    in_tile = lambda t: pl.BlockSpec((1, d, FF_TILE), lambda e, f, idx: (e, 0, t(e, f)))
    out_tile = lambda t: pl.BlockSpec((1, FF_TILE, d), lambda e, f, idx: (e, t(e, f), 0))
    grid_spec = pltpu.PrefetchScalarGridSpec(
        num_scalar_prefetch=1,
        grid=(ne, nf),
        in_specs=[pl.BlockSpec(memory_space=pl.ANY),
                  pl.BlockSpec(memory_space=pl.ANY),
                  in_tile(first), in_tile(first), out_tile(first),
                  in_tile(second), in_tile(second), out_tile(second)],
        out_specs=pl.BlockSpec((1, rows + Y_PAD, d), lambda e, f, idx: (e, 0, 0)),
        scratch_shapes=[pltpu.VMEM((TOK_SUB, 2 * gslots, LANES), F32), pltpu.VMEM((rows, d), BF16),
                        pltpu.SemaphoreType.DMA(())],
    )
    return pl.pallas_call(
        functools.partial(_ffn_kernel, cap=cap, gslots=gslots, per_step=per_step, ff_tiles=ff_tiles),
        grid_spec=grid_spec,
        out_shape=jax.ShapeDtypeStruct((ne, rows + Y_PAD, d), F32),
        compiler_params=_params("arbitrary", "arbitrary"),
        name="expert_ffn",
    )(idx_flat, h2p, h2s, w1, w3, w2, w1, w3, w2)


COMB_TILE = 512


COMB_GROUP = 8
STAGE_ROWS = COMB_TILE + Y_PAD


COMB_DEPTH = 8


def _combine_kernel(doff_ref, off_ref, wt_ref, x1_ref, mod_ref, fg_ref, y_hbm, o_ref, acc_ref, stage_ref, sem,
                    *, row0, cap):
    i = pl.program_id(0)
    nspan = pl.num_programs(0) * N_EXPERTS
    sub = TOK_SUB
    ahead = COMB_DEPTH - 1

    def span(p):
        e = lax.bitwise_and(p, N_EXPERTS - 1)
        ti = lax.shift_right_logical(p, N_EXPERTS.bit_length() - 1)
        a = off_ref[e, ti]
        cnt = off_ref[e, ti + 1] - a
        a8 = lax.shift_left(lax.shift_right_logical(a, 3), 3)
        lead = a - a8
        nchunk = lax.shift_right_logical(cnt + lead + (Y_PAD - 1), Y_PAD.bit_length() - 1)
        return e, a, cnt, a8, lead, nchunk

    def fetch(p, start):
        e, _, _, a8, _, nchunk = span(p)
        slot = lax.bitwise_and(p, COMB_DEPTH - 1)

        def body(ci, c):
            src0 = pl.multiple_of(row0 + a8 + ci * Y_PAD, 8)
            dst0 = pl.multiple_of(ci * Y_PAD, Y_PAD)
            for j in range(sub):
                cp = pltpu.make_async_copy(
                    y_hbm.at[e, pl.ds(src0, Y_PAD), pl.ds(j * LANES, LANES)],
                    stage_ref.at[pl.ds(pl.multiple_of(slot * STAGE_ROWS + dst0, Y_PAD), Y_PAD), j, :], sem.at[slot])
                cp.start() if start else cp.wait()
            return c

        lax.fori_loop(0, nchunk, body, 0)

    @pl.when(i == 0)
    def _():
        for p in range(ahead):
            fetch(jnp.int32(p), True)

    acc_ref[...] = jnp.zeros(acc_ref.shape, F32)

    def add_rows(tab0, row0_stage, count):
        dsts = [pl.ds(pl.multiple_of(doff_ref[tab0 + u], sub), sub) for u in range(count)]
        vals = [stage_ref[row0_stage + u] * wt_ref[tab0 + u] for u in range(count)]
        olds = [acc_ref[d, :] for d in dsts]
        for d, old, v in zip(dsts, olds, vals):
            acc_ref[d, :] = old + v

    def per_expert(ei, c):
        p = i * N_EXPERTS + ei

        @pl.when(p + ahead < nspan)
        def _():
            fetch(p + ahead, True)

        fetch(p, False)
        e, a, cnt, _, lead, _ = span(p)
        tab = e * cap + a
        srow = lax.bitwise_and(p, COMB_DEPTH - 1) * STAGE_ROWS + lead
        full = lax.shift_right_logical(cnt, 3)

        def group(g, cc):
            add_rows(tab + g * COMB_GROUP, srow + g * COMB_GROUP, COMB_GROUP)
            return cc

        lax.fori_loop(0, full, group, 0)

        def single(s, cc):
            add_rows(tab + s, srow + s, 1)
            return cc

        lax.fori_loop(full * COMB_GROUP, cnt, single, 0)
        return c

    lax.fori_loop(0, N_EXPERTS, per_expert, 0)
    g2 = mod_ref[0][:, 5 * D_MODEL:6 * D_MODEL]
    acc = jnp.concatenate([acc_ref[pl.ds(j, COMB_TILE, stride=sub), :] for j in range(sub)], axis=1)
    x = x1_ref[...] + g2 * acc
    o_ref[...] = _rms(x, D_MODEL) * fg_ref[...]


def combine(idx, offs, wts, x1, mods3, final_g, y_all, *, row0, mod_row):
    n, d = x1.shape
    assert COMB_GROUP == 8 and COMB_DEPTH & (COMB_DEPTH - 1) == 0
    doff = (idx % COMB_TILE) * TOK_SUB
    grid_spec = pltpu.PrefetchScalarGridSpec(
        num_scalar_prefetch=3,
        grid=(n // COMB_TILE,),
        in_specs=[pl.BlockSpec((COMB_TILE, d), lambda i, *_: (i, 0)),
                  pl.BlockSpec((1, 1, N_MOD * d), lambda i, *_: (mod_row(i), 0, 0)),
                  pl.BlockSpec((1, d), lambda i, *_: (0, 0)),
                  pl.BlockSpec(memory_space=pl.ANY)],
        out_specs=pl.BlockSpec((COMB_TILE, d), lambda i, *_: (i, 0)),
        scratch_shapes=[pltpu.VMEM((COMB_TILE * TOK_SUB, LANES), F32),
                        pltpu.VMEM((COMB_DEPTH * STAGE_ROWS, TOK_SUB, LANES), F32),
                        pltpu.SemaphoreType.DMA((COMB_DEPTH,))],
    )
    return pl.pallas_call(
        functools.partial(_combine_kernel, row0=row0, cap=idx.shape[1]),
        grid_spec=grid_spec,
        out_shape=jax.ShapeDtypeStruct((n, d), F32),
        compiler_params=_params("arbitrary"),
        name="combine",
    )(doff.reshape(-1), offs, wts.reshape(-1), x1, mods3, final_g.reshape(1, d), y_all)


def _tile_offsets(rank, cap):
    total = jnp.full((rank.shape[0], 1), cap, jnp.int32)
    return jnp.concatenate([rank[:, ::COMB_TILE], total], axis=1)


def kernel(x_prompt, x_sample, cache_k, cache_v, c, c_ctx, norm1_g, norm2_g, w_ada, b_ada, w_in, w_out,
           out_g_attn, out_g_hyena, rpb, conv_w, conv_b, filt_w1, filt_b1, filt_w2, filt_b2, filt_w3,
           filt_freq, hyena_bias, w_router, w1, w3, w2, final_g):
    depth = norm1_g.shape[0]
    assert depth == 1
    l = 0
    bp, tp, d = x_prompt.shape
    bs, ts, _ = x_sample.shape

    cond = jnp.zeros((MOD_ROWS, d), F32).at[0].set(c_ctx).at[1:1 + bs].set(c)
    mods3 = ada_mod(cond, w_ada[l], b_ada[l]).reshape(MOD_ROWS, 1, N_MOD * d)

    w_in_b = w_in[l].astype(BF16)
    w_out_b = w_out[l].astype(BF16)
    w_router_t = w_router[l].T
    rpb_ext = jnp.pad(rpb[l], ((0, 0), (0, 0), (RPB_PAD_L, RPB_PAD_R)), mode="edge")
    w1p = jnp.pad(filt_w1[l], ((0, LANES - FILTER_EMB), (0, 0)))

    ctx_row = lambda i: 0
    tt_s = 512
    lat_row = lambda i: 1 + i // (ts // tt_s)

    qkv_p, hy_p, state_k, state_v = in_projection(x_prompt, mods3, norm1_g[l], w_in_b, nb=2, tt=tp,
                                                  mod_row=ctx_row, with_state=True)
    qkv_s, hy_s = in_projection(x_sample, mods3, norm1_g[l], w_in_b, nb=1, tt=tt_s,
                                mod_row=lat_row, with_state=False)

    a_p = context_attention(qkv_p, nb=2)
    a_s = latent_attention(qkv_s, cache_k, cache_v, rpb_ext, l)

    def hyena(hy, length, nb):
        fwd_tab, inv_tab = _dft_tables(length)
        fwd_tab = jnp.asarray(fwd_tab)
        ka, kb = hyena_filter(length, w1p, filt_b1[l], filt_w2[l], filt_b2[l], filt_w3[l], filt_freq[l],
                              fwd_tab)
        return hyena_mixer(hy, conv_w[l], conv_b[l], fwd_tab.astype(BF16), jnp.asarray(inv_tab).astype(BF16),
                           ka, kb, hyena_bias[l], nb=nb)

    yh_p = hyena(hy_p, tp, 8)
    yh_s = hyena(hy_s, ts, 2)

    x1_p, h2_p, aff_p = out_projection(a_p, yh_p, x_prompt, mods3, out_g_attn[l], out_g_hyena[l],
                                       norm2_g[l], w_out_b, w_router_t, nb=2, tt=tp, mod_row=ctx_row)
    x1_s, h2_s, aff_s = out_projection(a_s, yh_s, x_sample, mods3, out_g_attn[l], out_g_hyena[l],
                                       norm2_g[l], w_out_b, w_router_t, nb=1, tt=tt_s, mod_row=lat_row)

    idx_p, wt_p, rank_p = expert_select(aff_p)
    idx_s, wt_s, rank_s = expert_select(aff_s)
    cap = idx_p.shape[1]
    nf = ffn_steps(w1.shape[3])
    gslots = -(-cap // (8 * nf)) * 8 * nf
    pad_to = lambda a, width: jnp.pad(a, ((0, 0), (0, width - a.shape[1])))
    idx_flat = jnp.concatenate([pad_to(idx_p, gslots), pad_to(idx_s, gslots)], axis=0).reshape(-1)
    tiles = lambda h2: h2.reshape(-1, TOK_SUB, LANES)
    y_all = expert_ffn(idx_flat, tiles(h2_p), tiles(h2_s), w1[l], w3[l], w2[l], cap=cap, gslots=gslots)

    comb_row_s = lambda i: 1 + i // (ts // COMB_TILE)
    y_p = combine(idx_p, _tile_offsets(rank_p, cap), wt_p, x1_p, mods3, final_g, y_all, row0=0, mod_row=ctx_row)
    y_s = combine(idx_s, _tile_offsets(rank_s, cap), wt_s, x1_s, mods3, final_g, y_all, row0=cap,
                  mod_row=comb_row_s)

    return (y_p.reshape(bp, tp, d), y_s.reshape(bs, ts, d),
            state_k.reshape(bp, depth, N_HEADS, tp, HEAD_DIM), state_v.reshape(bp, depth, N_HEADS, tp, HEAD_DIM))
```

```python
import functools
import math

import numpy as np
import jax
import jax.numpy as jnp
from jax import lax
from jax.experimental import pallas as pl
from jax.experimental.pallas import tpu as pltpu

D_MODEL = 1024
GRID_W = 64
D_ATTN = 512
D_HYENA = 512
HEAD_DIM = 64
N_HEADS = 8
WIN_ROWS = 8
WIN_COLS = 16
FILTER_EMB = 33
FILTER_BANDS = 16
FILTER_HIDDEN = 64
DECAY_TARGET = 1e-2
SHORT_DECAY_PCT = 0.3
LONG_DECAY_PCT = 1.5
MOD_SHIFT = 0.05
N_EXPERTS = 16
EC_FACTOR = 2
EXPERT_FF = 2816
N_MOD = 6
EPS = 1e-6
NEG_INF = -1e30

LANES = 128
MOD_ROWS = 16
VMEM_LIMIT = 56 * 1024 * 1024
HI = lax.Precision.HIGHEST
BF16 = jnp.bfloat16
F32 = jnp.float32

_NT = (((1,), (1,)), ((), ()))


def _params(*sem):
    return pltpu.CompilerParams(dimension_semantics=sem, vmem_limit_bytes=VMEM_LIMIT)


def _rms(x, n):
    return x * lax.rsqrt(jnp.sum(x * x, axis=-1, keepdims=True) * (1.0 / n) + EPS)


def _silu(x):
    return x * (1.0 / (1.0 + jnp.exp(-x)))


def _ada_kernel(c_ref, w_ref, b_ref, o_ref):
    s = _silu(c_ref[...])
    o_ref[...] = jnp.dot(s, w_ref[...], precision=HI, preferred_element_type=F32) + b_ref[...]


def ada_mod(cond, w_ada, b_ada):
    n = w_ada.shape[1]
    tn = 1024
    return pl.pallas_call(
        _ada_kernel,
        grid=(n // tn,),
        in_specs=[pl.BlockSpec((MOD_ROWS, D_MODEL), lambda j: (0, 0)),
                  pl.BlockSpec((D_MODEL, tn), lambda j: (0, j)),
                  pl.BlockSpec((1, tn), lambda j: (0, j))],
        out_specs=pl.BlockSpec((MOD_ROWS, tn), lambda j: (0, j)),
        out_shape=jax.ShapeDtypeStruct((MOD_ROWS, n), F32),
        compiler_params=_params("arbitrary"),
        name="ada_mod",
    )(cond, w_ada, b_ada.reshape(1, n))


def _inproj_kernel(x_ref, mod_ref, g_ref, w_ref, qkv_ref, hy_ref, *state_refs, nb, tt):
    x = x_ref[...].reshape(nb * tt, D_MODEL)
    mod = mod_ref[0]
    sh1 = mod[:, 0:D_MODEL]
    sc1 = mod[:, D_MODEL:2 * D_MODEL]
    h = _rms(x, D_MODEL) * g_ref[...] * (1.0 + sc1) + sh1
    proj = jnp.dot(h.astype(BF16), w_ref[...], preferred_element_type=F32)
    q = proj[:, 0:D_ATTN] * (HEAD_DIM ** -0.5)
    k = proj[:, D_ATTN:2 * D_ATTN]
    v = proj[:, 2 * D_ATTN:3 * D_ATTN]
    qkv_ref[:, :, 0:D_ATTN] = q.reshape(nb, tt, D_ATTN)
    qkv_ref[:, :, D_ATTN:2 * D_ATTN] = k.reshape(nb, tt, D_ATTN)
    qkv_ref[:, :, 2 * D_ATTN:3 * D_ATTN] = v.reshape(nb, tt, D_ATTN)
    hy_ref[...] = proj[:, 3 * D_ATTN:].reshape(nb, tt, 3 * D_HYENA)
    if state_refs:
        sk_ref, sv_ref = state_refs
        for bi in range(nb):
            for hd in range(N_HEADS):
                sl = slice(hd * HEAD_DIM, (hd + 1) * HEAD_DIM)
                sk_ref[bi, hd] = k[bi * tt:(bi + 1) * tt, sl]
                sv_ref[bi, hd] = v[bi * tt:(bi + 1) * tt, sl]


def in_projection(x, mods3, norm_g, w_in_bf16, *, nb, tt, mod_row, with_state):
    b, t, d = x.shape
    steps_per_b = t // tt
    grid = (b // nb * steps_per_b,)
    xmap = lambda i: (i // steps_per_b, i % steps_per_b, 0)
    nproj = w_in_bf16.shape[1]
    out_shape = [jax.ShapeDtypeStruct((b, t, 3 * D_ATTN), F32),
                 jax.ShapeDtypeStruct((b, t, 3 * D_HYENA), F32)]
    out_specs = [pl.BlockSpec((nb, tt, 3 * D_ATTN), xmap),
                 pl.BlockSpec((nb, tt, 3 * D_HYENA), xmap)]
    if with_state:
        assert steps_per_b == 1
        smap = lambda i: (i, 0, 0, 0)
        for _ in range(2):
            out_shape.append(jax.ShapeDtypeStruct((b, N_HEADS, t, HEAD_DIM), F32))
            out_specs.append(pl.BlockSpec((nb, N_HEADS, tt, HEAD_DIM), smap))
    return pl.pallas_call(
        functools.partial(_inproj_kernel, nb=nb, tt=tt),
        grid=grid,
        in_specs=[pl.BlockSpec((nb, tt, d), xmap),
                  pl.BlockSpec((1, 1, N_MOD * d), lambda i: (mod_row(i), 0, 0)),
                  pl.BlockSpec((1, d), lambda i: (0, 0)),
                  pl.BlockSpec((d, nproj), lambda i: (0, 0))],
        out_specs=out_specs,
        out_shape=out_shape,
        compiler_params=_params("arbitrary"),
        name="in_projection",
    )(x, mods3, norm_g.reshape(1, d), w_in_bf16)


def _lane_mask(shape, half):
    lane = lax.broadcasted_iota(jnp.int32, shape, 1)
    return (lane < HEAD_DIM) if half == 0 else (lane >= HEAD_DIM)


def _ctx_attn_kernel(qkv_ref, o_ref, *, nb):
    t = qkv_ref.shape[1]
    for bi in range(nb):
        for j in range(N_HEADS // 2):
            lo = j * LANES
            qp = qkv_ref[bi, :, lo:lo + LANES]
            kp = qkv_ref[bi, :, D_ATTN + lo:D_ATTN + lo + LANES].astype(BF16)
            vp = qkv_ref[bi, :, 2 * D_ATTN + lo:2 * D_ATTN + lo + LANES].astype(BF16)
            q2 = jnp.concatenate([jnp.where(_lane_mask(qp.shape, hh), qp, 0.0) for hh in range(2)],
                                 axis=0).astype(BF16)
            s = lax.dot_general(q2, kp, _NT, preferred_element_type=F32)
            p = jnp.exp(s - jnp.max(s, axis=-1, keepdims=True))
            l = jnp.sum(p, axis=-1, keepdims=True)
            o2 = jnp.dot(p.astype(BF16), vp, preferred_element_type=F32) * (1.0 / l)
            o_ref[bi, :, lo:lo + LANES] = jnp.where(_lane_mask(qp.shape, 0), o2[0:t], o2[t:2 * t])


def context_attention(qkv, *, nb):
    b, t, _ = qkv.shape
    return pl.pallas_call(
        functools.partial(_ctx_attn_kernel, nb=nb),
        grid=(b // nb,),
        in_specs=[pl.BlockSpec((nb, t, 3 * D_ATTN), lambda i: (i, 0, 0))],
        out_specs=pl.BlockSpec((nb, t, D_ATTN), lambda i: (i, 0, 0)),
        out_shape=jax.ShapeDtypeStruct((b, t, D_ATTN), F32),
        compiler_params=_params("arbitrary"),
        name="context_attention",
    )(qkv)


N_TPAIR = 2 * WIN_ROWS - 2
RPB_PAD_L = GRID_W - WIN_COLS
RPB_PAD_R = LANES - RPB_PAD_L - (2 * WIN_COLS - 1)


def _row_start(r, rows, wr):
    return min(max(r - wr // 2, 0), rows - wr)


def _build_bias(rpb_ref, tp_ref, m_ref, rows, wr):
    shape = (GRID_W, LANES)
    c = lax.broadcasted_iota(jnp.int32, shape, 0)
    lane = lax.broadcasted_iota(jnp.int32, shape, 1)
    kc = lane & (GRID_W - 1)
    ws = jnp.clip(c - WIN_COLS // 2, 0, GRID_W - WIN_COLS)
    col_ok = (kc >= ws) & (kc < ws + WIN_COLS)
    first = lane < GRID_W
    for hh in range(2):
        for dr in range(N_TPAIR):
            xa = jnp.broadcast_to(rpb_ref[hh, dr:dr + 1, :], shape)
            xb = jnp.broadcast_to(rpb_ref[hh, dr + 1:dr + 2, :], shape)
            ta = pltpu.roll(xa, GRID_W + 1, 1, stride=1, stride_axis=0)
            tb = pltpu.roll(xb, 1, 1, stride=1, stride_axis=0)
            tp_ref[hh, dr] = jnp.where(col_ok, jnp.where(first, ta, tb), NEG_INF)
        for r in range(rows):
            rs = _row_start(r, rows, wr)
            for jj in range(rows // 2):
                in_a = rs <= 2 * jj < rs + wr
                in_b = rs <= 2 * jj + 1 < rs + wr
                dr = 2 * jj - r + (WIN_ROWS - 1)
                if in_a and in_b:
                    blk = tp_ref[hh, dr]
                elif in_a:
                    blk = jnp.where(first, tp_ref[hh, dr], NEG_INF)
                elif in_b:
                    blk = jnp.where(first, NEG_INF, tp_ref[hh, dr])
                else:
                    blk = jnp.full(shape, NEG_INF, F32)
                m_ref[hh, r * GRID_W:(r + 1) * GRID_W, jj * LANES:(jj + 1) * LANES] = blk


def _key_range(qb, rows_per_blk, rows, wr):
    r0 = qb * rows_per_blk
    lo = min(_row_start(r, rows, wr) for r in range(r0, r0 + rows_per_blk))
    hi = max(_row_start(r, rows, wr) for r in range(r0, r0 + rows_per_blk)) + wr
    grp = 256 // GRID_W
    return (lo // grp) * 256, -(-hi // grp) * 256


def _lat_attn_kernel(q_ref, k_ref, v_ref, ck_ref, cv_ref, rpb_ref, o_ref, tp_ref, m_ref, *, rows, wr):
    @pl.when(pl.program_id(1) == 0)
    def _():
        _build_bias(rpb_ref, tp_ref, m_ref, rows, wr)

    l_tok = rows * GRID_W
    qblk = 256
    rows_per_blk = qblk // GRID_W
    sel_r = lax.broadcasted_iota(jnp.int32, (HEAD_DIM, LANES), 0)
    sel_c = lax.broadcasted_iota(jnp.int32, (HEAD_DIM, LANES), 1)

    def lane_pair(c_ref):
        parts = [jnp.dot(c_ref[0, 0, hh].astype(BF16), (sel_c == sel_r + hh * HEAD_DIM).astype(BF16),
                         preferred_element_type=F32) for hh in range(2)]
        return (parts[0] + parts[1]).astype(BF16)

    ck = lane_pair(ck_ref)
    cv = lane_pair(cv_ref)
    for qb in range(l_tok // qblk):
        c0, c1 = _key_range(qb, rows_per_blk, rows, wr)
        rs = slice(qb * qblk, (qb + 1) * qblk)
        qp = q_ref[0, rs, :]
        kp = k_ref[0, c0:c1, :].astype(BF16)
        vp = v_ref[0, c0:c1, :].astype(BF16)
        q2 = jnp.concatenate([jnp.where(_lane_mask(qp.shape, hh), qp, 0.0) for hh in range(2)], axis=0).astype(BF16)
        bias = jnp.concatenate([m_ref[hh, rs, c0:c1] for hh in range(2)], axis=0)
        s_win = lax.dot_general(q2, kp, _NT, preferred_element_type=F32) + bias
        s_ctx = lax.dot_general(q2, ck, _NT, preferred_element_type=F32)
        mx = jnp.maximum(jnp.max(s_win, axis=-1, keepdims=True), jnp.max(s_ctx, axis=-1, keepdims=True))
        p_win = jnp.exp(s_win - mx)
        p_ctx = jnp.exp(s_ctx - mx)
        l = jnp.sum(p_win, axis=-1, keepdims=True) + jnp.sum(p_ctx, axis=-1, keepdims=True)
        o2 = (jnp.dot(p_win.astype(BF16), vp, preferred_element_type=F32)
              + jnp.dot(p_ctx.astype(BF16), cv, preferred_element_type=F32)) * (1.0 / l)
        o_ref[0, rs, :] = jnp.where(_lane_mask(qp.shape, 0), o2[0:qblk], o2[qblk:2 * qblk])


def latent_attention(qkv, cache_k, cache_v, rpb_ext, layer):
    b, l_tok, _ = qkv.shape
    rows = l_tok // GRID_W
    wr = min(WIN_ROWS, rows)
    assert rows % 4 == 0 and wr == WIN_ROWS
    npair = N_HEADS // 2
    past = cache_k.shape[3]
    blk = lambda off: pl.BlockSpec((1, l_tok, LANES), lambda j, i: (i, 0, off + j))
    cspec = pl.BlockSpec((1, 1, 2, past, HEAD_DIM), lambda j, i: (i, layer, j, 0, 0))
    return pl.pallas_call(
        functools.partial(_lat_attn_kernel, rows=rows, wr=wr),
        grid=(npair, b),
        in_specs=[blk(0), blk(npair), blk(2 * npair), cspec, cspec,
                  pl.BlockSpec((2, 2 * WIN_ROWS - 1, LANES), lambda j, i: (j, 0, 0))],
        out_specs=pl.BlockSpec((1, l_tok, LANES), lambda j, i: (i, 0, j)),
        out_shape=jax.ShapeDtypeStruct((b, l_tok, D_ATTN), F32),
        scratch_shapes=[pltpu.VMEM((2, N_TPAIR, GRID_W, LANES), F32),
                        pltpu.VMEM((2, l_tok, l_tok), F32)],
        compiler_params=_params("arbitrary", "arbitrary"),
        name="latent_attention",
    )(qkv, qkv, qkv, cache_k, cache_v, rpb_ext)


HY_TC = 256


def _dft_tables(length):
    n2 = 2 * length
    k = np.arange(length, dtype=np.float64)[:, None]
    t = np.arange(length, dtype=np.float64)[None, :]
    ang = 2.0 * np.pi * ((k * t) % n2) / n2
    fa = np.cos(ang)
    fb = -np.sin(ang)
    fb[0, :] = np.cos(np.pi * t[0])
    fwd = np.concatenate([fa, fb], axis=0)
    ga = 2.0 * np.cos(ang).T
    ga[:, 0] = 1.0
    gb = -2.0 * np.sin(ang).T
    gb[:, 0] = np.cos(np.pi * t[0])
    inv = np.concatenate([ga, gb], axis=1)
    return fwd.astype(np.float32), inv.astype(np.float32)


def _filter_consts(length):
    t = np.linspace(0.0, 1.0, length, dtype=np.float32)[:, None]
    w = (np.float32(2.0 * math.pi / length) * np.arange(length, dtype=np.float32))[:, None]
    bands = np.linspace(1e-4, FILTER_BANDS - 1, FILTER_BANDS, dtype=np.float32)[None, :]
    z = np.concatenate([t, np.cos(bands * w), -np.sin(bands * w)], axis=-1).astype(np.float32)
    zp = np.zeros((length, LANES), np.float32)
    zp[:, :FILTER_EMB] = z
    deltas = np.abs(np.linspace(math.log(DECAY_TARGET) / LONG_DECAY_PCT,
                                math.log(DECAY_TARGET) / SHORT_DECAY_PCT, D_HYENA, dtype=np.float32))
    window = (np.exp(-t * deltas) + np.float32(MOD_SHIFT)).astype(np.float32)
    return zp, window


def _split3(a, b):
    a_hi = a.astype(BF16)
    a_lo = (a - a_hi.astype(F32)).astype(BF16)
    b_hi = b.astype(BF16)
    b_lo = (b - b_hi.astype(F32)).astype(BF16)
    return (jnp.dot(a_hi, b_hi, preferred_element_type=F32)
            + jnp.dot(a_hi, b_lo, preferred_element_type=F32)
            + jnp.dot(a_lo, b_hi, preferred_element_type=F32))


def _filter_kernel(z_ref, win_ref, w1_ref, b1_ref, w2_ref, b2_ref, w3_ref, fr_ref, fwd_ref,
                   ka_ref, kb_ref, s_ref, d_ref, *, length):
    j = pl.program_id(0)

    @pl.when(j == 0)
    def _():
        fr = fr_ref[...]
        hid = jnp.sin(fr * (jnp.dot(z_ref[...], w1_ref[...], precision=HI, preferred_element_type=F32)
                            + b1_ref[...]))
        hid = jnp.sin(fr * (jnp.dot(hid, w2_ref[...], precision=HI, preferred_element_type=F32)
                            + b2_ref[...]))
        filt = jnp.dot(hid, w3_ref[...], precision=HI, preferred_element_type=F32)
        nc = 2 * D_HYENA
        win = win_ref[...]
        win2 = jnp.concatenate([win, win], axis=1)
        fw = filt[:, 0:nc] * win2
        row = lax.broadcasted_iota(jnp.int32, (length, nc), 0)
        bw = jnp.where(row == 0, 0.0, filt[:, nc:2 * nc] * win2)
        inv = 1.0 / jnp.sum(jnp.abs(fw) + jnp.abs(bw), axis=0, keepdims=True)
        s_ref[...] = (fw + bw) * inv
        d_ref[...] = (fw - bw) * inv

    fa = fwd_ref[0]
    fb = fwd_ref[1]
    ka = _split3(fa, s_ref[...])
    kb = _split3(fb, d_ref[...])
    nyq = _split3(fb[0:8, :], s_ref[...])[0:1, :]
    row = lax.broadcasted_iota(jnp.int32, kb.shape, 0)
    kb = jnp.where((row == 0) & (j == 0), nyq, kb)
    for c in range(D_HYENA // HY_TC):
        lo, hi = c * HY_TC, (c + 1) * HY_TC
        ka_ref[c] = jnp.concatenate([ka[:, lo:hi], ka[:, D_HYENA + lo:D_HYENA + hi]], axis=1)
        kb_ref[c] = jnp.concatenate([kb[:, lo:hi], kb[:, D_HYENA + lo:D_HYENA + hi]], axis=1)


def hyena_filter(length, w1p, b1, w2, b2, w3, freq, fwd_tab):
    zp, window = _filter_consts(length)
    nc = 2 * D_HYENA
    nct = D_HYENA // HY_TC
    tm = 256
    fwd3 = fwd_tab.reshape(2, length, length)
    full = lambda a: pl.BlockSpec(a.shape, lambda j: (0,) * a.ndim)
    args = [jnp.asarray(zp), jnp.asarray(window), w1p, b1.reshape(1, -1), w2, b2.reshape(1, -1), w3,
            freq.reshape(1, -1)]
    return pl.pallas_call(
        functools.partial(_filter_kernel, length=length),
        grid=(length // tm,),
        in_specs=[full(a) for a in args] + [pl.BlockSpec((2, tm, length), lambda j: (0, j, 0))],
        out_specs=[pl.BlockSpec((nct, tm, 2 * HY_TC), lambda j: (0, j, 0))] * 2,
        out_shape=[jax.ShapeDtypeStruct((nct, length, 2 * HY_TC), F32)] * 2,
        scratch_shapes=[pltpu.VMEM((length, nc), F32), pltpu.VMEM((length, nc), F32)],
        compiler_params=_params("arbitrary"),
        name="hyena_filter",
    )(*args, fwd3)


def _dft_forward(u, f_ref):
    return jnp.dot(f_ref[...], u.astype(BF16), preferred_element_type=F32)


def _spectrum_product(x, ka, kb, length):
    xa = x[0:length]
    xb = x[length:2 * length]
    row = lax.broadcasted_iota(jnp.int32, xa.shape, 0)
    dc = row == 0
    ya = xa * ka - jnp.where(dc, 0.0, xb * kb)
    yb = jnp.where(dc, xb * kb, xa * kb + xb * ka)
    return ya.astype(BF16), yb.astype(BF16)


def _dft_inverse(ya, yb, g_ref, length):
    y = (jnp.dot(g_ref[:, 0:length], ya, preferred_element_type=F32)
         + jnp.dot(g_ref[:, length:2 * length], yb, preferred_element_type=F32))
    return y * (1.0 / (2 * length))


def _hyena_kernel(v_ref, x1_ref, x2_ref, cw_ref, cb_ref, f_ref, g_ref, ka_ref, kb_ref, hb_ref, o_ref,
                  *, length, tc, nb):
    row = lax.broadcasted_iota(jnp.int32, (length, tc), 0)

    def short(u, part):
        up = jnp.where(row == 0, 0.0, pltpu.roll(u, 1, 0))
        dn = jnp.where(row == length - 1, 0.0, pltpu.roll(u, length - 1, 0))
        w = cw_ref[part]
        return up * w[0:1] + u * w[1:2] + dn * w[2:3] + cb_ref[part]

    bs = range(nb)
    u = [short(v_ref[bi], 0) for bi in bs]
    for order in range(2):
        ka = ka_ref[:, order * tc:(order + 1) * tc]
        kb = kb_ref[:, order * tc:(order + 1) * tc]
        spec = [_dft_forward(u[bi], f_ref) for bi in bs]
        spec = [_spectrum_product(x, ka, kb, length) for x in spec]
        conv = [_dft_inverse(ya, yb, g_ref, length) + u[bi] * hb_ref[order] for bi, (ya, yb) in zip(bs, spec)]
        gate_ref = x1_ref if order == 0 else x2_ref
        u = [short(gate_ref[bi], order + 1) * conv[bi] for bi in bs]
    for bi in bs:
        o_ref[bi] = u[bi]


def hyena_mixer(hy, conv_w, conv_b, fwd_bf16, inv_bf16, ka, kb, hy_bias, *, nb):
    b, length, _ = hy.shape
    tc = HY_TC
    nct = D_HYENA // tc
    cw = conv_w.reshape(3, 3, nct, 1, tc).transpose(1, 2, 0, 3, 4).reshape(3, nct, 3, tc)
    cb = conv_b.reshape(3, nct, 1, tc)
    hb = hy_bias.reshape(2, nct, 1, tc).transpose(1, 0, 2, 3)
    part = lambda p: pl.BlockSpec((nb, length, tc), lambda c, i: (i, 0, p * nct + c))
    return pl.pallas_call(
        functools.partial(_hyena_kernel, length=length, tc=tc, nb=nb),
        grid=(nct, b // nb),
        in_specs=[part(0), part(1), part(2),
                  pl.BlockSpec((3, None, 3, tc), lambda c, i: (0, c, 0, 0)),
                  pl.BlockSpec((3, None, 1, tc), lambda c, i: (0, c, 0, 0)),
                  pl.BlockSpec((2 * length, length), lambda c, i: (0, 0)),
                  pl.BlockSpec((length, 2 * length), lambda c, i: (0, 0)),
                  pl.BlockSpec((None, length, 2 * tc), lambda c, i: (c, 0, 0)),
                  pl.BlockSpec((None, length, 2 * tc), lambda c, i: (c, 0, 0)),
                  pl.BlockSpec((None, 2, 1, tc), lambda c, i: (c, 0, 0, 0))],
        out_specs=pl.BlockSpec((nb, length, tc), lambda c, i: (i, 0, c)),
        out_shape=jax.ShapeDtypeStruct((b, length, D_HYENA), F32),
        compiler_params=_params("arbitrary", "arbitrary"),
        name="hyena_mixer",
    )(hy, hy, hy, cw, cb, fwd_bf16, inv_bf16, ka, kb, hb)


def _outproj_kernel(a_ref, y_ref, x_ref, mod_ref, ga_ref, gh_ref, g2_ref, wo_ref, wr_ref,
                    x1_ref, h2_ref, aff_ref, *, nb, tt):
    rows = nb * tt
    a = a_ref[...].reshape(rows, D_ATTN)
    yh = y_ref[...].reshape(rows, D_HYENA)
    x = x_ref[...].reshape(rows, D_MODEL)
    mod = mod_ref[0]
    g1 = mod[:, 2 * D_MODEL:3 * D_MODEL]
    sh2 = mod[:, 3 * D_MODEL:4 * D_MODEL]
    sc2 = mod[:, 4 * D_MODEL:5 * D_MODEL]
    an = (_rms(a, D_ATTN) * ga_ref[...]).astype(BF16)
    yn = (_rms(yh, D_HYENA) * gh_ref[...]).astype(BF16)
    mix = (jnp.dot(an, wo_ref[0:D_ATTN, :], preferred_element_type=F32)
           + jnp.dot(yn, wo_ref[D_ATTN:, :], preferred_element_type=F32))
    x1 = x + g1 * mix
    h2 = _rms(x1, D_MODEL) * g2_ref[...] * (1.0 + sc2) + sh2
    x1_ref[...] = x1
    sub = D_MODEL // LANES
    for j in range(sub):
        h2_ref[pl.ds(j, rows, stride=sub), :] = h2[:, j * LANES:(j + 1) * LANES]
    logits = lax.dot_general(wr_ref[...], h2, _NT, precision=HI, preferred_element_type=F32)
    p = jnp.exp(logits - jnp.max(logits, axis=0, keepdims=True))
    aff_ref[...] = p / jnp.sum(p, axis=0, keepdims=True)


def out_projection(a, yh, x, mods3, out_g_attn, out_g_hyena, norm2_g, w_out_bf16, w_router_t,
                   *, nb, tt, mod_row):
    b, t, d = x.shape
    steps_per_b = t // tt
    n = b * t
    rows = nb * tt
    grid = (n // rows,)
    xmap = lambda i: (i // steps_per_b, i % steps_per_b, 0)
    vec = lambda w: pl.BlockSpec((1, w), lambda i: (0, 0))
    return pl.pallas_call(
        functools.partial(_outproj_kernel, nb=nb, tt=tt),
        grid=grid,
        in_specs=[pl.BlockSpec((nb, tt, D_ATTN), xmap),
                  pl.BlockSpec((nb, tt, D_HYENA), xmap),
                  pl.BlockSpec((nb, tt, d), xmap),
                  pl.BlockSpec((1, 1, N_MOD * d), lambda i: (mod_row(i), 0, 0)),
                  vec(D_ATTN), vec(D_HYENA), vec(d),
                  pl.BlockSpec((d, d), lambda i: (0, 0)),
                  pl.BlockSpec((N_EXPERTS, d), lambda i: (0, 0))],
        out_specs=[pl.BlockSpec((rows, d), lambda i: (i, 0)),
                   pl.BlockSpec((rows * d // LANES, LANES), lambda i: (i, 0)),
                   pl.BlockSpec((N_EXPERTS, rows), lambda i: (0, i))],
        out_shape=[jax.ShapeDtypeStruct((n, d), F32),
                   jax.ShapeDtypeStruct((n * d // LANES, LANES), F32),
                   jax.ShapeDtypeStruct((N_EXPERTS, n), F32)],
        compiler_params=_params("arbitrary"),
        name="out_projection",
    )(a, yh, x, mods3, out_g_attn.reshape(1, -1), out_g_hyena.reshape(1, -1), norm2_g.reshape(1, -1),
      w_out_bf16, w_router_t)


SEL_CHUNK = 256


def _cumsum_excl(mask_ref, out_ref, n):
    i = lax.broadcasted_iota(jnp.int32, (SEL_CHUNK, SEL_CHUNK), 0)
    j = lax.broadcasted_iota(jnp.int32, (SEL_CHUNK, SEL_CHUNK), 1)
    tri = (i < j).astype(BF16)
    carry = jnp.zeros((N_EXPERTS, 1), F32)
    for ch in range(n // SEL_CHUNK):
        sl = slice(ch * SEL_CHUNK, (ch + 1) * SEL_CHUNK)
        m = mask_ref[:, sl]
        out_ref[:, sl] = jnp.dot(m.astype(BF16), tri, preferred_element_type=F32) + carry
        carry = carry + jnp.sum(m, axis=1, keepdims=True)
    return carry


def _rank_kernel(aff_ref, pos_ref, rank_ref, mask_ref, *, n, cap):
    as_float = lambda bits: lax.bitcast_convert_type(bits, F32)

    def bit_step(i, thr):
        cand = thr | (jnp.int32(1) << (30 - i))
        cnt = jnp.sum((aff_ref[...] >= as_float(cand)).astype(jnp.int32), axis=1, keepdims=True)
        return jnp.where(cnt >= cap, cand, thr)

    thr = lax.fori_loop(0, 31, bit_step, jnp.zeros((N_EXPERTS, 1), jnp.int32))
    aff = aff_ref[...]
    gt = aff >= as_float(thr + 1)
    eq = (aff >= as_float(thr)) & jnp.logical_not(gt)
    need = cap - jnp.sum(gt.astype(jnp.int32), axis=1, keepdims=True)
    mask_ref[...] = eq.astype(F32)
    _cumsum_excl(mask_ref, rank_ref, n)
    sel = gt | (eq & (rank_ref[...] < need.astype(F32)))
    mask_ref[...] = sel.astype(F32)
    _cumsum_excl(mask_ref, rank_ref, n)
    pos_ref[...] = jnp.where(sel, rank_ref[...].astype(jnp.int32), -1)


def _invert_kernel(rng_ref, pos_ref, aff_ref, idx_ref, w_ref, part_i_ref, part_w_ref, *, n, cap):
    sblk = LANES
    nblk = cap // sblk
    lane_e = lax.broadcasted_iota(jnp.int32, (cap, N_EXPERTS), 1)
    lane_t = lax.broadcasted_iota(jnp.int32, (sblk, LANES), 1)
    idx_ref[...] = jnp.zeros(idx_ref.shape, jnp.int32)
    w_ref[...] = jnp.zeros(w_ref.shape, F32)
    for e in range(N_EXPERTS):
        def slot_block(sb, carry, e=e):
            s0 = pl.multiple_of(sb * sblk, sblk)
            slot = lax.broadcasted_iota(jnp.int32, (sblk, LANES), 0) + s0

            def chunk(tcn, acc):
                acc_i, acc_w = acc
                t0 = pl.multiple_of(tcn * LANES, LANES)
                hit = pos_ref[e:e + 1, pl.ds(t0, LANES)] == slot
                acc_i = acc_i + jnp.where(hit, lane_t + t0, 0)
                acc_w = acc_w + jnp.where(hit, aff_ref[e:e + 1, pl.ds(t0, LANES)], 0.0)
                return acc_i, acc_w

            acc_i, acc_w = lax.fori_loop(
                rng_ref[e, sb], rng_ref[e, nblk + sb], chunk,
                (jnp.zeros((sblk, LANES), jnp.int32), jnp.zeros((sblk, LANES), F32)))
            part_i_ref[pl.ds(s0, sblk), :] = acc_i
            part_w_ref[pl.ds(s0, sblk), :] = acc_w
            return carry

        lax.fori_loop(0, nblk, slot_block, 0)
        col_i = jnp.sum(part_i_ref[...], axis=1, keepdims=True)
        col_w = jnp.sum(part_w_ref[...], axis=1, keepdims=True)
        idx_ref[...] = jnp.where(lane_e == e, col_i, idx_ref[...])
        w_ref[...] = jnp.where(lane_e == e, col_w, w_ref[...])


def expert_select(aff_t):
    e, n = aff_t.shape
    cap = EC_FACTOR * n // N_EXPERTS
    full = pl.BlockSpec((e, n), lambda i: (0, 0))
    pos, rank = pl.pallas_call(
        functools.partial(_rank_kernel, n=n, cap=cap),
        grid=(1,),
        in_specs=[full],
        out_specs=[full, full],
        out_shape=[jax.ShapeDtypeStruct((e, n), jnp.int32), jax.ShapeDtypeStruct((e, n), F32)],
        scratch_shapes=[pltpu.VMEM((e, n), F32)],
        compiler_params=_params("arbitrary"),
        name="expert_rank",
    )(aff_t)
    rank = rank.astype(jnp.int32)
    total = jnp.full((e, 1), cap, jnp.int32)
    bounds = jnp.concatenate([rank[:, ::LANES], total], axis=1)
    blk0 = jnp.arange(0, cap, LANES, dtype=jnp.int32)[None, :, None]
    first = jnp.sum(bounds[:, None, 1:] <= blk0, axis=2, dtype=jnp.int32)
    last = jnp.sum(bounds[:, None, :-1] < blk0 + LANES, axis=2, dtype=jnp.int32)
    chunk_ranges = jnp.concatenate([jnp.minimum(first, n // LANES - 1), last], axis=1)
    idx_t, w_t = pl.pallas_call(
        functools.partial(_invert_kernel, n=n, cap=cap),
        grid_spec=pltpu.PrefetchScalarGridSpec(
            num_scalar_prefetch=1,
            grid=(1,),
            in_specs=[pl.BlockSpec((e, n), lambda i, cs: (0, 0)), pl.BlockSpec((e, n), lambda i, cs: (0, 0))],
            out_specs=[pl.BlockSpec((cap, e), lambda i, cs: (0, 0)), pl.BlockSpec((cap, e), lambda i, cs: (0, 0))],
            scratch_shapes=[pltpu.VMEM((cap, LANES), jnp.int32), pltpu.VMEM((cap, LANES), F32)]),
        out_shape=[jax.ShapeDtypeStruct((cap, e), jnp.int32), jax.ShapeDtypeStruct((cap, e), F32)],
        compiler_params=_params("arbitrary"),
        name="expert_invert",
    )(chunk_ranges, pos, aff_t)
    return idx_t.T, w_t.T, rank


FF_TILE = 256
FFN_ROWS = 512
Y_PAD = 64


TOK_SUB = D_MODEL // LANES


def _tok_copy(src_hbm, tok, xe_ref, slot, sem):
    return pltpu.make_async_copy(src_hbm.at[tok], xe_ref.at[:, slot, :], sem)


def _ffn_kernel(idx_ref, hp_hbm, hs_hbm, w1_hbm, w3_hbm, w2_hbm, y_ref, xe_ref, xb_ref, w1_buf, w3_buf, w2_buf,
                sem, wsem, *, cap, gslots, per_tile, nf):
    e = pl.program_id(0)
    ne = pl.num_programs(0)
    rows = 2 * cap
    nrc = rows // FFN_ROWS

    def start_pair(ex, s):
        _tok_copy(hp_hbm, idx_ref[ex * gslots + s], xe_ref, s, sem).start()
        _tok_copy(hs_hbm, idx_ref[(ne + ex) * gslots + s], xe_ref, gslots + s, sem).start()

    def wait_all():
        def body(s, c):
            _tok_copy(hp_hbm, 0, xe_ref, s, sem).wait()
            _tok_copy(hs_hbm, 0, xe_ref, gslots + s, sem).wait()
            return c

        lax.fori_loop(0, gslots, body, 0, unroll=8)

    def weight_copies(ex, f, slot):
        col = pl.ds(pl.multiple_of(f * FF_TILE, FF_TILE), FF_TILE)
        return (pltpu.make_async_copy(w1_hbm.at[ex, :, col], w1_buf.at[slot], wsem.at[slot]),
                pltpu.make_async_copy(w3_hbm.at[ex, :, col], w3_buf.at[slot], wsem.at[slot]),
                pltpu.make_async_copy(w2_hbm.at[ex, col, :], w2_buf.at[slot], wsem.at[slot]))

    @pl.when(e == 0)
    def _():
        def body(s, c):
            start_pair(0, s)
            return c

        lax.fori_loop(0, gslots, body, 0, unroll=8)
        for cp in weight_copies(0, 0, 0):
            cp.start()

    wait_all()
    for g in range(2):
        for j in range(TOK_SUB):
            xb_ref[g * cap:(g + 1) * cap, j * LANES:(j + 1) * LANES] = (
                xe_ref[j, g * gslots:g * gslots + cap, :].astype(BF16))
    y_ref[0] = jnp.zeros(y_ref.shape[1:], F32)

    nxt = jnp.minimum(e + 1, ne - 1)

    def ff_tile(f, carry):
        t = e * nf + f
        slot = lax.rem(t, 2)
        last = f == nf - 1
        for cp in weight_copies(jnp.where(last, nxt, e), jnp.where(last, 0, f + 1), 1 - slot):
            cp.start()
        for cp in weight_copies(e, f, slot):
            cp.wait()
        w1b = w1_buf[slot].astype(BF16)
        w3b = w3_buf[slot].astype(BF16)
        w2b = w2_buf[slot].astype(BF16)
        for rc in range(nrc):
            rs = slice(rc * FFN_ROWS, (rc + 1) * FFN_ROWS)
            xb = xb_ref[rs, :]
            h1 = jnp.dot(xb, w1b, preferred_element_type=F32)
            h3 = jnp.dot(xb, w3b, preferred_element_type=F32)
            hid = (_silu(h1) * h3).astype(BF16)
            y_ref[0, rs, :] += jnp.dot(hid, w2b, preferred_element_type=F32)
            for k in range(rc * per_tile // nrc, (rc + 1) * per_tile // nrc):
                start_pair(nxt, f * per_tile + k)
        return carry

    lax.fori_loop(0, nf, ff_tile, 0)

    @pl.when(e == ne - 1)
    def _():
        wait_all()
        for cp in weight_copies(e, 0, lax.rem(ne * nf, 2)):
            cp.wait()


def expert_ffn(idx_flat, h2p, h2s, w1, w3, w2, *, cap, gslots):
    ne, d, ff = w1.shape
    rows = 2 * cap
    nf = ff // FF_TILE
    per_tile = gslots // nf
    assert nf * FF_TILE == ff and per_tile * nf == gslots and gslots >= cap
    any_spec = pl.BlockSpec(memory_space=pl.ANY)
    grid_spec = pltpu.PrefetchScalarGridSpec(
        num_scalar_prefetch=1,
        grid=(ne,),
        in_specs=[any_spec] * 5,
        out_specs=pl.BlockSpec((1, rows + Y_PAD, d), lambda e, idx: (e, 0, 0)),
        scratch_shapes=[pltpu.VMEM((TOK_SUB, 2 * gslots, LANES), F32), pltpu.VMEM((rows, d), BF16),
                        pltpu.VMEM((2, d, FF_TILE), F32), pltpu.VMEM((2, d, FF_TILE), F32),
                        pltpu.VMEM((2, FF_TILE, d), F32),
                        pltpu.SemaphoreType.DMA(()), pltpu.SemaphoreType.DMA((2,))],
    )
    return pl.pallas_call(
        functools.partial(_ffn_kernel, cap=cap, gslots=gslots, per_tile=per_tile, nf=nf),
        grid_spec=grid_spec,
        out_shape=jax.ShapeDtypeStruct((ne, rows + Y_PAD, d), F32),
        compiler_params=_params("arbitrary"),
        name="expert_ffn",
    )(idx_flat, h2p, h2s, w1, w3, w2)


COMB_TILE = 512


COMB_GROUP = 8
STAGE_ROWS = COMB_TILE + Y_PAD


COMB_DEPTH = 8


def _combine_kernel(doff_ref, off_ref, wt_ref, x1_ref, mod_ref, fg_ref, y_hbm, o_ref, acc_ref, stage_ref, sem,
                    *, row0, cap):
    i = pl.program_id(0)
    nspan = pl.num_programs(0) * N_EXPERTS
    sub = TOK_SUB
    ahead = COMB_DEPTH - 1

    def span(p):
        e = lax.bitwise_and(p, N_EXPERTS - 1)
        ti = lax.shift_right_logical(p, N_EXPERTS.bit_length() - 1)
        a = off_ref[e, ti]
        cnt = off_ref[e, ti + 1] - a
        a8 = lax.shift_left(lax.shift_right_logical(a, 3), 3)
        lead = a - a8
        nchunk = lax.shift_right_logical(cnt + lead + (Y_PAD - 1), Y_PAD.bit_length() - 1)
        return e, a, cnt, a8, lead, nchunk

    def fetch(p, start):
        e, _, _, a8, _, nchunk = span(p)
        slot = lax.bitwise_and(p, COMB_DEPTH - 1)

        def body(ci, c):
            src0 = pl.multiple_of(row0 + a8 + ci * Y_PAD, 8)
            dst0 = pl.multiple_of(ci * Y_PAD, Y_PAD)
            for j in range(sub):
                cp = pltpu.make_async_copy(
                    y_hbm.at[e, pl.ds(src0, Y_PAD), pl.ds(j * LANES, LANES)],
                    stage_ref.at[pl.ds(pl.multiple_of(slot * STAGE_ROWS + dst0, Y_PAD), Y_PAD), j, :], sem.at[slot])
                cp.start() if start else cp.wait()
            return c

        lax.fori_loop(0, nchunk, body, 0)

    @pl.when(i == 0)
    def _():
        for p in range(ahead):
            fetch(jnp.int32(p), True)

    acc_ref[...] = jnp.zeros(acc_ref.shape, F32)

    def add_rows(tab0, row0_stage, count):
        dsts = [pl.ds(pl.multiple_of(doff_ref[tab0 + u], sub), sub) for u in range(count)]
        vals = [stage_ref[row0_stage + u] * wt_ref[tab0 + u] for u in range(count)]
        olds = [acc_ref[d, :] for d in dsts]
        for d, old, v in zip(dsts, olds, vals):
            acc_ref[d, :] = old + v

    def per_expert(ei, c):
        p = i * N_EXPERTS + ei

        @pl.when(p + ahead < nspan)
        def _():
            fetch(p + ahead, True)

        fetch(p, False)
        e, a, cnt, _, lead, _ = span(p)
        tab = e * cap + a
        srow = lax.bitwise_and(p, COMB_DEPTH - 1) * STAGE_ROWS + lead
        full = lax.shift_right_logical(cnt, 3)

        def group(g, cc):
            add_rows(tab + g * COMB_GROUP, srow + g * COMB_GROUP, COMB_GROUP)
            return cc

        lax.fori_loop(0, full, group, 0)

        def single(s, cc):
            add_rows(tab + s, srow + s, 1)
            return cc

        lax.fori_loop(full * COMB_GROUP, cnt, single, 0)
        return c

    lax.fori_loop(0, N_EXPERTS, per_expert, 0)
    g2 = mod_ref[0][:, 5 * D_MODEL:6 * D_MODEL]
    acc = jnp.concatenate([acc_ref[pl.ds(j, COMB_TILE, stride=sub), :] for j in range(sub)], axis=1)
    x = x1_ref[...] + g2 * acc
    o_ref[...] = _rms(x, D_MODEL) * fg_ref[...]


def combine(idx, offs, wts, x1, mods3, final_g, y_all, *, row0, mod_row):
    n, d = x1.shape
    assert COMB_GROUP == 8 and COMB_DEPTH & (COMB_DEPTH - 1) == 0
    doff = (idx % COMB_TILE) * TOK_SUB
    grid_spec = pltpu.PrefetchScalarGridSpec(
        num_scalar_prefetch=3,
        grid=(n // COMB_TILE,),
        in_specs=[pl.BlockSpec((COMB_TILE, d), lambda i, *_: (i, 0)),
                  pl.BlockSpec((1, 1, N_MOD * d), lambda i, *_: (mod_row(i), 0, 0)),
                  pl.BlockSpec((1, d), lambda i, *_: (0, 0)),
                  pl.BlockSpec(memory_space=pl.ANY)],
        out_specs=pl.BlockSpec((COMB_TILE, d), lambda i, *_: (i, 0)),
        scratch_shapes=[pltpu.VMEM((COMB_TILE * TOK_SUB, LANES), F32),
                        pltpu.VMEM((COMB_DEPTH * STAGE_ROWS, TOK_SUB, LANES), F32),
                        pltpu.SemaphoreType.DMA((COMB_DEPTH,))],
    )
    return pl.pallas_call(
        functools.partial(_combine_kernel, row0=row0, cap=idx.shape[1]),
        grid_spec=grid_spec,
        out_shape=jax.ShapeDtypeStruct((n, d), F32),
        compiler_params=_params("arbitrary"),
        name="combine",
    )(doff.reshape(-1), offs, wts.reshape(-1), x1, mods3, final_g.reshape(1, d), y_all)


def _tile_offsets(rank, cap):
    total = jnp.full((rank.shape[0], 1), cap, jnp.int32)
    return jnp.concatenate([rank[:, ::COMB_TILE], total], axis=1)


def kernel(x_prompt, x_sample, cache_k, cache_v, c, c_ctx, norm1_g, norm2_g, w_ada, b_ada, w_in, w_out,
           out_g_attn, out_g_hyena, rpb, conv_w, conv_b, filt_w1, filt_b1, filt_w2, filt_b2, filt_w3,
           filt_freq, hyena_bias, w_router, w1, w3, w2, final_g):
    depth = norm1_g.shape[0]
    assert depth == 1
    l = 0
    bp, tp, d = x_prompt.shape
    bs, ts, _ = x_sample.shape

    cond = jnp.zeros((MOD_ROWS, d), F32).at[0].set(c_ctx).at[1:1 + bs].set(c)
    mods3 = ada_mod(cond, w_ada[l], b_ada[l]).reshape(MOD_ROWS, 1, N_MOD * d)

    w_in_b = w_in[l].astype(BF16)
    w_out_b = w_out[l].astype(BF16)
    w_router_t = w_router[l].T
    rpb_ext = jnp.pad(rpb[l], ((0, 0), (0, 0), (RPB_PAD_L, RPB_PAD_R)), mode="edge")
    w1p = jnp.pad(filt_w1[l], ((0, LANES - FILTER_EMB), (0, 0)))

    ctx_row = lambda i: 0
    tt_s = 512
    lat_row = lambda i: 1 + i // (ts // tt_s)

    qkv_p, hy_p, state_k, state_v = in_projection(x_prompt, mods3, norm1_g[l], w_in_b, nb=2, tt=tp,
                                                  mod_row=ctx_row, with_state=True)
    qkv_s, hy_s = in_projection(x_sample, mods3, norm1_g[l], w_in_b, nb=1, tt=tt_s,
                                mod_row=lat_row, with_state=False)

    a_p = context_attention(qkv_p, nb=2)
    a_s = latent_attention(qkv_s, cache_k, cache_v, rpb_ext, l)

    def hyena(hy, length, nb):
        fwd_tab, inv_tab = _dft_tables(length)
        fwd_tab = jnp.asarray(fwd_tab)
        ka, kb = hyena_filter(length, w1p, filt_b1[l], filt_w2[l], filt_b2[l], filt_w3[l], filt_freq[l],
                              fwd_tab)
        return hyena_mixer(hy, conv_w[l], conv_b[l], fwd_tab.astype(BF16), jnp.asarray(inv_tab).astype(BF16),
                           ka, kb, hyena_bias[l], nb=nb)

    yh_p = hyena(hy_p, tp, 8)
    yh_s = hyena(hy_s, ts, 2)

    x1_p, h2_p, aff_p = out_projection(a_p, yh_p, x_prompt, mods3, out_g_attn[l], out_g_hyena[l],
                                       norm2_g[l], w_out_b, w_router_t, nb=2, tt=tp, mod_row=ctx_row)
    x1_s, h2_s, aff_s = out_projection(a_s, yh_s, x_sample, mods3, out_g_attn[l], out_g_hyena[l],
                                       norm2_g[l], w_out_b, w_router_t, nb=1, tt=tt_s, mod_row=lat_row)

    idx_p, wt_p, rank_p = expert_select(aff_p)
    idx_s, wt_s, rank_s = expert_select(aff_s)
    cap = idx_p.shape[1]
    nf = w1.shape[3] // FF_TILE
    gslots = -(-cap // (8 * nf)) * 8 * nf
    pad_to = lambda a, width: jnp.pad(a, ((0, 0), (0, width - a.shape[1])))
    idx_flat = jnp.concatenate([pad_to(idx_p, gslots), pad_to(idx_s, gslots)], axis=0).reshape(-1)
    tiles = lambda h2: h2.reshape(-1, TOK_SUB, LANES)
    y_all = expert_ffn(idx_flat, tiles(h2_p), tiles(h2_s), w1[l], w3[l], w2[l], cap=cap, gslots=gslots)

    comb_row_s = lambda i: 1 + i // (ts // COMB_TILE)
    y_p = combine(idx_p, _tile_offsets(rank_p, cap), wt_p, x1_p, mods3, final_g, y_all, row0=0, mod_row=ctx_row)
    y_s = combine(idx_s, _tile_offsets(rank_s, cap), wt_s, x1_s, mods3, final_g, y_all, row0=cap,
                  mod_row=comb_row_s)

    return (y_p.reshape(bp, tp, d), y_s.reshape(bs, ts, d),
            state_k.reshape(bp, depth, N_HEADS, tp, HEAD_DIM), state_v.reshape(bp, depth, N_HEADS, tp, HEAD_DIM))
```

```python
import functools
import math

import numpy as np
import jax
import jax.numpy as jnp
from jax import lax
from jax.experimental import pallas as pl
from jax.experimental.pallas import tpu as pltpu

D_MODEL = 1024
GRID_W = 64
D_ATTN = 512
D_HYENA = 512
HEAD_DIM = 64
N_HEADS = 8
WIN_ROWS = 8
WIN_COLS = 16
FILTER_EMB = 33
FILTER_BANDS = 16
FILTER_HIDDEN = 64
DECAY_TARGET = 1e-2
SHORT_DECAY_PCT = 0.3
LONG_DECAY_PCT = 1.5
MOD_SHIFT = 0.05
N_EXPERTS = 16
EC_FACTOR = 2
EXPERT_FF = 2816
N_MOD = 6
EPS = 1e-6
NEG_INF = -1e30

LANES = 128
MOD_ROWS = 16
VMEM_LIMIT = 56 * 1024 * 1024
BF16 = jnp.bfloat16
F32 = jnp.float32


def _params(*sem):
    return pltpu.CompilerParams(dimension_semantics=sem, vmem_limit_bytes=VMEM_LIMIT)


def _rms(x, n):
    return x * lax.rsqrt(jnp.sum(x * x, axis=-1, keepdims=True) * (1.0 / n) + EPS)


def _silu(x):
    return x * (1.0 / (1.0 + jnp.exp(-x)))


_NN = (((1,), (0,)), ((), ()))
_NT = (((1,), (1,)), ((), ()))


def _split3(a, b, dims=_NN):
    a_hi = a.astype(BF16)
    a_lo = (a - a_hi.astype(F32)).astype(BF16)
    b_hi = b.astype(BF16)
    b_lo = (b - b_hi.astype(F32)).astype(BF16)
    dot = lambda x, y: lax.dot_general(x, y, dims, preferred_element_type=F32)
    return dot(a_hi, b_hi) + dot(a_hi, b_lo) + dot(a_lo, b_hi)


def _ada_kernel(c_ref, w_ref, b_ref, o_ref):
    s = _silu(c_ref[...])
    o_ref[...] = _split3(s, w_ref[...]) + b_ref[...]


def ada_mod(cond, w_ada, b_ada):
    n = w_ada.shape[1]
    tn = 1024
    return pl.pallas_call(
        _ada_kernel,
        grid=(n // tn,),
        in_specs=[pl.BlockSpec((MOD_ROWS, D_MODEL), lambda j: (0, 0)),
                  pl.BlockSpec((D_MODEL, tn), lambda j: (0, j)),
                  pl.BlockSpec((1, tn), lambda j: (0, j))],
        out_specs=pl.BlockSpec((MOD_ROWS, tn), lambda j: (0, j)),
        out_shape=jax.ShapeDtypeStruct((MOD_ROWS, n), F32),
        compiler_params=_params("arbitrary"),
        name="ada_mod",
    )(cond, w_ada, b_ada.reshape(1, n))


def _inproj_kernel(x_ref, mod_ref, g_ref, w_ref, qkv_ref, hy_ref, *state_refs, nb, tt):
    x = x_ref[...].reshape(nb * tt, D_MODEL)
    mod = mod_ref[0]
    sh1 = mod[:, 0:D_MODEL]
    sc1 = mod[:, D_MODEL:2 * D_MODEL]
    h = _rms(x, D_MODEL) * g_ref[...] * (1.0 + sc1) + sh1
    proj = jnp.dot(h.astype(BF16), w_ref[...], preferred_element_type=F32)
    q = proj[:, 0:D_ATTN] * (HEAD_DIM ** -0.5)
    k = proj[:, D_ATTN:2 * D_ATTN]
    v = proj[:, 2 * D_ATTN:3 * D_ATTN]
    qkv_ref[:, :, 0:D_ATTN] = q.reshape(nb, tt, D_ATTN)
    qkv_ref[:, :, D_ATTN:2 * D_ATTN] = k.reshape(nb, tt, D_ATTN)
    qkv_ref[:, :, 2 * D_ATTN:3 * D_ATTN] = v.reshape(nb, tt, D_ATTN)
    hy_ref[...] = proj[:, 3 * D_ATTN:].reshape(nb, tt, 3 * D_HYENA)
    if state_refs:
        sk_ref, sv_ref = state_refs
        for bi in range(nb):
            for hd in range(N_HEADS):
                sl = slice(hd * HEAD_DIM, (hd + 1) * HEAD_DIM)
                sk_ref[bi, hd] = k[bi * tt:(bi + 1) * tt, sl]
                sv_ref[bi, hd] = v[bi * tt:(bi + 1) * tt, sl]


def in_projection(x, mods3, norm_g, w_in_bf16, *, nb, tt, mod_row, with_state):
    b, t, d = x.shape
    steps_per_b = t // tt
    grid = (b // nb * steps_per_b,)
    xmap = lambda i: (i // steps_per_b, i % steps_per_b, 0)
    nproj = w_in_bf16.shape[1]
    out_shape = [jax.ShapeDtypeStruct((b, t, 3 * D_ATTN), F32),
                 jax.ShapeDtypeStruct((b, t, 3 * D_HYENA), F32)]
    out_specs = [pl.BlockSpec((nb, tt, 3 * D_ATTN), xmap),
                 pl.BlockSpec((nb, tt, 3 * D_HYENA), xmap)]
    if with_state:
        assert steps_per_b == 1
        smap = lambda i: (i, 0, 0, 0)
        for _ in range(2):
            out_shape.append(jax.ShapeDtypeStruct((b, N_HEADS, t, HEAD_DIM), F32))
            out_specs.append(pl.BlockSpec((nb, N_HEADS, tt, HEAD_DIM), smap))
    return pl.pallas_call(
        functools.partial(_inproj_kernel, nb=nb, tt=tt),
        grid=grid,
        in_specs=[pl.BlockSpec((nb, tt, d), xmap),
                  pl.BlockSpec((1, 1, N_MOD * d), lambda i: (mod_row(i), 0, 0)),
                  pl.BlockSpec((1, d), lambda i: (0, 0)),
                  pl.BlockSpec((d, nproj), lambda i: (0, 0))],
        out_specs=out_specs,
        out_shape=out_shape,
        compiler_params=_params("arbitrary"),
        name="in_projection",
    )(x, mods3, norm_g.reshape(1, d), w_in_bf16)


def _lane_mask(shape, half):
    lane = lax.broadcasted_iota(jnp.int32, shape, 1)
    return (lane < HEAD_DIM) if half == 0 else (lane >= HEAD_DIM)


def _with_ones(v):
    return jnp.concatenate([v, jnp.ones(v.shape, v.dtype)], axis=1)


def _normalised(pv):
    return pv[:, 0:LANES] * (1.0 / pv[:, LANES:2 * LANES])


def _ctx_attn_kernel(qkv_ref, o_ref, *, nb):
    t = qkv_ref.shape[1]
    for bi in range(nb):
        for j in range(N_HEADS // 2):
            lo = j * LANES
            qp = qkv_ref[bi, :, lo:lo + LANES]
            kp = qkv_ref[bi, :, D_ATTN + lo:D_ATTN + lo + LANES].astype(BF16)
            vp = _with_ones(qkv_ref[bi, :, 2 * D_ATTN + lo:2 * D_ATTN + lo + LANES].astype(BF16))
            q2 = jnp.concatenate([jnp.where(_lane_mask(qp.shape, hh), qp, 0.0) for hh in range(2)],
                                 axis=0).astype(BF16)
            s = lax.dot_general(q2, kp, _NT, preferred_element_type=F32)
            p = jnp.exp(s - jnp.max(s, axis=-1, keepdims=True))
            o2 = _normalised(jnp.dot(p.astype(BF16), vp, preferred_element_type=F32))
            o_ref[bi, :, lo:lo + LANES] = jnp.where(_lane_mask(qp.shape, 0), o2[0:t], o2[t:2 * t])


def context_attention(qkv, *, nb):
    b, t, _ = qkv.shape
    return pl.pallas_call(
        functools.partial(_ctx_attn_kernel, nb=nb),
        grid=(b // nb,),
        in_specs=[pl.BlockSpec((nb, t, 3 * D_ATTN), lambda i: (i, 0, 0))],
        out_specs=pl.BlockSpec((nb, t, D_ATTN), lambda i: (i, 0, 0)),
        out_shape=jax.ShapeDtypeStruct((b, t, D_ATTN), F32),
        compiler_params=_params("arbitrary"),
        name="context_attention",
    )(qkv)


N_TPAIR = 2 * WIN_ROWS - 2
RPB_PAD_L = GRID_W - WIN_COLS
RPB_PAD_R = LANES - RPB_PAD_L - (2 * WIN_COLS - 1)


def _row_start(r, rows, wr):
    return min(max(r - wr // 2, 0), rows - wr)


def _build_bias(rpb_ref, tp_ref, m_ref, rows, wr):
    shape = (GRID_W, LANES)
    c = lax.broadcasted_iota(jnp.int32, shape, 0)
    lane = lax.broadcasted_iota(jnp.int32, shape, 1)
    kc = lane & (GRID_W - 1)
    ws = jnp.clip(c - WIN_COLS // 2, 0, GRID_W - WIN_COLS)
    col_ok = (kc >= ws) & (kc < ws + WIN_COLS)
    first = lane < GRID_W
    for hh in range(2):
        for dr in range(N_TPAIR):
            xa = jnp.broadcast_to(rpb_ref[hh, dr:dr + 1, :], shape)
            xb = jnp.broadcast_to(rpb_ref[hh, dr + 1:dr + 2, :], shape)
            ta = pltpu.roll(xa, GRID_W + 1, 1, stride=1, stride_axis=0)
            tb = pltpu.roll(xb, 1, 1, stride=1, stride_axis=0)
            tp_ref[hh, dr] = jnp.where(col_ok, jnp.where(first, ta, tb), NEG_INF)
        for r in range(rows):
            rs = _row_start(r, rows, wr)
            for jj in range(rows // 2):
                in_a = rs <= 2 * jj < rs + wr
                in_b = rs <= 2 * jj + 1 < rs + wr
                dr = 2 * jj - r + (WIN_ROWS - 1)
                if in_a and in_b:
                    blk = tp_ref[hh, dr]
                elif in_a:
                    blk = jnp.where(first, tp_ref[hh, dr], NEG_INF)
                elif in_b:
                    blk = jnp.where(first, NEG_INF, tp_ref[hh, dr])
                else:
                    blk = jnp.full(shape, NEG_INF, F32)
                m_ref[hh, r * GRID_W:(r + 1) * GRID_W, jj * LANES:(jj + 1) * LANES] = blk


def _key_range(qb, rows_per_blk, rows, wr):
    r0 = qb * rows_per_blk
    lo = min(_row_start(r, rows, wr) for r in range(r0, r0 + rows_per_blk))
    hi = max(_row_start(r, rows, wr) for r in range(r0, r0 + rows_per_blk)) + wr
    grp = 256 // GRID_W
    return (lo // grp) * 256, -(-hi // grp) * 256


def _lat_attn_kernel(q_ref, k_ref, v_ref, ck_ref, cv_ref, rpb_ref, o_ref, tp_ref, m_ref, *, rows, wr):
    @pl.when(pl.program_id(1) == 0)
    def _():
        _build_bias(rpb_ref, tp_ref, m_ref, rows, wr)

    l_tok = rows * GRID_W
    qblk = 256
    rows_per_blk = qblk // GRID_W
    sel_r = lax.broadcasted_iota(jnp.int32, (HEAD_DIM, LANES), 0)
    sel_c = lax.broadcasted_iota(jnp.int32, (HEAD_DIM, LANES), 1)

    def lane_pair(c_ref):
        parts = [jnp.dot(c_ref[0, 0, hh].astype(BF16), (sel_c == sel_r + hh * HEAD_DIM).astype(BF16),
                         preferred_element_type=F32) for hh in range(2)]
        return (parts[0] + parts[1]).astype(BF16)

    ck = lane_pair(ck_ref)
    cv = _with_ones(lane_pair(cv_ref))
    for qb in range(l_tok // qblk):
        c0, c1 = _key_range(qb, rows_per_blk, rows, wr)
        rs = slice(qb * qblk, (qb + 1) * qblk)
        qp = q_ref[0, rs, :]
        kp = k_ref[0, c0:c1, :].astype(BF16)
        vp = _with_ones(v_ref[0, c0:c1, :].astype(BF16))
        q2 = jnp.concatenate([jnp.where(_lane_mask(qp.shape, hh), qp, 0.0) for hh in range(2)], axis=0).astype(BF16)
        bias = jnp.concatenate([m_ref[hh, rs, c0:c1] for hh in range(2)], axis=0)
        s_win = lax.dot_general(q2, kp, _NT, preferred_element_type=F32) + bias
        s_ctx = lax.dot_general(q2, ck, _NT, preferred_element_type=F32)
        mx = jnp.maximum(jnp.max(s_win, axis=-1, keepdims=True), jnp.max(s_ctx, axis=-1, keepdims=True))
        p_win = jnp.exp(s_win - mx)
        p_ctx = jnp.exp(s_ctx - mx)
        o2 = _normalised(jnp.dot(p_win.astype(BF16), vp, preferred_element_type=F32)
                         + jnp.dot(p_ctx.astype(BF16), cv, preferred_element_type=F32))
        o_ref[0, rs, :] = jnp.where(_lane_mask(qp.shape, 0), o2[0:qblk], o2[qblk:2 * qblk])


def latent_attention(qkv, cache_k, cache_v, rpb_ext, layer):
    b, l_tok, _ = qkv.shape
    rows = l_tok // GRID_W
    wr = min(WIN_ROWS, rows)
    assert rows % 4 == 0 and wr == WIN_ROWS
    npair = N_HEADS // 2
    past = cache_k.shape[3]
    blk = lambda off: pl.BlockSpec((1, l_tok, LANES), lambda j, i: (i, 0, off + j))
    cspec = pl.BlockSpec((1, 1, 2, past, HEAD_DIM), lambda j, i: (i, layer, j, 0, 0))
    return pl.pallas_call(
        functools.partial(_lat_attn_kernel, rows=rows, wr=wr),
        grid=(npair, b),
        in_specs=[blk(0), blk(npair), blk(2 * npair), cspec, cspec,
                  pl.BlockSpec((2, 2 * WIN_ROWS - 1, LANES), lambda j, i: (j, 0, 0))],
        out_specs=pl.BlockSpec((1, l_tok, LANES), lambda j, i: (i, 0, j)),
        out_shape=jax.ShapeDtypeStruct((b, l_tok, D_ATTN), F32),
        scratch_shapes=[pltpu.VMEM((2, N_TPAIR, GRID_W, LANES), F32),
                        pltpu.VMEM((2, l_tok, l_tok), F32)],
        compiler_params=_params("arbitrary", "arbitrary"),
        name="latent_attention",
    )(qkv, qkv, qkv, cache_k, cache_v, rpb_ext)


HY_TC = 256


def _dft_tables(length):
    n2 = 2 * length
    k = np.arange(length, dtype=np.float64)[:, None]
    t = np.arange(length, dtype=np.float64)[None, :]
    ang = 2.0 * np.pi * ((k * t) % n2) / n2
    fa = np.cos(ang)
    fb = -np.sin(ang)
    fb[0, :] = np.cos(np.pi * t[0])
    fwd = np.concatenate([fa, fb], axis=0)
    ga = 2.0 * np.cos(ang).T
    ga[:, 0] = 1.0
    gb = -2.0 * np.sin(ang).T
    gb[:, 0] = np.cos(np.pi * t[0])
    inv = np.concatenate([ga, gb], axis=1)
    return fwd.astype(np.float32), inv.astype(np.float32)


def _filter_consts(length):
    t = np.linspace(0.0, 1.0, length, dtype=np.float32)[:, None]
    w = (np.float32(2.0 * math.pi / length) * np.arange(length, dtype=np.float32))[:, None]
    bands = np.linspace(1e-4, FILTER_BANDS - 1, FILTER_BANDS, dtype=np.float32)[None, :]
    z = np.concatenate([t, np.cos(bands * w), -np.sin(bands * w)], axis=-1).astype(np.float32)
    zp = np.zeros((length, LANES), np.float32)
    zp[:, :FILTER_EMB] = z
    deltas = np.abs(np.linspace(math.log(DECAY_TARGET) / LONG_DECAY_PCT,
                                math.log(DECAY_TARGET) / SHORT_DECAY_PCT, D_HYENA, dtype=np.float32))
    window = (np.exp(-t * deltas) + np.float32(MOD_SHIFT)).astype(np.float32)
    return zp, window


def _filter_kernel(z_ref, win_ref, w1_ref, b1_ref, w2_ref, b2_ref, w3_ref, fr_ref, fwd_ref,
                   ka_ref, kb_ref, s_ref, d_ref, *, length):
    j = pl.program_id(0)

    @pl.when(j == 0)
    def _():
        fr = fr_ref[...]
        hid = jnp.sin(fr * (_split3(z_ref[...], w1_ref[...]) + b1_ref[...]))
        hid = jnp.sin(fr * (_split3(hid, w2_ref[...]) + b2_ref[...]))
        filt = _split3(hid, w3_ref[...])
        nc = 2 * D_HYENA
        win = win_ref[...]
        win2 = jnp.concatenate([win, win], axis=1)
        fw = filt[:, 0:nc] * win2
        row = lax.broadcasted_iota(jnp.int32, (length, nc), 0)
        bw = jnp.where(row == 0, 0.0, filt[:, nc:2 * nc] * win2)
        inv = 1.0 / jnp.sum(jnp.abs(fw) + jnp.abs(bw), axis=0, keepdims=True)
        s_ref[...] = (fw + bw) * inv
        d_ref[...] = (fw - bw) * inv

    fa = fwd_ref[0]
    fb = fwd_ref[1]
    ka = _split3(fa, s_ref[...])
    kb = _split3(fb, d_ref[...])
    nyq = _split3(fb[0:8, :], s_ref[...])[0:1, :]
    row = lax.broadcasted_iota(jnp.int32, kb.shape, 0)
    kb = jnp.where((row == 0) & (j == 0), nyq, kb)
    for c in range(D_HYENA // HY_TC):
        lo, hi = c * HY_TC, (c + 1) * HY_TC
        ka_ref[c] = jnp.concatenate([ka[:, lo:hi], ka[:, D_HYENA + lo:D_HYENA + hi]], axis=1)
        kb_ref[c] = jnp.concatenate([kb[:, lo:hi], kb[:, D_HYENA + lo:D_HYENA + hi]], axis=1)


def hyena_filter(length, w1p, b1, w2, b2, w3, freq, fwd_tab):
    zp, window = _filter_consts(length)
    nc = 2 * D_HYENA
    nct = D_HYENA // HY_TC
    tm = 256
    fwd3 = fwd_tab.reshape(2, length, length)
    full = lambda a: pl.BlockSpec(a.shape, lambda j: (0,) * a.ndim)
    args = [jnp.asarray(zp), jnp.asarray(window), w1p, b1.reshape(1, -1), w2, b2.reshape(1, -1), w3,
            freq.reshape(1, -1)]
    return pl.pallas_call(
        functools.partial(_filter_kernel, length=length),
        grid=(length // tm,),
        in_specs=[full(a) for a in args] + [pl.BlockSpec((2, tm, length), lambda j: (0, j, 0))],
        out_specs=[pl.BlockSpec((nct, tm, 2 * HY_TC), lambda j: (0, j, 0))] * 2,
        out_shape=[jax.ShapeDtypeStruct((nct, length, 2 * HY_TC), F32)] * 2,
        scratch_shapes=[pltpu.VMEM((length, nc), F32), pltpu.VMEM((length, nc), F32)],
        compiler_params=_params("arbitrary"),
        name="hyena_filter",
    )(*args, fwd3)


def _dft_forward(u, f_ref):
    return jnp.dot(f_ref[...], u.astype(BF16), preferred_element_type=F32)


def _spectrum_product(x, ka, kb, length):
    xa = x[0:length]
    xb = x[length:2 * length]
    row = lax.broadcasted_iota(jnp.int32, xa.shape, 0)
    dc = row == 0
    ya = xa * ka - jnp.where(dc, 0.0, xb * kb)
    yb = jnp.where(dc, xb * kb, xa * kb + xb * ka)
    return ya.astype(BF16), yb.astype(BF16)


def _dft_inverse(ya, yb, g_ref, length):
    y = (jnp.dot(g_ref[:, 0:length], ya, preferred_element_type=F32)
         + jnp.dot(g_ref[:, length:2 * length], yb, preferred_element_type=F32))
    return y * (1.0 / (2 * length))


def _hyena_kernel(v_ref, x1_ref, x2_ref, cw_ref, cb_ref, f_ref, g_ref, ka_ref, kb_ref, hb_ref, o_ref,
                  *, length, tc, nb):
    row = lax.broadcasted_iota(jnp.int32, (length, tc), 0)

    def short(u, part):
        up = jnp.where(row == 0, 0.0, pltpu.roll(u, 1, 0))
        dn = jnp.where(row == length - 1, 0.0, pltpu.roll(u, length - 1, 0))
        w = cw_ref[part]
        return up * w[0:1] + u * w[1:2] + dn * w[2:3] + cb_ref[part]

    bs = range(nb)
    u = [short(v_ref[bi], 0) for bi in bs]
    for order in range(2):
        ka = ka_ref[:, order * tc:(order + 1) * tc]
        kb = kb_ref[:, order * tc:(order + 1) * tc]
        spec = [_dft_forward(u[bi], f_ref) for bi in bs]
        spec = [_spectrum_product(x, ka, kb, length) for x in spec]
        conv = [_dft_inverse(ya, yb, g_ref, length) + u[bi] * hb_ref[order] for bi, (ya, yb) in zip(bs, spec)]
        gate_ref = x1_ref if order == 0 else x2_ref
        u = [short(gate_ref[bi], order + 1) * conv[bi] for bi in bs]
    for bi in bs:
        o_ref[bi] = u[bi]


def hyena_mixer(hy, conv_w, conv_b, fwd_bf16, inv_bf16, ka, kb, hy_bias, *, nb):
    b, length, _ = hy.shape
    tc = HY_TC
    nct = D_HYENA // tc
    cw = conv_w.reshape(3, 3, nct, 1, tc).transpose(1, 2, 0, 3, 4).reshape(3, nct, 3, tc)
    cb = conv_b.reshape(3, nct, 1, tc)
    hb = hy_bias.reshape(2, nct, 1, tc).transpose(1, 0, 2, 3)
    part = lambda p: pl.BlockSpec((nb, length, tc), lambda c, i: (i, 0, p * nct + c))
    return pl.pallas_call(
        functools.partial(_hyena_kernel, length=length, tc=tc, nb=nb),
        grid=(nct, b // nb),
        in_specs=[part(0), part(1), part(2),
                  pl.BlockSpec((3, None, 3, tc), lambda c, i: (0, c, 0, 0)),
                  pl.BlockSpec((3, None, 1, tc), lambda c, i: (0, c, 0, 0)),
                  pl.BlockSpec((2 * length, length), lambda c, i: (0, 0)),
                  pl.BlockSpec((length, 2 * length), lambda c, i: (0, 0)),
                  pl.BlockSpec((None, length, 2 * tc), lambda c, i: (c, 0, 0)),
                  pl.BlockSpec((None, length, 2 * tc), lambda c, i: (c, 0, 0)),
                  pl.BlockSpec((None, 2, 1, tc), lambda c, i: (c, 0, 0, 0))],
        out_specs=pl.BlockSpec((nb, length, tc), lambda c, i: (i, 0, c)),
        out_shape=jax.ShapeDtypeStruct((b, length, D_HYENA), F32),
        compiler_params=_params("arbitrary", "arbitrary"),
        name="hyena_mixer",
    )(hy, hy, hy, cw, cb, fwd_bf16, inv_bf16, ka, kb, hb)


def _outproj_kernel(a_ref, y_ref, x_ref, mod_ref, ga_ref, gh_ref, g2_ref, wo_ref, wr_ref,
                    x1_ref, h2_ref, aff_ref, *, nb, tt):
    rows = nb * tt
    a = a_ref[...].reshape(rows, D_ATTN)
    yh = y_ref[...].reshape(rows, D_HYENA)
    x = x_ref[...].reshape(rows, D_MODEL)
    mod = mod_ref[0]
    g1 = mod[:, 2 * D_MODEL:3 * D_MODEL]
    sh2 = mod[:, 3 * D_MODEL:4 * D_MODEL]
    sc2 = mod[:, 4 * D_MODEL:5 * D_MODEL]
    an = (_rms(a, D_ATTN) * ga_ref[...]).astype(BF16)
    yn = (_rms(yh, D_HYENA) * gh_ref[...]).astype(BF16)
    mix = (jnp.dot(an, wo_ref[0:D_ATTN, :], preferred_element_type=F32)
           + jnp.dot(yn, wo_ref[D_ATTN:, :], preferred_element_type=F32))
    x1 = x + g1 * mix
    h2 = _rms(x1, D_MODEL) * g2_ref[...] * (1.0 + sc2) + sh2
    x1_ref[...] = x1
    sub = D_MODEL // LANES
    for j in range(sub):
        h2_ref[pl.ds(j, rows, stride=sub), :] = h2[:, j * LANES:(j + 1) * LANES]
    logits = _split3(wr_ref[...], h2, _NT)
    p = jnp.exp(logits - jnp.max(logits, axis=0, keepdims=True))
    aff_ref[...] = p / jnp.sum(p, axis=0, keepdims=True)


def out_projection(a, yh, x, mods3, out_g_attn, out_g_hyena, norm2_g, w_out_bf16, w_router_t,
                   *, nb, tt, mod_row):
    b, t, d = x.shape
    steps_per_b = t // tt
    n = b * t
    rows = nb * tt
    grid = (n // rows,)
    xmap = lambda i: (i // steps_per_b, i % steps_per_b, 0)
    vec = lambda w: pl.BlockSpec((1, w), lambda i: (0, 0))
    return pl.pallas_call(
        functools.partial(_outproj_kernel, nb=nb, tt=tt),
        grid=grid,
        in_specs=[pl.BlockSpec((nb, tt, D_ATTN), xmap),
                  pl.BlockSpec((nb, tt, D_HYENA), xmap),
                  pl.BlockSpec((nb, tt, d), xmap),
                  pl.BlockSpec((1, 1, N_MOD * d), lambda i: (mod_row(i), 0, 0)),
                  vec(D_ATTN), vec(D_HYENA), vec(d),
                  pl.BlockSpec((d, d), lambda i: (0, 0)),
                  pl.BlockSpec((N_EXPERTS, d), lambda i: (0, 0))],
        out_specs=[pl.BlockSpec((rows, d), lambda i: (i, 0)),
                   pl.BlockSpec((rows * d // LANES, LANES), lambda i: (i, 0)),
                   pl.BlockSpec((N_EXPERTS, rows), lambda i: (0, i))],
        out_shape=[jax.ShapeDtypeStruct((n, d), F32),
                   jax.ShapeDtypeStruct((n * d // LANES, LANES), F32),
                   jax.ShapeDtypeStruct((N_EXPERTS, n), F32)],
        compiler_params=_params("arbitrary"),
        name="out_projection",
    )(a, yh, x, mods3, out_g_attn.reshape(1, -1), out_g_hyena.reshape(1, -1), norm2_g.reshape(1, -1),
      w_out_bf16, w_router_t)


SEL_CHUNK = 256


def _cumsum_excl(mask_ref, out_ref, n):
    i = lax.broadcasted_iota(jnp.int32, (SEL_CHUNK, SEL_CHUNK), 0)
    j = lax.broadcasted_iota(jnp.int32, (SEL_CHUNK, SEL_CHUNK), 1)
    tri = (i < j).astype(BF16)
    carry = jnp.zeros((N_EXPERTS, 1), F32)
    for ch in range(n // SEL_CHUNK):
        sl = slice(ch * SEL_CHUNK, (ch + 1) * SEL_CHUNK)
        m = mask_ref[:, sl]
        out_ref[:, sl] = jnp.dot(m.astype(BF16), tri, preferred_element_type=F32) + carry
        carry = carry + jnp.sum(m, axis=1, keepdims=True)
    return carry


def _rank_kernel(aff_ref, pos_ref, rank_ref, mask_ref, *, n, cap):
    as_float = lambda bits: lax.bitcast_convert_type(bits, F32)

    def bit_step(i, thr):
        cand = thr | (jnp.int32(1) << (30 - i))
        cnt = jnp.sum((aff_ref[...] >= as_float(cand)).astype(jnp.int32), axis=1, keepdims=True)
        return jnp.where(cnt >= cap, cand, thr)

    thr = lax.fori_loop(0, 31, bit_step, jnp.zeros((N_EXPERTS, 1), jnp.int32))
    aff = aff_ref[...]
    gt = aff >= as_float(thr + 1)
    eq = (aff >= as_float(thr)) & jnp.logical_not(gt)
    need = cap - jnp.sum(gt.astype(jnp.int32), axis=1, keepdims=True)
    mask_ref[...] = eq.astype(F32)
    _cumsum_excl(mask_ref, rank_ref, n)
    sel = gt | (eq & (rank_ref[...] < need.astype(F32)))
    mask_ref[...] = sel.astype(F32)
    _cumsum_excl(mask_ref, rank_ref, n)
    pos_ref[...] = jnp.where(sel, rank_ref[...].astype(jnp.int32), -1)


def _invert_kernel(rng_ref, pos_ref, aff_ref, idx_ref, w_ref, part_i_ref, part_w_ref, *, n, cap):
    sblk = LANES
    nblk = cap // sblk
    lane_e = lax.broadcasted_iota(jnp.int32, (cap, N_EXPERTS), 1)
    lane_t = lax.broadcasted_iota(jnp.int32, (sblk, LANES), 1)
    idx_ref[...] = jnp.zeros(idx_ref.shape, jnp.int32)
    w_ref[...] = jnp.zeros(w_ref.shape, F32)
    for e in range(N_EXPERTS):
        def slot_block(sb, carry, e=e):
            s0 = pl.multiple_of(sb * sblk, sblk)
            slot = lax.broadcasted_iota(jnp.int32, (sblk, LANES), 0) + s0

            def chunk(tcn, acc):
                acc_i, acc_w = acc
                t0 = pl.multiple_of(tcn * LANES, LANES)
                hit = pos_ref[e:e + 1, pl.ds(t0, LANES)] == slot
                acc_i = acc_i + jnp.where(hit, lane_t + t0, 0)
                acc_w = acc_w + jnp.where(hit, aff_ref[e:e + 1, pl.ds(t0, LANES)], 0.0)
                return acc_i, acc_w

            acc_i, acc_w = lax.fori_loop(
                rng_ref[e, sb], rng_ref[e, nblk + sb], chunk,
                (jnp.zeros((sblk, LANES), jnp.int32), jnp.zeros((sblk, LANES), F32)))
            part_i_ref[pl.ds(s0, sblk), :] = acc_i
            part_w_ref[pl.ds(s0, sblk), :] = acc_w
            return carry

        lax.fori_loop(0, nblk, slot_block, 0)
        col_i = jnp.sum(part_i_ref[...], axis=1, keepdims=True)
        col_w = jnp.sum(part_w_ref[...], axis=1, keepdims=True)
        idx_ref[...] = jnp.where(lane_e == e, col_i, idx_ref[...])
        w_ref[...] = jnp.where(lane_e == e, col_w, w_ref[...])


def expert_select(aff_t):
    e, n = aff_t.shape
    cap = EC_FACTOR * n // N_EXPERTS
    full = pl.BlockSpec((e, n), lambda i: (0, 0))
    pos, rank = pl.pallas_call(
        functools.partial(_rank_kernel, n=n, cap=cap),
        grid=(1,),
        in_specs=[full],
        out_specs=[full, full],
        out_shape=[jax.ShapeDtypeStruct((e, n), jnp.int32), jax.ShapeDtypeStruct((e, n), F32)],
        scratch_shapes=[pltpu.VMEM((e, n), F32)],
        compiler_params=_params("arbitrary"),
        name="expert_rank",
    )(aff_t)
    rank = rank.astype(jnp.int32)
    total = jnp.full((e, 1), cap, jnp.int32)
    bounds = jnp.concatenate([rank[:, ::LANES], total], axis=1)
    blk0 = jnp.arange(0, cap, LANES, dtype=jnp.int32)[None, :, None]
    first = jnp.sum(bounds[:, None, 1:] <= blk0, axis=2, dtype=jnp.int32)
    last = jnp.sum(bounds[:, None, :-1] < blk0 + LANES, axis=2, dtype=jnp.int32)
    chunk_ranges = jnp.concatenate([jnp.minimum(first, n // LANES - 1), last], axis=1)
    idx_t, w_t = pl.pallas_call(
        functools.partial(_invert_kernel, n=n, cap=cap),
        grid_spec=pltpu.PrefetchScalarGridSpec(
            num_scalar_prefetch=1,
            grid=(1,),
            in_specs=[pl.BlockSpec((e, n), lambda i, cs: (0, 0)), pl.BlockSpec((e, n), lambda i, cs: (0, 0))],
            out_specs=[pl.BlockSpec((cap, e), lambda i, cs: (0, 0)), pl.BlockSpec((cap, e), lambda i, cs: (0, 0))],
            scratch_shapes=[pltpu.VMEM((cap, LANES), jnp.int32), pltpu.VMEM((cap, LANES), F32)]),
        out_shape=[jax.ShapeDtypeStruct((cap, e), jnp.int32), jax.ShapeDtypeStruct((cap, e), F32)],
        compiler_params=_params("arbitrary"),
        name="expert_invert",
    )(chunk_ranges, pos, aff_t)
    return idx_t.T, w_t.T, rank


FF_TILE = 256
FFN_ROWS = 512
Y_PAD = 64


TOK_SUB = D_MODEL // LANES


def _tok_copy(src_hbm, tok, xe_ref, slot, sem):
    return pltpu.make_async_copy(src_hbm.at[tok], xe_ref.at[:, slot, :], sem)


def _ffn_kernel(idx_ref, hp_hbm, hs_hbm, w1_hbm, w3_hbm, w2_hbm, y_ref, xe_ref, xb_ref, w1_buf, w3_buf, w2_buf,
                sem, wsem, *, cap, gslots, per_tile, nf):
    e = pl.program_id(0)
    ne = pl.num_programs(0)
    rows = 2 * cap
    nrc = rows // FFN_ROWS

    def start_pair(ex, s):
        _tok_copy(hp_hbm, idx_ref[ex * gslots + s], xe_ref, s, sem).start()
        _tok_copy(hs_hbm, idx_ref[(ne + ex) * gslots + s], xe_ref, gslots + s, sem).start()

    def wait_all():
        def body(s, c):
            _tok_copy(hp_hbm, 0, xe_ref, s, sem).wait()
            _tok_copy(hs_hbm, 0, xe_ref, gslots + s, sem).wait()
            return c

        lax.fori_loop(0, gslots, body, 0, unroll=8)

    def weight_copies(ex, f, slot):
        col = pl.ds(pl.multiple_of(f * FF_TILE, FF_TILE), FF_TILE)
        return (pltpu.make_async_copy(w1_hbm.at[ex, :, col], w1_buf.at[slot], wsem.at[slot]),
                pltpu.make_async_copy(w3_hbm.at[ex, :, col], w3_buf.at[slot], wsem.at[slot]),
                pltpu.make_async_copy(w2_hbm.at[ex, col, :], w2_buf.at[slot], wsem.at[slot]))

    @pl.when(e == 0)
    def _():
        def body(s, c):
            start_pair(0, s)
            return c

        lax.fori_loop(0, gslots, body, 0, unroll=8)
        for cp in weight_copies(0, 0, 0):
            cp.start()

    wait_all()
    for g in range(2):
        for j in range(TOK_SUB):
            xb_ref[g * cap:(g + 1) * cap, j * LANES:(j + 1) * LANES] = (
                xe_ref[j, g * gslots:g * gslots + cap, :].astype(BF16))
    y_ref[0] = jnp.zeros(y_ref.shape[1:], F32)

    nxt = jnp.minimum(e + 1, ne - 1)

    def ff_tile(f, carry):
        t = e * nf + f
        slot = lax.rem(t, 2)
        last = f == nf - 1
        for cp in weight_copies(jnp.where(last, nxt, e), jnp.where(last, 0, f + 1), 1 - slot):
            cp.start()
        for cp in weight_copies(e, f, slot):
            cp.wait()
        w1b = w1_buf[slot].astype(BF16)
        w3b = w3_buf[slot].astype(BF16)
        w2b = w2_buf[slot].astype(BF16)
        for rc in range(nrc):
            rs = slice(rc * FFN_ROWS, (rc + 1) * FFN_ROWS)
            xb = xb_ref[rs, :]
            h1 = jnp.dot(xb, w1b, preferred_element_type=F32)
            h3 = jnp.dot(xb, w3b, preferred_element_type=F32)
            hid = (_silu(h1) * h3).astype(BF16)
            y_ref[0, rs, :] += jnp.dot(hid, w2b, preferred_element_type=F32)
            for k in range(rc * per_tile // nrc, (rc + 1) * per_tile // nrc):
                start_pair(nxt, f * per_tile + k)
        return carry

    lax.fori_loop(0, nf, ff_tile, 0)

    @pl.when(e == ne - 1)
    def _():
        wait_all()
        for cp in weight_copies(e, 0, lax.rem(ne * nf, 2)):
            cp.wait()


def expert_ffn(idx_flat, h2p, h2s, w1, w3, w2, *, cap, gslots):
    ne, d, ff = w1.shape
    rows = 2 * cap
    nf = ff // FF_TILE
    per_tile = gslots // nf
    assert nf * FF_TILE == ff and per_tile * nf == gslots and gslots >= cap
    any_spec = pl.BlockSpec(memory_space=pl.ANY)
    grid_spec = pltpu.PrefetchScalarGridSpec(
        num_scalar_prefetch=1,
        grid=(ne,),
        in_specs=[any_spec] * 5,
        out_specs=pl.BlockSpec((1, rows + Y_PAD, d), lambda e, idx: (e, 0, 0)),
        scratch_shapes=[pltpu.VMEM((TOK_SUB, 2 * gslots, LANES), F32), pltpu.VMEM((rows, d), BF16),
                        pltpu.VMEM((2, d, FF_TILE), F32), pltpu.VMEM((2, d, FF_TILE), F32),
                        pltpu.VMEM((2, FF_TILE, d), F32),
                        pltpu.SemaphoreType.DMA(()), pltpu.SemaphoreType.DMA((2,))],
    )
    return pl.pallas_call(
        functools.partial(_ffn_kernel, cap=cap, gslots=gslots, per_tile=per_tile, nf=nf),
        grid_spec=grid_spec,
        out_shape=jax.ShapeDtypeStruct((ne, rows + Y_PAD, d), F32),
        compiler_params=_params("arbitrary"),
        name="expert_ffn",
    )(idx_flat, h2p, h2s, w1, w3, w2)


COMB_TILE = 512


COMB_GROUP = 8
STAGE_ROWS = COMB_TILE + Y_PAD


COMB_DEPTH = 8


def _combine_kernel(doff_ref, off_ref, wt_ref, x1_ref, mod_ref, fg_ref, y_hbm, o_ref, acc_ref, stage_ref, sem,
                    *, row0, cap):
    i = pl.program_id(0)
    nspan = pl.num_programs(0) * N_EXPERTS
    sub = TOK_SUB
    ahead = COMB_DEPTH - 1

    def span(p):
        e = lax.bitwise_and(p, N_EXPERTS - 1)
        ti = lax.shift_right_logical(p, N_EXPERTS.bit_length() - 1)
        a = off_ref[e, ti]
        cnt = off_ref[e, ti + 1] - a
        a8 = lax.shift_left(lax.shift_right_logical(a, 3), 3)
        lead = a - a8
        nchunk = lax.shift_right_logical(cnt + lead + (Y_PAD - 1), Y_PAD.bit_length() - 1)
        return e, a, cnt, a8, lead, nchunk

    def fetch(p, start):
        e, _, _, a8, _, nchunk = span(p)
        slot = lax.bitwise_and(p, COMB_DEPTH - 1)

        def body(ci, c):
            src0 = pl.multiple_of(row0 + a8 + ci * Y_PAD, 8)
            dst0 = pl.multiple_of(ci * Y_PAD, Y_PAD)
            for j in range(sub):
                cp = pltpu.make_async_copy(
                    y_hbm.at[e, pl.ds(src0, Y_PAD), pl.ds(j * LANES, LANES)],
                    stage_ref.at[pl.ds(pl.multiple_of(slot * STAGE_ROWS + dst0, Y_PAD), Y_PAD), j, :], sem.at[slot])
                cp.start() if start else cp.wait()
            return c

        lax.fori_loop(0, nchunk, body, 0)

    @pl.when(i == 0)
    def _():
        for p in range(ahead):
            fetch(jnp.int32(p), True)

    acc_ref[...] = jnp.zeros(acc_ref.shape, F32)

    def add_rows(tab0, row0_stage, count):
        dsts = [pl.ds(pl.multiple_of(doff_ref[tab0 + u], sub), sub) for u in range(count)]
        vals = [stage_ref[row0_stage + u] * wt_ref[tab0 + u] for u in range(count)]
        olds = [acc_ref[d, :] for d in dsts]
        for d, old, v in zip(dsts, olds, vals):
            acc_ref[d, :] = old + v

    def per_expert(ei, c):
        p = i * N_EXPERTS + ei

        @pl.when(p + ahead < nspan)
        def _():
            fetch(p + ahead, True)

        fetch(p, False)
        e, a, cnt, _, lead, _ = span(p)
        tab = e * cap + a
        srow = lax.bitwise_and(p, COMB_DEPTH - 1) * STAGE_ROWS + lead
        full = lax.shift_right_logical(cnt, 3)

        def group(g, cc):
            add_rows(tab + g * COMB_GROUP, srow + g * COMB_GROUP, COMB_GROUP)
            return cc

        lax.fori_loop(0, full, group, 0)

        def single(s, cc):
            add_rows(tab + s, srow + s, 1)
            return cc

        lax.fori_loop(full * COMB_GROUP, cnt, single, 0)
        return c

    lax.fori_loop(0, N_EXPERTS, per_expert, 0)
    g2 = mod_ref[0][:, 5 * D_MODEL:6 * D_MODEL]
    acc = jnp.concatenate([acc_ref[pl.ds(j, COMB_TILE, stride=sub), :] for j in range(sub)], axis=1)
    x = x1_ref[...] + g2 * acc
    o_ref[...] = _rms(x, D_MODEL) * fg_ref[...]


def combine(idx, offs, wts, x1, mods3, final_g, y_all, *, row0, mod_row):
    n, d = x1.shape
    assert COMB_GROUP == 8 and COMB_DEPTH & (COMB_DEPTH - 1) == 0
    doff = (idx % COMB_TILE) * TOK_SUB
    grid_spec = pltpu.PrefetchScalarGridSpec(
        num_scalar_prefetch=3,
        grid=(n // COMB_TILE,),
        in_specs=[pl.BlockSpec((COMB_TILE, d), lambda i, *_: (i, 0)),
                  pl.BlockSpec((1, 1, N_MOD * d), lambda i, *_: (mod_row(i), 0, 0)),
                  pl.BlockSpec((1, d), lambda i, *_: (0, 0)),
                  pl.BlockSpec(memory_space=pl.ANY)],
        out_specs=pl.BlockSpec((COMB_TILE, d), lambda i, *_: (i, 0)),
        scratch_shapes=[pltpu.VMEM((COMB_TILE * TOK_SUB, LANES), F32),
                        pltpu.VMEM((COMB_DEPTH * STAGE_ROWS, TOK_SUB, LANES), F32),
                        pltpu.SemaphoreType.DMA((COMB_DEPTH,))],
    )
    return pl.pallas_call(
        functools.partial(_combine_kernel, row0=row0, cap=idx.shape[1]),
        grid_spec=grid_spec,
        out_shape=jax.ShapeDtypeStruct((n, d), F32),
        compiler_params=_params("arbitrary"),
        name="combine",
    )(doff.reshape(-1), offs, wts.reshape(-1), x1, mods3, final_g.reshape(1, d), y_all)


def _tile_offsets(rank, cap):
    total = jnp.full((rank.shape[0], 1), cap, jnp.int32)
    return jnp.concatenate([rank[:, ::COMB_TILE], total], axis=1)


def kernel(x_prompt, x_sample, cache_k, cache_v, c, c_ctx, norm1_g, norm2_g, w_ada, b_ada, w_in, w_out,
           out_g_attn, out_g_hyena, rpb, conv_w, conv_b, filt_w1, filt_b1, filt_w2, filt_b2, filt_w3,
           filt_freq, hyena_bias, w_router, w1, w3, w2, final_g):
    depth = norm1_g.shape[0]
    assert depth == 1
    l = 0
    bp, tp, d = x_prompt.shape
    bs, ts, _ = x_sample.shape

    cond = jnp.zeros((MOD_ROWS, d), F32).at[0].set(c_ctx).at[1:1 + bs].set(c)
    mods3 = ada_mod(cond, w_ada[l], b_ada[l]).reshape(MOD_ROWS, 1, N_MOD * d)

    w_in_b = w_in[l].astype(BF16)
    w_out_b = w_out[l].astype(BF16)
    w_router_t = w_router[l].T
    rpb_ext = jnp.pad(rpb[l], ((0, 0), (0, 0), (RPB_PAD_L, RPB_PAD_R)), mode="edge")
    w1p = jnp.pad(filt_w1[l], ((0, LANES - FILTER_EMB), (0, 0)))

    ctx_row = lambda i: 0
    tt_s = 512
    lat_row = lambda i: 1 + i // (ts // tt_s)

    qkv_p, hy_p, state_k, state_v = in_projection(x_prompt, mods3, norm1_g[l], w_in_b, nb=2, tt=tp,
                                                  mod_row=ctx_row, with_state=True)
    qkv_s, hy_s = in_projection(x_sample, mods3, norm1_g[l], w_in_b, nb=1, tt=tt_s,
                                mod_row=lat_row, with_state=False)

    a_p = context_attention(qkv_p, nb=2)
    a_s = latent_attention(qkv_s, cache_k, cache_v, rpb_ext, l)

    def hyena(hy, length, nb):
        fwd_tab, inv_tab = _dft_tables(length)
        fwd_tab = jnp.asarray(fwd_tab)
        ka, kb = hyena_filter(length, w1p, filt_b1[l], filt_w2[l], filt_b2[l], filt_w3[l], filt_freq[l],
                              fwd_tab)
        return hyena_mixer(hy, conv_w[l], conv_b[l], fwd_tab.astype(BF16), jnp.asarray(inv_tab).astype(BF16),
                           ka, kb, hyena_bias[l], nb=nb)

    yh_p = hyena(hy_p, tp, 8)
    yh_s = hyena(hy_s, ts, 2)

    x1_p, h2_p, aff_p = out_projection(a_p, yh_p, x_prompt, mods3, out_g_attn[l], out_g_hyena[l],
                                       norm2_g[l], w_out_b, w_router_t, nb=2, tt=tp, mod_row=ctx_row)
    x1_s, h2_s, aff_s = out_projection(a_s, yh_s, x_sample, mods3, out_g_attn[l], out_g_hyena[l],
                                       norm2_g[l], w_out_b, w_router_t, nb=1, tt=tt_s, mod_row=lat_row)

    idx_p, wt_p, rank_p = expert_select(aff_p)
    idx_s, wt_s, rank_s = expert_select(aff_s)
    cap = idx_p.shape[1]
    nf = w1.shape[3] // FF_TILE
    gslots = -(-cap // (8 * nf)) * 8 * nf
    pad_to = lambda a, width: jnp.pad(a, ((0, 0), (0, width - a.shape[1])))
    idx_flat = jnp.concatenate([pad_to(idx_p, gslots), pad_to(idx_s, gslots)], axis=0).reshape(-1)
    tiles = lambda h2: h2.reshape(-1, TOK_SUB, LANES)
    y_all = expert_ffn(idx_flat, tiles(h2_p), tiles(h2_s), w1[l], w3[l], w2[l], cap=cap, gslots=gslots)

    comb_row_s = lambda i: 1 + i // (ts // COMB_TILE)
    y_p = combine(idx_p, _tile_offsets(rank_p, cap), wt_p, x1_p, mods3, final_g, y_all, row0=0, mod_row=ctx_row)
    y_s = combine(idx_s, _tile_offsets(rank_s, cap), wt_s, x1_s, mods3, final_g, y_all, row0=cap,
                  mod_row=comb_row_s)

    return (y_p.reshape(bp, tp, d), y_s.reshape(bs, ts, d),
            state_k.reshape(bp, depth, N_HEADS, tp, HEAD_DIM), state_v.reshape(bp, depth, N_HEADS, tp, HEAD_DIM))
```

```python
import functools
import math

import numpy as np
import jax
import jax.numpy as jnp
from jax import lax
from jax.experimental import pallas as pl
from jax.experimental.pallas import tpu as pltpu

D_MODEL = 1024
GRID_W = 64
D_ATTN = 512
D_HYENA = 512
HEAD_DIM = 64
N_HEADS = 8
WIN_ROWS = 8
WIN_COLS = 16
FILTER_EMB = 33
FILTER_BANDS = 16
FILTER_HIDDEN = 64
DECAY_TARGET = 1e-2
SHORT_DECAY_PCT = 0.3
LONG_DECAY_PCT = 1.5
MOD_SHIFT = 0.05
N_EXPERTS = 16
EC_FACTOR = 2
EXPERT_FF = 2816
N_MOD = 6
EPS = 1e-6
NEG_INF = -1e30

LANES = 128
MOD_ROWS = 16
VMEM_LIMIT = 56 * 1024 * 1024
BF16 = jnp.bfloat16
F32 = jnp.float32


def _params(*sem):
    return pltpu.CompilerParams(dimension_semantics=sem, vmem_limit_bytes=VMEM_LIMIT)


def _rms(x, n):
    return x * lax.rsqrt(jnp.sum(x * x, axis=-1, keepdims=True) * (1.0 / n) + EPS)


def _silu(x):
    return x * (1.0 / (1.0 + jnp.exp(-x)))


_NN = (((1,), (0,)), ((), ()))
_NT = (((1,), (1,)), ((), ()))


def _split3(a, b, dims=_NN):
    a_hi = a.astype(BF16)
    a_lo = (a - a_hi.astype(F32)).astype(BF16)
    b_hi = b.astype(BF16)
    b_lo = (b - b_hi.astype(F32)).astype(BF16)
    dot = lambda x, y: lax.dot_general(x, y, dims, preferred_element_type=F32)
    return dot(a_hi, b_hi) + dot(a_hi, b_lo) + dot(a_lo, b_hi)


def _ada_kernel(c_ref, w_ref, b_ref, o_ref):
    s = _silu(c_ref[...])
    o_ref[...] = _split3(s, w_ref[...]) + b_ref[...]


def ada_mod(cond, w_ada, b_ada):
    n = w_ada.shape[1]
    tn = 1024
    return pl.pallas_call(
        _ada_kernel,
        grid=(n // tn,),
        in_specs=[pl.BlockSpec((MOD_ROWS, D_MODEL), lambda j: (0, 0)),
                  pl.BlockSpec((D_MODEL, tn), lambda j: (0, j)),
                  pl.BlockSpec((1, tn), lambda j: (0, j))],
        out_specs=pl.BlockSpec((MOD_ROWS, tn), lambda j: (0, j)),
        out_shape=jax.ShapeDtypeStruct((MOD_ROWS, n), F32),
        compiler_params=_params("arbitrary"),
        name="ada_mod",
    )(cond, w_ada, b_ada.reshape(1, n))


def _inproj_kernel(x_ref, mod_ref, g_ref, w_ref, qkv_ref, hy_ref, *state_refs, nb, tt):
    x = x_ref[...].reshape(nb * tt, D_MODEL)
    mod = mod_ref[0]
    sh1 = mod[:, 0:D_MODEL]
    sc1 = mod[:, D_MODEL:2 * D_MODEL]
    h = _rms(x, D_MODEL) * g_ref[...] * (1.0 + sc1) + sh1
    proj = jnp.dot(h.astype(BF16), w_ref[...], preferred_element_type=F32)
    q = proj[:, 0:D_ATTN] * (HEAD_DIM ** -0.5)
    k = proj[:, D_ATTN:2 * D_ATTN]
    v = proj[:, 2 * D_ATTN:3 * D_ATTN]
    qkv_ref[:, :, 0:D_ATTN] = q.astype(BF16).reshape(nb, tt, D_ATTN)
    qkv_ref[:, :, D_ATTN:2 * D_ATTN] = k.astype(BF16).reshape(nb, tt, D_ATTN)
    qkv_ref[:, :, 2 * D_ATTN:3 * D_ATTN] = v.astype(BF16).reshape(nb, tt, D_ATTN)
    hy_ref[...] = proj[:, 3 * D_ATTN:].reshape(nb, tt, 3 * D_HYENA)
    if state_refs:
        sk_ref, sv_ref = state_refs
        for bi in range(nb):
            for hd in range(N_HEADS):
                sl = slice(hd * HEAD_DIM, (hd + 1) * HEAD_DIM)
                sk_ref[bi, hd] = k[bi * tt:(bi + 1) * tt, sl]
                sv_ref[bi, hd] = v[bi * tt:(bi + 1) * tt, sl]


def in_projection(x, mods3, norm_g, w_in_bf16, *, nb, tt, mod_row, with_state):
    b, t, d = x.shape
    steps_per_b = t // tt
    grid = (b // nb * steps_per_b,)
    xmap = lambda i: (i // steps_per_b, i % steps_per_b, 0)
    nproj = w_in_bf16.shape[1]
    out_shape = [jax.ShapeDtypeStruct((b, t, 3 * D_ATTN), BF16),
                 jax.ShapeDtypeStruct((b, t, 3 * D_HYENA), F32)]
    out_specs = [pl.BlockSpec((nb, tt, 3 * D_ATTN), xmap),
                 pl.BlockSpec((nb, tt, 3 * D_HYENA), xmap)]
    if with_state:
        assert steps_per_b == 1
        smap = lambda i: (i, 0, 0, 0)
        for _ in range(2):
            out_shape.append(jax.ShapeDtypeStruct((b, N_HEADS, t, HEAD_DIM), F32))
            out_specs.append(pl.BlockSpec((nb, N_HEADS, tt, HEAD_DIM), smap))
    return pl.pallas_call(
        functools.partial(_inproj_kernel, nb=nb, tt=tt),
        grid=grid,
        in_specs=[pl.BlockSpec((nb, tt, d), xmap),
                  pl.BlockSpec((1, 1, N_MOD * d), lambda i: (mod_row(i), 0, 0)),
                  pl.BlockSpec((1, d), lambda i: (0, 0)),
                  pl.BlockSpec((d, nproj), lambda i: (0, 0))],
        out_specs=out_specs,
        out_shape=out_shape,
        compiler_params=_params("arbitrary"),
        name="in_projection",
    )(x, mods3, norm_g.reshape(1, d), w_in_bf16)


def _lane_mask(shape, half):
    lane = lax.broadcasted_iota(jnp.int32, shape, 1)
    return (lane < HEAD_DIM) if half == 0 else (lane >= HEAD_DIM)


def _with_ones(v):
    return jnp.concatenate([v, jnp.ones(v.shape, v.dtype)], axis=1)


def _normalised(pv):
    return pv[:, 0:LANES] * (1.0 / pv[:, LANES:2 * LANES])


def _ctx_attn_kernel(qkv_ref, o_ref, *, nb):
    t = qkv_ref.shape[1]
    for bi in range(nb):
        for j in range(N_HEADS // 2):
            lo = j * LANES
            qp = qkv_ref[bi, :, lo:lo + LANES]
            kp = qkv_ref[bi, :, D_ATTN + lo:D_ATTN + lo + LANES].astype(BF16)
            vp = _with_ones(qkv_ref[bi, :, 2 * D_ATTN + lo:2 * D_ATTN + lo + LANES].astype(BF16))
            q2 = jnp.concatenate([jnp.where(_lane_mask(qp.shape, hh), qp, 0.0) for hh in range(2)],
                                 axis=0).astype(BF16)
            s = lax.dot_general(q2, kp, _NT, preferred_element_type=F32)
            p = jnp.exp(s - jnp.max(s, axis=-1, keepdims=True))
            o2 = _normalised(jnp.dot(p.astype(BF16), vp, preferred_element_type=F32))
            o_ref[bi, :, lo:lo + LANES] = jnp.where(_lane_mask(qp.shape, 0), o2[0:t], o2[t:2 * t])


def context_attention(qkv, *, nb):
    b, t, _ = qkv.shape
    return pl.pallas_call(
        functools.partial(_ctx_attn_kernel, nb=nb),
        grid=(b // nb,),
        in_specs=[pl.BlockSpec((nb, t, 3 * D_ATTN), lambda i: (i, 0, 0))],
        out_specs=pl.BlockSpec((nb, t, D_ATTN), lambda i: (i, 0, 0)),
        out_shape=jax.ShapeDtypeStruct((b, t, D_ATTN), F32),
        compiler_params=_params("arbitrary"),
        name="context_attention",
    )(qkv)


N_TPAIR = 2 * WIN_ROWS - 2
RPB_PAD_L = GRID_W - WIN_COLS
RPB_PAD_R = LANES - RPB_PAD_L - (2 * WIN_COLS - 1)


def _row_start(r, rows, wr):
    return min(max(r - wr // 2, 0), rows - wr)


def _build_bias(rpb_ref, tp_ref, m_ref, rows, wr):
    shape = (GRID_W, LANES)
    c = lax.broadcasted_iota(jnp.int32, shape, 0)
    lane = lax.broadcasted_iota(jnp.int32, shape, 1)
    kc = lane & (GRID_W - 1)
    ws = jnp.clip(c - WIN_COLS // 2, 0, GRID_W - WIN_COLS)
    col_ok = (kc >= ws) & (kc < ws + WIN_COLS)
    first = lane < GRID_W
    for hh in range(2):
        for dr in range(N_TPAIR):
            xa = jnp.broadcast_to(rpb_ref[hh, dr:dr + 1, :], shape)
            xb = jnp.broadcast_to(rpb_ref[hh, dr + 1:dr + 2, :], shape)
            ta = pltpu.roll(xa, GRID_W + 1, 1, stride=1, stride_axis=0)
            tb = pltpu.roll(xb, 1, 1, stride=1, stride_axis=0)
            tp_ref[hh, dr] = jnp.where(col_ok, jnp.where(first, ta, tb), NEG_INF)
        for r in range(rows):
            rs = _row_start(r, rows, wr)
            for jj in range(rows // 2):
                in_a = rs <= 2 * jj < rs + wr
                in_b = rs <= 2 * jj + 1 < rs + wr
                dr = 2 * jj - r + (WIN_ROWS - 1)
                if in_a and in_b:
                    blk = tp_ref[hh, dr]
                elif in_a:
                    blk = jnp.where(first, tp_ref[hh, dr], NEG_INF)
                elif in_b:
                    blk = jnp.where(first, NEG_INF, tp_ref[hh, dr])
                else:
                    blk = jnp.full(shape, NEG_INF, F32)
                m_ref[hh, r * GRID_W:(r + 1) * GRID_W, jj * LANES:(jj + 1) * LANES] = blk


def _key_range(qb, rows_per_blk, rows, wr):
    r0 = qb * rows_per_blk
    lo = min(_row_start(r, rows, wr) for r in range(r0, r0 + rows_per_blk))
    hi = max(_row_start(r, rows, wr) for r in range(r0, r0 + rows_per_blk)) + wr
    grp = 256 // GRID_W
    return (lo // grp) * 256, -(-hi // grp) * 256


def _lat_attn_kernel(q_ref, k_ref, v_ref, ck_ref, cv_ref, rpb_ref, o_ref, tp_ref, m_ref, *, rows, wr):
    @pl.when(pl.program_id(1) == 0)
    def _():
        _build_bias(rpb_ref, tp_ref, m_ref, rows, wr)

    l_tok = rows * GRID_W
    qblk = 256
    rows_per_blk = qblk // GRID_W
    sel_r = lax.broadcasted_iota(jnp.int32, (HEAD_DIM, LANES), 0)
    sel_c = lax.broadcasted_iota(jnp.int32, (HEAD_DIM, LANES), 1)

    def lane_pair(c_ref):
        parts = [jnp.dot(c_ref[0, 0, hh].astype(BF16), (sel_c == sel_r + hh * HEAD_DIM).astype(BF16),
                         preferred_element_type=F32) for hh in range(2)]
        return (parts[0] + parts[1]).astype(BF16)

    ck = lane_pair(ck_ref)
    cv = _with_ones(lane_pair(cv_ref))
    for qb in range(l_tok // qblk):
        c0, c1 = _key_range(qb, rows_per_blk, rows, wr)
        rs = slice(qb * qblk, (qb + 1) * qblk)
        qp = q_ref[0, rs, :]
        kp = k_ref[0, c0:c1, :].astype(BF16)
        vp = _with_ones(v_ref[0, c0:c1, :].astype(BF16))
        q2 = jnp.concatenate([jnp.where(_lane_mask(qp.shape, hh), qp, 0.0) for hh in range(2)], axis=0).astype(BF16)
        bias = jnp.concatenate([m_ref[hh, rs, c0:c1] for hh in range(2)], axis=0)
        s_win = lax.dot_general(q2, kp, _NT, preferred_element_type=F32) + bias
        s_ctx = lax.dot_general(q2, ck, _NT, preferred_element_type=F32)
        mx = jnp.maximum(jnp.max(s_win, axis=-1, keepdims=True), jnp.max(s_ctx, axis=-1, keepdims=True))
        p_win = jnp.exp(s_win - mx)
        p_ctx = jnp.exp(s_ctx - mx)
        o2 = _normalised(jnp.dot(p_win.astype(BF16), vp, preferred_element_type=F32)
                         + jnp.dot(p_ctx.astype(BF16), cv, preferred_element_type=F32))
        o_ref[0, rs, :] = jnp.where(_lane_mask(qp.shape, 0), o2[0:qblk], o2[qblk:2 * qblk])


def latent_attention(qkv, cache_k, cache_v, rpb_ext, layer):
    b, l_tok, _ = qkv.shape
    rows = l_tok // GRID_W
    wr = min(WIN_ROWS, rows)
    assert rows % 4 == 0 and wr == WIN_ROWS
    npair = N_HEADS // 2
    past = cache_k.shape[3]
    blk = lambda off: pl.BlockSpec((1, l_tok, LANES), lambda j, i: (i, 0, off + j))
    cspec = pl.BlockSpec((1, 1, 2, past, HEAD_DIM), lambda j, i: (i, layer, j, 0, 0))
    return pl.pallas_call(
        functools.partial(_lat_attn_kernel, rows=rows, wr=wr),
        grid=(npair, b),
        in_specs=[blk(0), blk(npair), blk(2 * npair), cspec, cspec,
                  pl.BlockSpec((2, 2 * WIN_ROWS - 1, LANES), lambda j, i: (j, 0, 0))],
        out_specs=pl.BlockSpec((1, l_tok, LANES), lambda j, i: (i, 0, j)),
        out_shape=jax.ShapeDtypeStruct((b, l_tok, D_ATTN), F32),
        scratch_shapes=[pltpu.VMEM((2, N_TPAIR, GRID_W, LANES), F32),
                        pltpu.VMEM((2, l_tok, l_tok), F32)],
        compiler_params=_params("arbitrary", "arbitrary"),
        name="latent_attention",
    )(qkv, qkv, qkv, cache_k, cache_v, rpb_ext)


HY_TC = 256


def _dft_tables(length):
    n2 = 2 * length
    k = np.arange(length, dtype=np.float64)[:, None]
    t = np.arange(length, dtype=np.float64)[None, :]
    ang = 2.0 * np.pi * ((k * t) % n2) / n2
    fa = np.cos(ang)
    fb = -np.sin(ang)
    fb[0, :] = np.cos(np.pi * t[0])
    fwd = np.concatenate([fa, fb], axis=0)
    ga = 2.0 * np.cos(ang).T
    ga[:, 0] = 1.0
    gb = -2.0 * np.sin(ang).T
    gb[:, 0] = np.cos(np.pi * t[0])
    inv = np.concatenate([ga, gb], axis=1)
    return fwd.astype(np.float32), inv.astype(np.float32)


def _filter_consts(length):
    t = np.linspace(0.0, 1.0, length, dtype=np.float32)[:, None]
    w = (np.float32(2.0 * math.pi / length) * np.arange(length, dtype=np.float32))[:, None]
    bands = np.linspace(1e-4, FILTER_BANDS - 1, FILTER_BANDS, dtype=np.float32)[None, :]
    z = np.concatenate([t, np.cos(bands * w), -np.sin(bands * w)], axis=-1).astype(np.float32)
    zp = np.zeros((length, LANES), np.float32)
    zp[:, :FILTER_EMB] = z
    deltas = np.abs(np.linspace(math.log(DECAY_TARGET) / LONG_DECAY_PCT,
                                math.log(DECAY_TARGET) / SHORT_DECAY_PCT, D_HYENA, dtype=np.float32))
    window = (np.exp(-t * deltas) + np.float32(MOD_SHIFT)).astype(np.float32)
    return zp, window


def _filter_kernel(z_ref, win_ref, w1_ref, b1_ref, w2_ref, b2_ref, w3_ref, fr_ref, fwd_ref,
                   ka_ref, kb_ref, s_ref, d_ref, *, length):
    j = pl.program_id(0)

    @pl.when(j == 0)
    def _():
        fr = fr_ref[...]
        hid = jnp.sin(fr * (_split3(z_ref[...], w1_ref[...]) + b1_ref[...]))
        hid = jnp.sin(fr * (_split3(hid, w2_ref[...]) + b2_ref[...]))
        filt = _split3(hid, w3_ref[...])
        nc = 2 * D_HYENA
        win = win_ref[...]
        win2 = jnp.concatenate([win, win], axis=1)
        fw = filt[:, 0:nc] * win2
        row = lax.broadcasted_iota(jnp.int32, (length, nc), 0)
        bw = jnp.where(row == 0, 0.0, filt[:, nc:2 * nc] * win2)
        inv = 1.0 / jnp.sum(jnp.abs(fw) + jnp.abs(bw), axis=0, keepdims=True)
        s_ref[...] = (fw + bw) * inv
        d_ref[...] = (fw - bw) * inv

    fa = fwd_ref[0]
    fb = fwd_ref[1]
    ka = _split3(fa, s_ref[...])
    kb = _split3(fb, d_ref[...])
    nyq = _split3(fb[0:8, :], s_ref[...])[0:1, :]
    row = lax.broadcasted_iota(jnp.int32, kb.shape, 0)
    kb = jnp.where((row == 0) & (j == 0), nyq, kb)
    for c in range(D_HYENA // HY_TC):
        lo, hi = c * HY_TC, (c + 1) * HY_TC
        ka_ref[c] = jnp.concatenate([ka[:, lo:hi], ka[:, D_HYENA + lo:D_HYENA + hi]], axis=1)
        kb_ref[c] = jnp.concatenate([kb[:, lo:hi], kb[:, D_HYENA + lo:D_HYENA + hi]], axis=1)


def hyena_filter(length, w1p, b1, w2, b2, w3, freq, fwd_tab):
    zp, window = _filter_consts(length)
    nc = 2 * D_HYENA
    nct = D_HYENA // HY_TC
    tm = 256
    fwd3 = fwd_tab.reshape(2, length, length)
    full = lambda a: pl.BlockSpec(a.shape, lambda j: (0,) * a.ndim)
    args = [jnp.asarray(zp), jnp.asarray(window), w1p, b1.reshape(1, -1), w2, b2.reshape(1, -1), w3,
            freq.reshape(1, -1)]
    return pl.pallas_call(
        functools.partial(_filter_kernel, length=length),
        grid=(length // tm,),
        in_specs=[full(a) for a in args] + [pl.BlockSpec((2, tm, length), lambda j: (0, j, 0))],
        out_specs=[pl.BlockSpec((nct, tm, 2 * HY_TC), lambda j: (0, j, 0))] * 2,
        out_shape=[jax.ShapeDtypeStruct((nct, length, 2 * HY_TC), F32)] * 2,
        scratch_shapes=[pltpu.VMEM((length, nc), F32), pltpu.VMEM((length, nc), F32)],
        compiler_params=_params("arbitrary"),
        name="hyena_filter",
    )(*args, fwd3)


def _dft_forward(u, f_ref):
    return jnp.dot(f_ref[...], u.astype(BF16), preferred_element_type=F32)


def _spectrum_product(x, ka, kb, length):
    xa = x[0:length]
    xb = x[length:2 * length]
    row = lax.broadcasted_iota(jnp.int32, xa.shape, 0)
    dc = row == 0
    ya = xa * ka - jnp.where(dc, 0.0, xb * kb)
    yb = jnp.where(dc, xb * kb, xa * kb + xb * ka)
    return ya.astype(BF16), yb.astype(BF16)


def _dft_inverse(ya, yb, g_ref, length):
    y = (jnp.dot(g_ref[:, 0:length], ya, preferred_element_type=F32)
         + jnp.dot(g_ref[:, length:2 * length], yb, preferred_element_type=F32))
    return y * (1.0 / (2 * length))


def _hyena_kernel(v_ref, x1_ref, x2_ref, cw_ref, cb_ref, f_ref, g_ref, ka_ref, kb_ref, hb_ref, o_ref,
                  *, length, tc, nb):
    row = lax.broadcasted_iota(jnp.int32, (length, tc), 0)

    def short(u, part):
        up = jnp.where(row == 0, 0.0, pltpu.roll(u, 1, 0))
        dn = jnp.where(row == length - 1, 0.0, pltpu.roll(u, length - 1, 0))
        w = cw_ref[part]
        return up * w[0:1] + u * w[1:2] + dn * w[2:3] + cb_ref[part]

    bs = range(nb)
    u = [short(v_ref[bi], 0) for bi in bs]
    for order in range(2):
        ka = ka_ref[:, order * tc:(order + 1) * tc]
        kb = kb_ref[:, order * tc:(order + 1) * tc]
        spec = [_dft_forward(u[bi], f_ref) for bi in bs]
        spec = [_spectrum_product(x, ka, kb, length) for x in spec]
        conv = [_dft_inverse(ya, yb, g_ref, length) + u[bi] * hb_ref[order] for bi, (ya, yb) in zip(bs, spec)]
        gate_ref = x1_ref if order == 0 else x2_ref
        u = [short(gate_ref[bi], order + 1) * conv[bi] for bi in bs]
    for bi in bs:
        o_ref[bi] = u[bi]


def hyena_mixer(hy, conv_w, conv_b, fwd_bf16, inv_bf16, ka, kb, hy_bias, *, nb):
    b, length, _ = hy.shape
    tc = HY_TC
    nct = D_HYENA // tc
    cw = conv_w.reshape(3, 3, nct, 1, tc).transpose(1, 2, 0, 3, 4).reshape(3, nct, 3, tc)
    cb = conv_b.reshape(3, nct, 1, tc)
    hb = hy_bias.reshape(2, nct, 1, tc).transpose(1, 0, 2, 3)
    part = lambda p: pl.BlockSpec((nb, length, tc), lambda c, i: (i, 0, p * nct + c))
    return pl.pallas_call(
        functools.partial(_hyena_kernel, length=length, tc=tc, nb=nb),
        grid=(nct, b // nb),
        in_specs=[part(0), part(1), part(2),
                  pl.BlockSpec((3, None, 3, tc), lambda c, i: (0, c, 0, 0)),
                  pl.BlockSpec((3, None, 1, tc), lambda c, i: (0, c, 0, 0)),
                  pl.BlockSpec((2 * length, length), lambda c, i: (0, 0)),
                  pl.BlockSpec((length, 2 * length), lambda c, i: (0, 0)),
                  pl.BlockSpec((None, length, 2 * tc), lambda c, i: (c, 0, 0)),
                  pl.BlockSpec((None, length, 2 * tc), lambda c, i: (c, 0, 0)),
                  pl.BlockSpec((None, 2, 1, tc), lambda c, i: (c, 0, 0, 0))],
        out_specs=pl.BlockSpec((nb, length, tc), lambda c, i: (i, 0, c)),
        out_shape=jax.ShapeDtypeStruct((b, length, D_HYENA), F32),
        compiler_params=_params("arbitrary", "arbitrary"),
        name="hyena_mixer",
    )(hy, hy, hy, cw, cb, fwd_bf16, inv_bf16, ka, kb, hb)


def _outproj_kernel(a_ref, y_ref, x_ref, mod_ref, ga_ref, gh_ref, g2_ref, wo_ref, wr_ref,
                    x1_ref, h2_ref, aff_ref, *, nb, tt):
    rows = nb * tt
    a = a_ref[...].reshape(rows, D_ATTN)
    yh = y_ref[...].reshape(rows, D_HYENA)
    x = x_ref[...].reshape(rows, D_MODEL)
    mod = mod_ref[0]
    g1 = mod[:, 2 * D_MODEL:3 * D_MODEL]
    sh2 = mod[:, 3 * D_MODEL:4 * D_MODEL]
    sc2 = mod[:, 4 * D_MODEL:5 * D_MODEL]
    an = (_rms(a, D_ATTN) * ga_ref[...]).astype(BF16)
    yn = (_rms(yh, D_HYENA) * gh_ref[...]).astype(BF16)
    mix = (jnp.dot(an, wo_ref[0:D_ATTN, :], preferred_element_type=F32)
           + jnp.dot(yn, wo_ref[D_ATTN:, :], preferred_element_type=F32))
    x1 = x + g1 * mix
    h2 = _rms(x1, D_MODEL) * g2_ref[...] * (1.0 + sc2) + sh2
    x1_ref[...] = x1
    sub = D_MODEL // LANES
    for j in range(sub):
        h2_ref[pl.ds(j, rows, stride=sub), :] = h2[:, j * LANES:(j + 1) * LANES]
    logits = _split3(wr_ref[...], h2, _NT)
    p = jnp.exp(logits - jnp.max(logits, axis=0, keepdims=True))
    aff_ref[...] = p / jnp.sum(p, axis=0, keepdims=True)


def out_projection(a, yh, x, mods3, out_g_attn, out_g_hyena, norm2_g, w_out_bf16, w_router_t,
                   *, nb, tt, mod_row):
    b, t, d = x.shape
    steps_per_b = t // tt
    n = b * t
    rows = nb * tt
    grid = (n // rows,)
    xmap = lambda i: (i // steps_per_b, i % steps_per_b, 0)
    vec = lambda w: pl.BlockSpec((1, w), lambda i: (0, 0))
    return pl.pallas_call(
        functools.partial(_outproj_kernel, nb=nb, tt=tt),
        grid=grid,
        in_specs=[pl.BlockSpec((nb, tt, D_ATTN), xmap),
                  pl.BlockSpec((nb, tt, D_HYENA), xmap),
                  pl.BlockSpec((nb, tt, d), xmap),
                  pl.BlockSpec((1, 1, N_MOD * d), lambda i: (mod_row(i), 0, 0)),
                  vec(D_ATTN), vec(D_HYENA), vec(d),
                  pl.BlockSpec((d, d), lambda i: (0, 0)),
                  pl.BlockSpec((N_EXPERTS, d), lambda i: (0, 0))],
        out_specs=[pl.BlockSpec((rows, d), lambda i: (i, 0)),
                   pl.BlockSpec((rows * d // LANES, LANES), lambda i: (i, 0)),
                   pl.BlockSpec((N_EXPERTS, rows), lambda i: (0, i))],
        out_shape=[jax.ShapeDtypeStruct((n, d), F32),
                   jax.ShapeDtypeStruct((n * d // LANES, LANES), F32),
                   jax.ShapeDtypeStruct((N_EXPERTS, n), F32)],
        compiler_params=_params("arbitrary"),
        name="out_projection",
    )(a, yh, x, mods3, out_g_attn.reshape(1, -1), out_g_hyena.reshape(1, -1), norm2_g.reshape(1, -1),
      w_out_bf16, w_router_t)


SEL_CHUNK = 256


def _cumsum_excl(mask_ref, out_ref, n):
    i = lax.broadcasted_iota(jnp.int32, (SEL_CHUNK, SEL_CHUNK), 0)
    j = lax.broadcasted_iota(jnp.int32, (SEL_CHUNK, SEL_CHUNK), 1)
    tri = (i < j).astype(BF16)
    carry = jnp.zeros((N_EXPERTS, 1), F32)
    for ch in range(n // SEL_CHUNK):
        sl = slice(ch * SEL_CHUNK, (ch + 1) * SEL_CHUNK)
        m = mask_ref[:, sl]
        out_ref[:, sl] = jnp.dot(m.astype(BF16), tri, preferred_element_type=F32) + carry
        carry = carry + jnp.sum(m, axis=1, keepdims=True)
    return carry


def _rank_kernel(aff_ref, pos_ref, rank_ref, mask_ref, *, n, cap):
    as_float = lambda bits: lax.bitcast_convert_type(bits, F32)

    def bit_step(i, thr):
        cand = thr | (jnp.int32(1) << (30 - i))
        cnt = jnp.sum((aff_ref[...] >= as_float(cand)).astype(jnp.int32), axis=1, keepdims=True)
        return jnp.where(cnt >= cap, cand, thr)

    thr = lax.fori_loop(0, 31, bit_step, jnp.zeros((N_EXPERTS, 1), jnp.int32))
    aff = aff_ref[...]
    gt = aff >= as_float(thr + 1)
    eq = (aff >= as_float(thr)) & jnp.logical_not(gt)
    need = cap - jnp.sum(gt.astype(jnp.int32), axis=1, keepdims=True)
    mask_ref[...] = eq.astype(F32)
    _cumsum_excl(mask_ref, rank_ref, n)
    sel = gt | (eq & (rank_ref[...] < need.astype(F32)))
    mask_ref[...] = sel.astype(F32)
    _cumsum_excl(mask_ref, rank_ref, n)
    pos_ref[...] = jnp.where(sel, rank_ref[...].astype(jnp.int32), -1)


def _invert_kernel(rng_ref, pos_ref, aff_ref, idx_ref, w_ref, part_i_ref, part_w_ref, *, n, cap):
    sblk = LANES
    nblk = cap // sblk
    lane_e = lax.broadcasted_iota(jnp.int32, (cap, N_EXPERTS), 1)
    lane_t = lax.broadcasted_iota(jnp.int32, (sblk, LANES), 1)
    idx_ref[...] = jnp.zeros(idx_ref.shape, jnp.int32)
    w_ref[...] = jnp.zeros(w_ref.shape, F32)
    for e in range(N_EXPERTS):
        def slot_block(sb, carry, e=e):
            s0 = pl.multiple_of(sb * sblk, sblk)
            slot = lax.broadcasted_iota(jnp.int32, (sblk, LANES), 0) + s0

            def chunk(tcn, acc):
                acc_i, acc_w = acc
                t0 = pl.multiple_of(tcn * LANES, LANES)
                hit = pos_ref[e:e + 1, pl.ds(t0, LANES)] == slot
                acc_i = acc_i + jnp.where(hit, lane_t + t0, 0)
                acc_w = acc_w + jnp.where(hit, aff_ref[e:e + 1, pl.ds(t0, LANES)], 0.0)
                return acc_i, acc_w

            acc_i, acc_w = lax.fori_loop(
                rng_ref[e, sb], rng_ref[e, nblk + sb], chunk,
                (jnp.zeros((sblk, LANES), jnp.int32), jnp.zeros((sblk, LANES), F32)))
            part_i_ref[pl.ds(s0, sblk), :] = acc_i
            part_w_ref[pl.ds(s0, sblk), :] = acc_w
            return carry

        lax.fori_loop(0, nblk, slot_block, 0)
        col_i = jnp.sum(part_i_ref[...], axis=1, keepdims=True)
        col_w = jnp.sum(part_w_ref[...], axis=1, keepdims=True)
        idx_ref[...] = jnp.where(lane_e == e, col_i, idx_ref[...])
        w_ref[...] = jnp.where(lane_e == e, col_w, w_ref[...])


def expert_select(aff_t):
    e, n = aff_t.shape
    cap = EC_FACTOR * n // N_EXPERTS
    full = pl.BlockSpec((e, n), lambda i: (0, 0))
    pos, rank = pl.pallas_call(
        functools.partial(_rank_kernel, n=n, cap=cap),
        grid=(1,),
        in_specs=[full],
        out_specs=[full, full],
        out_shape=[jax.ShapeDtypeStruct((e, n), jnp.int32), jax.ShapeDtypeStruct((e, n), F32)],
        scratch_shapes=[pltpu.VMEM((e, n), F32)],
        compiler_params=_params("arbitrary"),
        name="expert_rank",
    )(aff_t)
    rank = rank.astype(jnp.int32)
    total = jnp.full((e, 1), cap, jnp.int32)
    bounds = jnp.concatenate([rank[:, ::LANES], total], axis=1)
    blk0 = jnp.arange(0, cap, LANES, dtype=jnp.int32)[None, :, None]
    first = jnp.sum(bounds[:, None, 1:] <= blk0, axis=2, dtype=jnp.int32)
    last = jnp.sum(bounds[:, None, :-1] < blk0 + LANES, axis=2, dtype=jnp.int32)
    chunk_ranges = jnp.concatenate([jnp.minimum(first, n // LANES - 1), last], axis=1)
    idx_t, w_t = pl.pallas_call(
        functools.partial(_invert_kernel, n=n, cap=cap),
        grid_spec=pltpu.PrefetchScalarGridSpec(
            num_scalar_prefetch=1,
            grid=(1,),
            in_specs=[pl.BlockSpec((e, n), lambda i, cs: (0, 0)), pl.BlockSpec((e, n), lambda i, cs: (0, 0))],
            out_specs=[pl.BlockSpec((cap, e), lambda i, cs: (0, 0)), pl.BlockSpec((cap, e), lambda i, cs: (0, 0))],
            scratch_shapes=[pltpu.VMEM((cap, LANES), jnp.int32), pltpu.VMEM((cap, LANES), F32)]),
        out_shape=[jax.ShapeDtypeStruct((cap, e), jnp.int32), jax.ShapeDtypeStruct((cap, e), F32)],
        compiler_params=_params("arbitrary"),
        name="expert_invert",
    )(chunk_ranges, pos, aff_t)
    return idx_t.T, w_t.T, rank


FF_TILE = 256
FFN_ROWS = 512
Y_PAD = 64


TOK_SUB = D_MODEL // LANES


def _tok_copy(src_hbm, tok, xe_ref, slot, sem):
    return pltpu.make_async_copy(src_hbm.at[tok], xe_ref.at[:, slot, :], sem)


def _ffn_kernel(idx_ref, hp_hbm, hs_hbm, w1_hbm, w3_hbm, w2_hbm, y_ref, xe_ref, xb_ref, w1_buf, w3_buf, w2_buf,
                sem, wsem, *, cap, gslots, per_tile, nf):
    e = pl.program_id(0)
    ne = pl.num_programs(0)
    rows = 2 * cap
    nrc = rows // FFN_ROWS

    def start_pair(ex, s):
        _tok_copy(hp_hbm, idx_ref[ex * gslots + s], xe_ref, s, sem).start()
        _tok_copy(hs_hbm, idx_ref[(ne + ex) * gslots + s], xe_ref, gslots + s, sem).start()

    def wait_all():
        def body(s, c):
            _tok_copy(hp_hbm, 0, xe_ref, s, sem).wait()
            _tok_copy(hs_hbm, 0, xe_ref, gslots + s, sem).wait()
            return c

        lax.fori_loop(0, gslots, body, 0, unroll=8)

    def weight_copies(ex, f, slot):
        col = pl.ds(pl.multiple_of(f * FF_TILE, FF_TILE), FF_TILE)
        return (pltpu.make_async_copy(w1_hbm.at[ex, :, col], w1_buf.at[slot], wsem.at[slot]),
                pltpu.make_async_copy(w3_hbm.at[ex, :, col], w3_buf.at[slot], wsem.at[slot]),
                pltpu.make_async_copy(w2_hbm.at[ex, col, :], w2_buf.at[slot], wsem.at[slot]))

    @pl.when(e == 0)
    def _():
        def body(s, c):
            start_pair(0, s)
            return c

        lax.fori_loop(0, gslots, body, 0, unroll=8)
        for cp in weight_copies(0, 0, 0):
            cp.start()

    wait_all()
    for g in range(2):
        for j in range(TOK_SUB):
            xb_ref[g * cap:(g + 1) * cap, j * LANES:(j + 1) * LANES] = (
                xe_ref[j, g * gslots:g * gslots + cap, :].astype(BF16))
    y_ref[0] = jnp.zeros(y_ref.shape[1:], F32)

    nxt = jnp.minimum(e + 1, ne - 1)

    def ff_tile(f, carry):
        t = e * nf + f
        slot = lax.rem(t, 2)
        last = f == nf - 1
        for cp in weight_copies(jnp.where(last, nxt, e), jnp.where(last, 0, f + 1), 1 - slot):
            cp.start()
        for cp in weight_copies(e, f, slot):
            cp.wait()
        w1b = w1_buf[slot].astype(BF16)
        w3b = w3_buf[slot].astype(BF16)
        w2b = w2_buf[slot].astype(BF16)
        for rc in range(nrc):
            rs = slice(rc * FFN_ROWS, (rc + 1) * FFN_ROWS)
            xb = xb_ref[rs, :]
            h1 = jnp.dot(xb, w1b, preferred_element_type=F32)
            h3 = jnp.dot(xb, w3b, preferred_element_type=F32)
            hid = (_silu(h1) * h3).astype(BF16)
            y_ref[0, rs, :] += jnp.dot(hid, w2b, preferred_element_type=F32)
            for k in range(rc * per_tile // nrc, (rc + 1) * per_tile // nrc):
                start_pair(nxt, f * per_tile + k)
        return carry

    lax.fori_loop(0, nf, ff_tile, 0)

    @pl.when(e == ne - 1)
    def _():
        wait_all()
        for cp in weight_copies(e, 0, lax.rem(ne * nf, 2)):
            cp.wait()


def expert_ffn(idx_flat, h2p, h2s, w1, w3, w2, *, cap, gslots):
    ne, d, ff = w1.shape
    rows = 2 * cap
    nf = ff // FF_TILE
    per_tile = gslots // nf
    assert nf * FF_TILE == ff and per_tile * nf == gslots and gslots >= cap
    any_spec = pl.BlockSpec(memory_space=pl.ANY)
    grid_spec = pltpu.PrefetchScalarGridSpec(
        num_scalar_prefetch=1,
        grid=(ne,),
        in_specs=[any_spec] * 5,
        out_specs=pl.BlockSpec((1, rows + Y_PAD, d), lambda e, idx: (e, 0, 0)),
        scratch_shapes=[pltpu.VMEM((TOK_SUB, 2 * gslots, LANES), F32), pltpu.VMEM((rows, d), BF16),
                        pltpu.VMEM((2, d, FF_TILE), F32), pltpu.VMEM((2, d, FF_TILE), F32),
                        pltpu.VMEM((2, FF_TILE, d), F32),
                        pltpu.SemaphoreType.DMA(()), pltpu.SemaphoreType.DMA((2,))],
    )
    return pl.pallas_call(
        functools.partial(_ffn_kernel, cap=cap, gslots=gslots, per_tile=per_tile, nf=nf),
        grid_spec=grid_spec,
        out_shape=jax.ShapeDtypeStruct((ne, rows + Y_PAD, d), F32),
        compiler_params=_params("arbitrary"),
        name="expert_ffn",
    )(idx_flat, h2p, h2s, w1, w3, w2)


COMB_TILE = 1024


COMB_GROUP = 8
STAGE_ROWS = COMB_TILE + Y_PAD


COMB_DEPTH = 4


def _combine_kernel(doff_ref, off_ref, wt_ref, x1_ref, mod_ref, fg_ref, y_hbm, o_ref, acc_ref, stage_ref, sem,
                    *, row0, cap):
    i = pl.program_id(0)
    nspan = pl.num_programs(0) * N_EXPERTS
    sub = TOK_SUB
    ahead = COMB_DEPTH - 1

    def span(p):
        e = lax.bitwise_and(p, N_EXPERTS - 1)
        ti = lax.shift_right_logical(p, N_EXPERTS.bit_length() - 1)
        a = off_ref[e, ti]
        cnt = off_ref[e, ti + 1] - a
        a8 = lax.shift_left(lax.shift_right_logical(a, 3), 3)
        lead = a - a8
        nchunk = lax.shift_right_logical(cnt + lead + (Y_PAD - 1), Y_PAD.bit_length() - 1)
        return e, a, cnt, a8, lead, nchunk

    def fetch(p, start):
        e, _, _, a8, _, nchunk = span(p)
        slot = lax.bitwise_and(p, COMB_DEPTH - 1)

        def body(ci, c):
            src0 = pl.multiple_of(row0 + a8 + ci * Y_PAD, 8)
            dst0 = pl.multiple_of(ci * Y_PAD, Y_PAD)
            for j in range(sub):
                cp = pltpu.make_async_copy(
                    y_hbm.at[e, pl.ds(src0, Y_PAD), pl.ds(j * LANES, LANES)],
                    stage_ref.at[pl.ds(pl.multiple_of(slot * STAGE_ROWS + dst0, Y_PAD), Y_PAD), j, :], sem.at[slot])
                cp.start() if start else cp.wait()
            return c

        lax.fori_loop(0, nchunk, body, 0)

    @pl.when(i == 0)
    def _():
        for p in range(ahead):
            fetch(jnp.int32(p), True)

    acc_ref[...] = jnp.zeros(acc_ref.shape, F32)

    def add_rows(tab0, row0_stage, count):
        dsts = [pl.ds(pl.multiple_of(doff_ref[tab0 + u], sub), sub) for u in range(count)]
        vals = [stage_ref[row0_stage + u] * wt_ref[tab0 + u] for u in range(count)]
        olds = [acc_ref[d, :] for d in dsts]
        for d, old, v in zip(dsts, olds, vals):
            acc_ref[d, :] = old + v

    def per_expert(ei, c):
        p = i * N_EXPERTS + ei

        @pl.when(p + ahead < nspan)
        def _():
            fetch(p + ahead, True)

        fetch(p, False)
        e, a, cnt, _, lead, _ = span(p)
        tab = e * cap + a
        srow = lax.bitwise_and(p, COMB_DEPTH - 1) * STAGE_ROWS + lead
        full = lax.shift_right_logical(cnt, 3)

        def group(g, cc):
            add_rows(tab + g * COMB_GROUP, srow + g * COMB_GROUP, COMB_GROUP)
            return cc

        lax.fori_loop(0, full, group, 0)

        def single(s, cc):
            add_rows(tab + s, srow + s, 1)
            return cc

        lax.fori_loop(full * COMB_GROUP, cnt, single, 0)
        return c

    lax.fori_loop(0, N_EXPERTS, per_expert, 0)
    g2 = mod_ref[0][:, 5 * D_MODEL:6 * D_MODEL]
    acc = jnp.concatenate([acc_ref[pl.ds(j, COMB_TILE, stride=sub), :] for j in range(sub)], axis=1)
    x = x1_ref[...] + g2 * acc
    o_ref[...] = _rms(x, D_MODEL) * fg_ref[...]


def combine(idx, offs, wts, x1, mods3, final_g, y_all, *, row0, mod_row):
    n, d = x1.shape
    assert COMB_GROUP == 8 and COMB_DEPTH & (COMB_DEPTH - 1) == 0
    doff = (idx % COMB_TILE) * TOK_SUB
    grid_spec = pltpu.PrefetchScalarGridSpec(
        num_scalar_prefetch=3,
        grid=(n // COMB_TILE,),
        in_specs=[pl.BlockSpec((COMB_TILE, d), lambda i, *_: (i, 0)),
                  pl.BlockSpec((1, 1, N_MOD * d), lambda i, *_: (mod_row(i), 0, 0)),
                  pl.BlockSpec((1, d), lambda i, *_: (0, 0)),
                  pl.BlockSpec(memory_space=pl.ANY)],
        out_specs=pl.BlockSpec((COMB_TILE, d), lambda i, *_: (i, 0)),
        scratch_shapes=[pltpu.VMEM((COMB_TILE * TOK_SUB, LANES), F32),
                        pltpu.VMEM((COMB_DEPTH * STAGE_ROWS, TOK_SUB, LANES), F32),
                        pltpu.SemaphoreType.DMA((COMB_DEPTH,))],
    )
    return pl.pallas_call(
        functools.partial(_combine_kernel, row0=row0, cap=idx.shape[1]),
        grid_spec=grid_spec,
        out_shape=jax.ShapeDtypeStruct((n, d), F32),
        compiler_params=_params("arbitrary"),
        name="combine",
    )(doff.reshape(-1), offs, wts.reshape(-1), x1, mods3, final_g.reshape(1, d), y_all)


def _tile_offsets(rank, cap):
    total = jnp.full((rank.shape[0], 1), cap, jnp.int32)
    return jnp.concatenate([rank[:, ::COMB_TILE], total], axis=1)


def kernel(x_prompt, x_sample, cache_k, cache_v, c, c_ctx, norm1_g, norm2_g, w_ada, b_ada, w_in, w_out,
           out_g_attn, out_g_hyena, rpb, conv_w, conv_b, filt_w1, filt_b1, filt_w2, filt_b2, filt_w3,
           filt_freq, hyena_bias, w_router, w1, w3, w2, final_g):
    depth = norm1_g.shape[0]
    assert depth == 1
    l = 0
    bp, tp, d = x_prompt.shape
    bs, ts, _ = x_sample.shape

    cond = jnp.zeros((MOD_ROWS, d), F32).at[0].set(c_ctx).at[1:1 + bs].set(c)
    mods3 = ada_mod(cond, w_ada[l], b_ada[l]).reshape(MOD_ROWS, 1, N_MOD * d)

    w_in_b = w_in[l].astype(BF16)
    w_out_b = w_out[l].astype(BF16)
    w_router_t = w_router[l].T
    rpb_ext = jnp.pad(rpb[l], ((0, 0), (0, 0), (RPB_PAD_L, RPB_PAD_R)), mode="edge")
    w1p = jnp.pad(filt_w1[l], ((0, LANES - FILTER_EMB), (0, 0)))

    ctx_row = lambda i: 0
    tt_s = 512
    lat_row = lambda i: 1 + i // (ts // tt_s)

    qkv_p, hy_p, state_k, state_v = in_projection(x_prompt, mods3, norm1_g[l], w_in_b, nb=2, tt=tp,
                                                  mod_row=ctx_row, with_state=True)
    qkv_s, hy_s = in_projection(x_sample, mods3, norm1_g[l], w_in_b, nb=1, tt=tt_s,
                                mod_row=lat_row, with_state=False)

    a_p = context_attention(qkv_p, nb=2)
    a_s = latent_attention(qkv_s, cache_k, cache_v, rpb_ext, l)

    def hyena(hy, length, nb):
        fwd_tab, inv_tab = _dft_tables(length)
        fwd_tab = jnp.asarray(fwd_tab)
        ka, kb = hyena_filter(length, w1p, filt_b1[l], filt_w2[l], filt_b2[l], filt_w3[l], filt_freq[l],
                              fwd_tab)
        return hyena_mixer(hy, conv_w[l], conv_b[l], fwd_tab.astype(BF16), jnp.asarray(inv_tab).astype(BF16),
                           ka, kb, hyena_bias[l], nb=nb)

    yh_p = hyena(hy_p, tp, 8)
    yh_s = hyena(hy_s, ts, 2)

    x1_p, h2_p, aff_p = out_projection(a_p, yh_p, x_prompt, mods3, out_g_attn[l], out_g_hyena[l],
                                       norm2_g[l], w_out_b, w_router_t, nb=2, tt=tp, mod_row=ctx_row)
    x1_s, h2_s, aff_s = out_projection(a_s, yh_s, x_sample, mods3, out_g_attn[l], out_g_hyena[l],
                                       norm2_g[l], w_out_b, w_router_t, nb=1, tt=tt_s, mod_row=lat_row)

    idx_p, wt_p, rank_p = expert_select(aff_p)
    idx_s, wt_s, rank_s = expert_select(aff_s)
    cap = idx_p.shape[1]
    nf = w1.shape[3] // FF_TILE
    gslots = -(-cap // (8 * nf)) * 8 * nf
    pad_to = lambda a, width: jnp.pad(a, ((0, 0), (0, width - a.shape[1])))
    idx_flat = jnp.concatenate([pad_to(idx_p, gslots), pad_to(idx_s, gslots)], axis=0).reshape(-1)
    tiles = lambda h2: h2.reshape(-1, TOK_SUB, LANES)
    y_all = expert_ffn(idx_flat, tiles(h2_p), tiles(h2_s), w1[l], w3[l], w2[l], cap=cap, gslots=gslots)

    comb_row_s = lambda i: 1 + i // (ts // COMB_TILE)
    y_p = combine(idx_p, _tile_offsets(rank_p, cap), wt_p, x1_p, mods3, final_g, y_all, row0=0, mod_row=ctx_row)
    y_s = combine(idx_s, _tile_offsets(rank_s, cap), wt_s, x1_s, mods3, final_g, y_all, row0=cap,
                  mod_row=comb_row_s)

    return (y_p.reshape(bp, tp, d), y_s.reshape(bs, ts, d),
            state_k.reshape(bp, depth, N_HEADS, tp, HEAD_DIM), state_v.reshape(bp, depth, N_HEADS, tp, HEAD_DIM))
```
